```python
import math
import jax, jax.numpy as jnp
from jax import lax
import numpy as np

D_MODEL = 1024
BATCH = 2
SEQ = 8192
DEPTH = 2
DEC_BATCH = 32
DEC_SEQ = 8
PAST_LEN = 16384
PAGE_SIZE = 128

N_MIXERS = 2
N_ATTN_LAYERS = (DEPTH + 1) // 2
N_REC_LAYERS = DEPTH // 2

N_HEADS = 16
HEAD_DIM = D_MODEL // N_HEADS
KV_HEADS = 4
GROUP = N_HEADS // KV_HEADS
KV_WIDTH = KV_HEADS * HEAD_DIM
CMP_BLOCK = 64
TOP_K = 16
WINDOW = 512
Q_BLOCK = 128
FORCED_SCORE = float(GROUP + 1)
ATTN_IN = N_HEADS * HEAD_DIM + 6 * KV_WIDTH + 3 * N_HEADS

HG_DK = 128
HG_HEADS = D_MODEL // HG_DK
HG_DV = D_MODEL // HG_HEADS
HG_CHUNK = 32
REC_IN = 4 * D_MODEL

N_GROUPS = 4
EXPERTS_PER_GROUP = 8
EXPERT_HIDDEN = D_MODEL // 4
INNER_TOP_K = 2

ALPHA = (2.0 * DEPTH) ** 0.25
BETA = (8.0 * DEPTH) ** -0.25
LN_EPS = 1e-5
RMS_EPS = 1e-6

kernel_name = "nsa_hgrn2_hmoe_deepnorm_step"


def layer_norm(x, g, b):
    xf = x.astype(jnp.float32)
    mu = xf.mean(-1, keepdims=True)
    var = jnp.square(xf - mu).mean(-1, keepdims=True)
    return ((xf - mu) * lax.rsqrt(var + LN_EPS) * g + b).astype(x.dtype)


def masked_softmax(s, mask):
    s = jnp.where(mask, s, -jnp.inf)
    m = jnp.max(s, axis=-1, keepdims=True)
    m = jnp.where(jnp.isfinite(m), m, 0.0)
    e = jnp.exp(s - m)
    return e / jnp.maximum(e.sum(-1, keepdims=True), 1e-30)


def compress_blocks(rows, alpha):
    b, n = rows.shape[:2]
    blocks = rows.reshape(b, n // CMP_BLOCK, CMP_BLOCK, *rows.shape[2:])
    return jnp.einsum('bnrchd,rch->bnchd', blocks, alpha)


def nsa_split(x, w_in):
    b, L, _ = x.shape
    p = (x @ w_in).astype(jnp.float32)
    nq = N_HEADS * HEAD_DIM
    q = p[..., :nq].reshape(b, L, KV_HEADS, GROUP, HEAD_DIM)
    kv = p[..., nq:nq + 6 * KV_WIDTH].reshape(b, L, 3, 2, KV_HEADS, HEAD_DIM)
    gates = jax.nn.sigmoid(p[..., nq + 6 * KV_WIDTH:]).reshape(b, L, KV_HEADS, GROUP, 3)
    return q, kv[:, :, 0], kv[:, :, 1], kv[:, :, 2], gates


def nsa_core(q, qpos, kcb, vcb, sel_gather, kw, vw, wpos, gates):
    scale = HEAD_DIM ** -0.5
    nb = kcb.shape[1]
    blk = jnp.arange(nb, dtype=jnp.int32)
    s_c = jnp.einsum('bqhgd,bnhd->bqhgn', q, kcb) * scale
    m_c = (blk * CMP_BLOCK + CMP_BLOCK - 1)[None, :] <= qpos[:, None]
    p_c = masked_softmax(s_c, m_c[None, :, None, None, :])
    o_c = jnp.einsum('bqhgn,bnhd->bqhgd', p_c, vcb)
    imp = p_c.sum(axis=3)
    cur = (qpos // CMP_BLOCK)[:, None]
    forced = (blk == 0) | (blk == cur) | (blk == cur - 1)
    valid = blk * CMP_BLOCK <= qpos[:, None]
    score = jnp.where(forced, FORCED_SCORE, jnp.where(valid, 0.0, -1.0))[None, :, None, :] + imp
    k_sel = min(TOP_K, nb)
    _, idx = lax.top_k(score, k_sel)
    ks, vs = sel_gather(idx)
    tok = idx[..., None] * CMP_BLOCK + jnp.arange(CMP_BLOCK, dtype=jnp.int32)
    m_s = (tok <= qpos[None, :, None, None, None]).reshape(*idx.shape[:3], 1, -1)
    s_s = jnp.einsum('bqhgd,bqhkrd->bqhgkr', q, ks).reshape(*q.shape[:4], -1) * scale
    p_s = masked_softmax(s_s, m_s).reshape(*q.shape[:4], k_sel, CMP_BLOCK)
    o_s = jnp.einsum('bqhgkr,bqhkrd->bqhgd', p_s, vs)
    dist = qpos[:, None] - wpos[None, :]
    m_w = (dist >= 0) & (dist < WINDOW) & (wpos >= 0)[None, :]
    s_w = jnp.einsum('bqhgd,bkhd->bqhgk', q, kw) * scale
    p_w = masked_softmax(s_w, m_w[None, :, None, None, :])
    o_w = jnp.einsum('bqhgk,bkhd->bqhgd', p_w, vw)
    return gates[..., 0:1] * o_c + gates[..., 1:2] * o_s + gates[..., 2:3] * o_w


def nsa_prompt(x, w_in, cmp_alpha, w_out):
    b, L, _ = x.shape
    q, kv_c, kv_s, kv_w, gates = nsa_split(x, w_in)
    cblk = compress_blocks(kv_c, cmp_alpha)
    kcb, vcb = cblk[:, :, 0], cblk[:, :, 1]
    sel_blk = kv_s.reshape(b, L // CMP_BLOCK, CMP_BLOCK, 2, KV_HEADS, HEAD_DIM).transpose(0, 4, 1, 2, 3, 5)
    bi = jnp.arange(b)[:, None, None, None]
    hi = jnp.arange(KV_HEADS)[None, None, :, None]

    def sel_gather(idx):
        g = sel_blk[bi, hi, idx]
        return g[..., 0, :], g[..., 1, :]

    kw_pad = jnp.pad(kv_w, ((0, 0), (WINDOW, 0), (0, 0), (0, 0), (0, 0)))

    def one_block(i):
        s0 = i * Q_BLOCK
        qb = lax.dynamic_slice_in_dim(q, s0, Q_BLOCK, axis=1)
        gb = lax.dynamic_slice_in_dim(gates, s0, Q_BLOCK, axis=1)
        qpos = s0 + jnp.arange(Q_BLOCK, dtype=jnp.int32)
        wb = lax.dynamic_slice_in_dim(kw_pad, s0, WINDOW + Q_BLOCK, axis=1)
        wpos = s0 - WINDOW + jnp.arange(WINDOW + Q_BLOCK, dtype=jnp.int32)
        return nsa_core(qb, qpos, kcb, vcb, sel_gather, wb[:, :, 0], wb[:, :, 1], wpos, gb)

    o = lax.map(one_block, jnp.arange(L // Q_BLOCK, dtype=jnp.int32))
    o = o.transpose(1, 0, 2, 3, 4, 5).reshape(b, L, D_MODEL)
    y = o.astype(x.dtype) @ w_out
    n_win = min(WINDOW, L)
    return y, kv_c.astype(x.dtype), kv_s.astype(x.dtype), kv_w[:, L - n_win:].astype(x.dtype)


def nsa_sample(x, cache_cmp_kv, cache_sel_kv, cache_win_kv, page_table, li, w_in, cmp_alpha, w_out):
    b, L, _ = x.shape
    past = page_table.shape[1] * PAGE_SIZE
    q, kv_c, kv_s, kv_w, gates = nsa_split(x, w_in)
    qpos = past + jnp.arange(L, dtype=jnp.int32)
    past_c = cache_cmp_kv[li, page_table].reshape(b, past, 2, KV_HEADS, HEAD_DIM).astype(jnp.float32)
    n_new = -(-L // CMP_BLOCK) * CMP_BLOCK
    new_c = jnp.pad(kv_c, ((0, 0), (0, n_new - L), (0, 0), (0, 0), (0, 0)))
    cblk = jnp.concatenate([compress_blocks(past_c, cmp_alpha), compress_blocks(new_c, cmp_alpha)], axis=1)
    bi = jnp.arange(b)[:, None, None, None, None]
    hi = jnp.arange(KV_HEADS)[None, None, :, None, None]

    def sel_gather(idx):
        tok = idx[..., None] * CMP_BLOCK + jnp.arange(CMP_BLOCK, dtype=jnp.int32)
        in_past = tok < past
        tp = jnp.minimum(tok, past - 1)
        page = page_table[bi, tp // PAGE_SIZE]
        from_cache = cache_sel_kv[li, page, tp % PAGE_SIZE, :, hi]
        tn = jnp.clip(tok - past, 0, L - 1)
        from_new = kv_s[bi, tn, :, hi]
        g = jnp.where(in_past[..., None, None], from_cache.astype(jnp.float32), from_new)
        return g[..., 0, :], g[..., 1, :]

    win_buf = cache_win_kv[li].astype(jnp.float32)
    nw = win_buf.shape[1]
    wkv = jnp.concatenate([win_buf, kv_w], axis=1)
    wpos = past - nw + jnp.arange(nw + L, dtype=jnp.int32)
    o = nsa_core(q, qpos, cblk[:, :, 0], cblk[:, :, 1], sel_gather, wkv[:, :, 0], wkv[:, :, 1], wpos, gates)
    y = o.reshape(b, L, D_MODEL).astype(x.dtype) @ w_out
    return y, kv_c.astype(x.dtype), kv_s.astype(x.dtype), wkv[:, L:].astype(x.dtype)


def hgrn_scan(q, k, v, logf, s0):
    b, L = q.shape[:2]
    C = HG_CHUNK
    n = -(-L // C)
    pad = n * C - L

    def to_chunks(a):
        a = jnp.pad(a, ((0, 0), (0, pad), (0, 0), (0, 0)))
        return a.reshape(b, n, C, *a.shape[2:]).transpose(1, 0, 3, 2, 4)

    qc, kc, vc, fc = to_chunks(q), to_chunks(k), to_chunks(v), to_chunks(logf)
    causal = jnp.tril(jnp.ones((C, C), dtype=bool))
    mid = C // 2

    def step(S, inp):
        qi, ki, vi, fi = inp
        cum = jnp.cumsum(fi, axis=2)
        ref = cum[:, :, mid:mid + 1]
        a = jnp.einsum('bhtk,bhsk->bhts', qi * jnp.exp(cum - ref), ki * jnp.exp(ref - cum))
        a = jnp.where(causal, a, 0.0)
        o = jnp.einsum('bhts,bhsv->bhtv', a, vi) + jnp.einsum('bhtk,bhkv->bhtv', qi * jnp.exp(cum), S)
        last = cum[:, :, -1:]
        S = jnp.exp(last)[:, :, 0, :, None] * S + jnp.einsum('bhsk,bhsv->bhkv', ki * jnp.exp(last - cum), vi)
        return S, o

    S, o = lax.scan(step, s0, (qc, kc, vc, fc))
    o = o.transpose(1, 0, 3, 2, 4).reshape(b, n * C, q.shape[2], v.shape[-1])[:, :L]
    return o, S


def hgrn_mixer(x, s0, w_in, lb, norm_g, w_out):
    b, L, _ = x.shape
    p = (x @ w_in).astype(jnp.float32)
    zq, zf, zi, zg = jnp.split(p, 4, axis=-1)
    log_f = jnp.logaddexp(jnp.log(lb), jnp.log1p(-lb) + jax.nn.log_sigmoid(zf))
    k = -jnp.expm1(log_f)
    heads = lambda a: a.reshape(b, L, HG_HEADS, -1)
    o, S = hgrn_scan(heads(zq), heads(k), heads(zi), heads(log_f), s0.astype(jnp.float32))
    o = o * lax.rsqrt(jnp.mean(o * o, -1, keepdims=True) + RMS_EPS)
    o = o.reshape(b, L, D_MODEL) * norm_g * jax.nn.silu(zg)
    return o.astype(x.dtype) @ w_out, S


def hier_moe(x, w_rc, b_rc, w_re, b_re, w_gate, w_up, w_down):
    shp = x.shape
    t = x.reshape(-1, D_MODEL)
    n_tok = t.shape[0]
    lc = (t @ w_rc).astype(jnp.float32) + b_rc
    grp = jnp.argmax(lc, axis=-1)
    pg = jnp.max(jax.nn.softmax(lc, axis=-1), axis=-1)
    le = jnp.einsum('td,dge->tge', t, w_re).astype(jnp.float32) + b_re
    le = le[jnp.arange(n_tok), grp]
    tv, ti = lax.top_k(le, INNER_TOP_K)
    wi = jax.nn.softmax(tv, axis=-1) * pg[:, None]
    w_exp = jnp.sum(jax.nn.one_hot(ti, EXPERTS_PER_GROUP, dtype=jnp.float32) * wi[..., None], axis=1)
    gmask = jax.nn.one_hot(grp, N_GROUPS, dtype=jnp.float32)
    out = jnp.zeros(t.shape, jnp.float32)
    for g in range(N_GROUPS):
        h = jax.nn.silu(jnp.einsum('td,edf->tef', t, w_gate[g])) * jnp.einsum('td,edf->tef', t, w_up[g])
        cw = (w_exp * gmask[:, g:g + 1]).astype(h.dtype)
        out = out + jnp.einsum('tef,efd->td', h * cw[..., None], w_down[g])
    return out.astype(x.dtype).reshape(shp)


def setup_inputs(seed: int = 0) -> dict:
    key = jax.random.key(seed)
    ks = jax.random.split(key, 24)
    n_pages = PAST_LEN // PAGE_SIZE
    n_pool = (DEC_BATCH * n_pages * 5) // 4
    win_buf = min(WINDOW, PAST_LEN)
    nrm = lambda k, shape, s: jax.random.normal(k, shape, jnp.float32) * s
    x_prompt = nrm(ks[0], (BATCH, SEQ, D_MODEL), 1.0)
    x_sample = nrm(ks[1], (DEC_BATCH, DEC_SEQ, D_MODEL), 1.0)
    cache_cmp_kv = nrm(ks[2], (N_ATTN_LAYERS, n_pool, PAGE_SIZE, 2, KV_HEADS, HEAD_DIM), 1.0)
    cache_sel_kv = nrm(ks[3], (N_ATTN_LAYERS, n_pool, PAGE_SIZE, 2, KV_HEADS, HEAD_DIM), 1.0)
    cache_win_kv = nrm(ks[4], (N_ATTN_LAYERS, DEC_BATCH, win_buf, 2, KV_HEADS, HEAD_DIM), 1.0)
    state_hgrn = nrm(ks[5], (N_REC_LAYERS, DEC_BATCH, HG_HEADS, HG_DK, HG_DV), 0.3)
    page_table = jax.random.permutation(ks[6], n_pool)[:DEC_BATCH * n_pages].reshape(DEC_BATCH, n_pages).astype(jnp.int32)
    kv_scale = jnp.ones((3, 2, KV_WIDTH), jnp.float32).at[:, 1].set(BETA).reshape(-1)
    attn_col = jnp.concatenate([jnp.ones((N_HEADS * HEAD_DIM,), jnp.float32), kv_scale, jnp.ones((3 * N_HEADS,), jnp.float32)])
    attn_w_in = nrm(ks[7], (N_ATTN_LAYERS, D_MODEL, ATTN_IN), D_MODEL ** -0.5) * attn_col
    alpha_scale = jnp.array([CMP_BLOCK ** -0.5, 1.0 / CMP_BLOCK], jnp.float32)[None, :, None]
    attn_cmp_alpha = (1.0 + nrm(ks[8], (N_ATTN_LAYERS, CMP_BLOCK, 2, KV_HEADS), 0.1)) * alpha_scale
    attn_w_out = nrm(ks[9], (N_ATTN_LAYERS, D_MODEL, D_MODEL), D_MODEL ** -0.5 * BETA)
    rec_col = jnp.concatenate([jnp.ones((2 * D_MODEL,), jnp.float32), jnp.full((D_MODEL,), BETA, jnp.float32), jnp.ones((D_MODEL,), jnp.float32)])
    rec_w_in = nrm(ks[10], (N_REC_LAYERS, D_MODEL, REC_IN), D_MODEL ** -0.5) * rec_col
    rec_lb_logits = nrm(ks[11], (DEPTH, D_MODEL), 0.5)
    rec_norm_g = 1.0 + nrm(ks[12], (N_REC_LAYERS, D_MODEL), 0.1)
    rec_w_out = nrm(ks[13], (N_REC_LAYERS, D_MODEL, D_MODEL), D_MODEL ** -0.5 * BETA)
    ln_g = 1.0 + nrm(ks[14], (DEPTH, 2, D_MODEL), 0.1)
    ln_b = nrm(ks[15], (DEPTH, 2, D_MODEL), 0.02)
    moe_w_router_c = nrm(ks[16], (DEPTH, D_MODEL, N_GROUPS), D_MODEL ** -0.5)
    moe_b_router_c = nrm(ks[17], (DEPTH, N_GROUPS), 0.01)
    moe_w_router_e = nrm(ks[18], (DEPTH, D_MODEL, N_GROUPS, EXPERTS_PER_GROUP), D_MODEL ** -0.5)
    moe_b_router_e = nrm(ks[19], (DEPTH, N_GROUPS, EXPERTS_PER_GROUP), 0.01)
    ew = (DEPTH, N_GROUPS, EXPERTS_PER_GROUP, D_MODEL, EXPERT_HIDDEN)
    moe_w_gate = nrm(ks[20], ew, D_MODEL ** -0.5)
    moe_w_up = nrm(ks[21], ew, D_MODEL ** -0.5 * BETA)
    moe_w_down = nrm(ks[22], (DEPTH, N_GROUPS, EXPERTS_PER_GROUP, EXPERT_HIDDEN, D_MODEL), EXPERT_HIDDEN ** -0.5 * BETA)
    return {"x_prompt": x_prompt, "x_sample": x_sample, "cache_cmp_kv": cache_cmp_kv, "cache_sel_kv": cache_sel_kv,
            "cache_win_kv": cache_win_kv, "state_hgrn": state_hgrn, "page_table": page_table,
            "attn_w_in": attn_w_in, "attn_cmp_alpha": attn_cmp_alpha, "attn_w_out": attn_w_out,
            "rec_w_in": rec_w_in, "rec_lb_logits": rec_lb_logits, "rec_norm_g": rec_norm_g, "rec_w_out": rec_w_out,
            "ln_g": ln_g, "ln_b": ln_b, "moe_w_router_c": moe_w_router_c, "moe_b_router_c": moe_b_router_c,
            "moe_w_router_e": moe_w_router_e, "moe_b_router_e": moe_b_router_e, "moe_w_gate": moe_w_gate,
            "moe_w_up": moe_w_up, "moe_w_down": moe_w_down}


def reference(x_prompt, x_sample, cache_cmp_kv, cache_sel_kv, cache_win_kv, state_hgrn, page_table,
              attn_w_in, attn_cmp_alpha, attn_w_out, rec_w_in, rec_lb_logits, rec_norm_g, rec_w_out,
              ln_g, ln_b, moe_w_router_c, moe_b_router_c, moe_w_router_e, moe_b_router_e,
              moe_w_gate, moe_w_up, moe_w_down):
    lb_all = jnp.cumsum(jax.nn.softmax(rec_lb_logits.astype(jnp.float32), axis=0), axis=0)
    hp, hs = x_prompt, x_sample
    cmp_p, cmp_s, sel_p, sel_s, win_p, win_s, rec_p, rec_s = [], [], [], [], [], [], [], []
    for layer in range(DEPTH):
        j = layer // N_MIXERS
        if layer % N_MIXERS == 0:
            mp, c_p, s_p, w_p = nsa_prompt(hp, attn_w_in[j], attn_cmp_alpha[j], attn_w_out[j])
            ms, c_s, s_s, w_s = nsa_sample(hs, cache_cmp_kv, cache_sel_kv, cache_win_kv, page_table, j,
                                           attn_w_in[j], attn_cmp_alpha[j], attn_w_out[j])
            cmp_p.append(c_p); cmp_s.append(c_s); sel_p.append(s_p); sel_s.append(s_s)
            win_p.append(w_p); win_s.append(w_s)
        else:
            lb = lb_all[layer] - lb_all[0]
            s0 = jnp.zeros((hp.shape[0], HG_HEADS, HG_DK, HG_DV), jnp.float32)
            mp, S_p = hgrn_mixer(hp, s0, rec_w_in[j], lb, rec_norm_g[j], rec_w_out[j])
            ms, S_s = hgrn_mixer(hs, state_hgrn[j], rec_w_in[j], lb, rec_norm_g[j], rec_w_out[j])
            rec_p.append(S_p.astype(hp.dtype)); rec_s.append(S_s.astype(hs.dtype))
        hp = layer_norm(ALPHA * hp + mp, ln_g[layer, 0], ln_b[layer, 0])
        hs = layer_norm(ALPHA * hs + ms, ln_g[layer, 0], ln_b[layer, 0])
        moe_args = (moe_w_router_c[layer], moe_b_router_c[layer], moe_w_router_e[layer], moe_b_router_e[layer],
                    moe_w_gate[layer], moe_w_up[layer], moe_w_down[layer])
        hp = layer_norm(ALPHA * hp + hier_moe(hp, *moe_args), ln_g[layer, 1], ln_b[layer, 1])
        hs = layer_norm(ALPHA * hs + hier_moe(hs, *moe_args), ln_g[layer, 1], ln_b[layer, 1])
    return (hp, hs, jnp.stack(cmp_p), jnp.stack(cmp_s), jnp.stack(sel_p), jnp.stack(sel_s),
            jnp.stack(win_p), jnp.stack(win_s), jnp.stack(rec_p), jnp.stack(rec_s))
```

```python
import functools

import jax
import jax.numpy as jnp
from jax import lax
from jax.experimental import pallas as pl
from jax.experimental.pallas import tpu as pltpu

f32, bf16, i32 = jnp.float32, jnp.bfloat16, jnp.int32

DEPTH = 2
N_HEADS, KV_HEADS, HEAD_DIM = 16, 4, 64
GROUP = N_HEADS // KV_HEADS
KV_WIDTH = KV_HEADS * HEAD_DIM
CMP_BLOCK, TOP_K, WINDOW = 64, 16, 512
FORCED_SCORE = float(GROUP + 1)
PAGE_SIZE = 128
HG_HEADS, HG_DK, HG_DV, HG_CHUNK = 8, 128, 128, 32
N_GROUPS, EXPERTS_PER_GROUP = 4, 8
ALPHA = (2.0 * DEPTH) ** 0.25
LN_EPS, RMS_EPS = 1e-5, 1e-6

NEG = -1e30
LANES = 128
VMEM_LIMIT = 56 * 1024 * 1024


def _cp(*sem):
    return pltpu.CompilerParams(dimension_semantics=sem, vmem_limit_bytes=VMEM_LIMIT)


def _sigmoid(x):
    return 1.0 / (1.0 + jnp.exp(-x))


def _log_sigmoid(x):
    return jnp.minimum(x, 0.0) - jnp.log1p(jnp.exp(-jnp.abs(x)))


def _layer_norm(z, g, b):
    mu = jnp.mean(z, axis=-1, keepdims=True)
    d = z - mu
    var = jnp.mean(d * d, axis=-1, keepdims=True)
    return d * lax.rsqrt(var + LN_EPS) * g + b


def _mm_kernel(x_ref, w_ref, o_ref):
    o_ref[...] = jnp.dot(x_ref[...].astype(bf16), w_ref[...], preferred_element_type=f32).astype(o_ref.dtype)


def _matmul(x, w, tm, name):
    m, k = x.shape
    n = w.shape[1]
    return pl.pallas_call(
        _mm_kernel, grid=(m // tm,),
        in_specs=[pl.BlockSpec((tm, k), lambda i: (i, 0)), pl.BlockSpec((k, n), lambda i: (0, 0))],
        out_specs=pl.BlockSpec((tm, n), lambda i: (i, 0)),
        out_shape=jax.ShapeDtypeStruct((m, n), f32), compiler_params=_cp("arbitrary"), name=name)(x, w)


def _proj_ln_kernel(o_ref, w_ref, x_ref, g_ref, b_ref, h_ref):
    y = jnp.dot(o_ref[...].astype(bf16), w_ref[...], preferred_element_type=f32)
    h_ref[...] = _layer_norm(ALPHA * x_ref[...] + y, g_ref[...], b_ref[...])


def _proj_ln(o, w, x, g, b, tm, name):
    m, k = o.shape
    n = w.shape[1]
    row = lambda i: (i, 0)
    fix = lambda i: (0, 0)
    return pl.pallas_call(
        _proj_ln_kernel, grid=(m // tm,),
        in_specs=[pl.BlockSpec((tm, k), row), pl.BlockSpec((k, n), fix), pl.BlockSpec((tm, n), row),
                  pl.BlockSpec((1, n), fix), pl.BlockSpec((1, n), fix)],
        out_specs=pl.BlockSpec((tm, n), row),
        out_shape=jax.ShapeDtypeStruct((m, n), f32), compiler_params=_cp("arbitrary"), name=name)(
            o, w, x, g.reshape(1, n), b.reshape(1, n))


def _compress_kernel(x_ref, a_ref, o_ref):
    nb = x_ref.shape[0] // CMP_BLOCK
    y = x_ref[...].reshape(nb, CMP_BLOCK, x_ref.shape[1]) * a_ref[...][None]
    o_ref[...] = jnp.sum(y, axis=1)


def _compress(rows, alpha_exp, tm, name):
    r, w = rows.shape
    return pl.pallas_call(
        _compress_kernel, grid=(r // tm,),
        in_specs=[pl.BlockSpec((tm, w), lambda i: (i, 0)), pl.BlockSpec((CMP_BLOCK, w), lambda i: (0, 0))],
        out_specs=pl.BlockSpec((tm // CMP_BLOCK, w), lambda i: (i, 0)),
        out_shape=jax.ShapeDtypeStruct((r // CMP_BLOCK, w), f32), compiler_params=_cp("arbitrary"), name=name)(
            rows, alpha_exp)


def _compress_paged_kernel(pt_ref, *refs, n_pg):
    del pt_ref
    pages, a_ref, o_ref = refs[:n_pg], refs[n_pg], refs[n_pg + 1]
    x = jnp.concatenate([r[0, 0] for r in pages], axis=0) * a_ref[...]
    nb = x.shape[0] // CMP_BLOCK
    o_ref[0] = jnp.sum(x.reshape(nb, CMP_BLOCK, x.shape[1]), axis=1)


def _compress_paged(cache, layer, page_table, alpha_exp, n_pg, name):
    nb_, n_pages = page_table.shape
    w = cache.shape[-1]
    bpp = PAGE_SIZE // CMP_BLOCK
    a_t = jnp.tile(alpha_exp, (n_pg * bpp, 1))

    def page_map(b, i, pt, k):
        return (layer, pt[b, i * n_pg + k], 0, 0)

    grid_spec = pltpu.PrefetchScalarGridSpec(
        num_scalar_prefetch=1, grid=(nb_, n_pages // n_pg),
        in_specs=[pl.BlockSpec((1, 1, PAGE_SIZE, w), functools.partial(page_map, k=k)) for k in range(n_pg)]
        + [pl.BlockSpec((n_pg * PAGE_SIZE, w), lambda b, i, pt: (0, 0))],
        out_specs=pl.BlockSpec((1, n_pg * bpp, w), lambda b, i, pt: (b, i, 0)))
    return pl.pallas_call(
        functools.partial(_compress_paged_kernel, n_pg=n_pg), grid_spec=grid_spec,
        out_shape=jax.ShapeDtypeStruct((nb_, n_pages * bpp, w), f32),
        compiler_params=_cp("arbitrary", "arbitrary"), name=name)(page_table, *([cache] * n_pg), a_t)


def _select_bias(score, k_sel):
    rowf = lax.broadcasted_iota(i32, score.shape, 0).astype(f32)
    picked = jnp.zeros(score.shape, f32)
    work = score
    for _ in range(k_sel):
        mx = jnp.max(work, axis=0, keepdims=True)
        first = jnp.min(jnp.where(work == mx, rowf, 1e9), axis=0, keepdims=True)
        pick = rowf == first
        picked = jnp.where(pick, 1.0, picked)
        work = jnp.where(pick, -jnp.inf, work)
    return jnp.where(picked > 0.0, 0.0, NEG)


def _masked_softmax_cols(s, mask):
    sm = jnp.where(mask, s, NEG)
    m = jnp.max(sm, axis=0, keepdims=True)
    e = jnp.where(mask, jnp.exp(sm - m), 0.0)
    return e / jnp.maximum(jnp.sum(e, axis=0, keepdims=True), 1e-30)


def _expand_block_rows(bias, reps):
    strips = [jnp.broadcast_to(bias[j:j + 1, :], (CMP_BLOCK, bias.shape[1])) for j in range(bias.shape[0])]
    full = jnp.concatenate(strips, axis=0)
    return full if reps == 1 else jnp.concatenate([full] * reps, axis=1)


def _nsa_prompt_kernel(q_ref, g_ref, kc_ref, vct_ref, ks_ref, vst_ref, kw_ref, vwt_ref, o_ref, sel_ref, *, tq, tk):
    s0 = pl.program_id(1) * tq
    nb = kc_ref.shape[2]
    c = GROUP * tq
    q = q_ref[0, 0, 0]
    tq_pos = s0 + lax.broadcasted_iota(i32, (1, tq), 1)
    tpos = jnp.concatenate([tq_pos] * GROUP, axis=1)

    sc = jnp.dot(kc_ref[0, 0], q, preferred_element_type=f32)
    blk_c = lax.broadcasted_iota(i32, (nb, c), 0)
    pc = _masked_softmax_cols(sc, blk_c * CMP_BLOCK + (CMP_BLOCK - 1) <= tpos)
    oc = jnp.dot(vct_ref[0, 0], pc.astype(bf16), preferred_element_type=f32)

    imp = pc[:, 0:tq]
    for g in range(1, GROUP):
        imp = imp + pc[:, g * tq:(g + 1) * tq]
    blk = lax.broadcasted_iota(i32, (nb, tq), 0)
    cur = jnp.right_shift(tq_pos, 6)
    forced = (blk == 0) | (blk == cur) | (blk == cur - 1)
    base = jnp.where(forced, FORCED_SCORE, jnp.where(blk * CMP_BLOCK <= tq_pos, 0.0, -1.0))
    sel_ref[...] = _select_bias(base + imp, min(TOP_K, nb))

    bpt = tk // CMP_BLOCK
    row_t = lax.broadcasted_iota(i32, (tk, c), 0)

    def body(kt, carry):
        m, l, acc = carry
        k0 = pl.multiple_of(kt * tk, tk)
        s = jnp.dot(ks_ref[0, 0, pl.ds(k0, tk), :], q, preferred_element_type=f32)
        b0 = pl.multiple_of(kt * bpt, bpt)
        s = s + _expand_block_rows(sel_ref[pl.ds(b0, bpt), :], GROUP)
        s = jnp.where(k0 + row_t <= tpos, s, NEG)
        m_new = jnp.maximum(m, jnp.max(s, axis=0, keepdims=True))
        a = jnp.exp(m - m_new)
        p = jnp.exp(s - m_new)
        l = a * l + jnp.sum(p, axis=0, keepdims=True)
        acc = a * acc + jnp.dot(vst_ref[0, 0, :, pl.ds(k0, tk)], p.astype(bf16), preferred_element_type=f32)
        return m_new, l, acc

    n_kt = (s0 + tq + tk - 1) // tk
    init = (jnp.full((1, c), NEG, f32), jnp.zeros((1, c), f32), jnp.zeros((HEAD_DIM, c), f32))
    _, l, acc = lax.fori_loop(0, n_kt, body, init)
    osel = acc / jnp.maximum(l, 1e-30)

    wt = WINDOW + tq
    k0w = pl.multiple_of(jnp.maximum(s0 - WINDOW, 0), LANES)
    s = jnp.dot(kw_ref[0, 0, pl.ds(k0w, wt), :], q, preferred_element_type=f32)
    dist = tpos - (k0w + lax.broadcasted_iota(i32, (wt, c), 0))
    s = jnp.where(dist >= 0, jnp.where(dist < WINDOW, s, NEG), NEG)
    p = jnp.exp(s - jnp.max(s, axis=0, keepdims=True))
    lw = jnp.sum(p, axis=0, keepdims=True)
    ow = jnp.dot(vwt_ref[0, 0, :, pl.ds(k0w, wt)], p.astype(bf16), preferred_element_type=f32)
    ow = ow / jnp.maximum(lw, 1e-30)

    gt = _sigmoid(g_ref[0, 0, 0])
    o_ref[0, 0, 0] = (gt[0:1] * oc + gt[1:2] * osel + gt[2:3] * ow).astype(o_ref.dtype)


def _nsa_prompt_attention(qt, gt, kc, vct, ks, vst, kw, vwt, tq, tk):
    b, kvh, n_t, hd, c = qt.shape
    nb, seq = kc.shape[2], ks.shape[2]
    tile = lambda bh, i: (bh // kvh, bh % kvh, i, 0, 0)
    head = lambda bh, i: (bh // kvh, bh % kvh, 0, 0)
    return pl.pallas_call(
        functools.partial(_nsa_prompt_kernel, tq=tq, tk=tk), grid=(b * kvh, n_t),
        in_specs=[pl.BlockSpec((1, 1, 1, hd, c), tile), pl.BlockSpec((1, 1, 1, 3, c), tile),
                  pl.BlockSpec((1, 1, nb, hd), head), pl.BlockSpec((1, 1, hd, nb), head),
                  pl.BlockSpec((1, 1, seq, hd), head), pl.BlockSpec((1, 1, hd, seq), head),
                  pl.BlockSpec((1, 1, seq, hd), head), pl.BlockSpec((1, 1, hd, seq), head)],
        out_specs=pl.BlockSpec((1, 1, 1, hd, c), tile),
        out_shape=jax.ShapeDtypeStruct((b, kvh, n_t, hd, c), bf16),
        scratch_shapes=[pltpu.VMEM((nb, tq), f32)],
        compiler_params=_cp("arbitrary", "arbitrary"), name="nsa_prompt_attn")(qt, gt, kc, vct, ks, vst, kw, vwt)


def _nsa_prompt(x, w_in, alpha_exp, tq=128, tk=512):
    b, seq, d = x.shape
    n_t = seq // tq
    p = _matmul(x.reshape(b * seq, d), w_in, 512, "nsa_in_proj")
    nq = N_HEADS * HEAD_DIM
    qt = (p[:, :nq] * HEAD_DIM ** -0.5).astype(bf16).reshape(b, n_t, tq, KV_HEADS, GROUP, HEAD_DIM)
    qt = qt.transpose(0, 3, 1, 5, 4, 2).reshape(b, KV_HEADS, n_t, HEAD_DIM, GROUP * tq)
    gt = p[:, nq + 6 * KV_WIDTH:nq + 6 * KV_WIDTH + 3 * N_HEADS].reshape(b, n_t, tq, KV_HEADS, GROUP, 3)
    gt = gt.transpose(0, 3, 1, 5, 4, 2).reshape(b, KV_HEADS, n_t, 3, GROUP * tq)
    kv = p[:, nq:nq + 6 * KV_WIDTH].reshape(b, seq, 3, 2, KV_HEADS, HEAD_DIM)
    kv_c, kv_s, kv_w = kv[:, :, 0], kv[:, :, 1], kv[:, :, 2]
    cblk = _compress(kv_c.reshape(b * seq, 2 * KV_WIDTH), alpha_exp, min(2048, b * seq), "nsa_compress")
    cblk = cblk.reshape(b, seq // CMP_BLOCK, 2, KV_HEADS, HEAD_DIM)
    k_rows = lambda a: a[:, :, 0].transpose(0, 2, 1, 3).astype(bf16)
    v_cols = lambda a: a[:, :, 1].transpose(0, 2, 3, 1).astype(bf16)
    ot = _nsa_prompt_attention(qt, gt, k_rows(cblk), v_cols(cblk), k_rows(kv_s), v_cols(kv_s),
                               k_rows(kv_w), v_cols(kv_w), tq, tk)
    o = ot.reshape(b, KV_HEADS, n_t, HEAD_DIM, GROUP, tq).transpose(0, 2, 5, 1, 4, 3).reshape(b * seq, d)
    return o, kv_c, kv_s, kv_w


def _dot_t0(a, b):
    return lax.dot_general(a, b, (((0,), (0,)), ((), ())), preferred_element_type=f32)


def _smp_cmp_win_kernel(q_ref, kc_ref, vc_ref, kw_ref, vw_ref, oc_ref, ow_ref, sel_ref, *, past, lq, nb_real):
    q = q_ref[0]
    nbp = kc_ref.shape[1]
    col = lax.broadcasted_iota(i32, (1, LANES), 1)
    tpos = past + jnp.bitwise_and(col, lq - 1)

    sc = jnp.dot(kc_ref[0], q, preferred_element_type=f32)
    blk = lax.broadcasted_iota(i32, (nbp, LANES), 0)
    pc = _masked_softmax_cols(sc, blk * CMP_BLOCK + (CMP_BLOCK - 1) <= tpos)
    oc_ref[0] = _dot_t0(vc_ref[0], pc.astype(bf16))

    r = lax.broadcasted_iota(i32, (LANES, LANES), 0)
    cc = lax.broadcasted_iota(i32, (LANES, LANES), 1)
    per_kv = GROUP * lq
    same = jnp.where(r // per_kv == cc // per_kv, jnp.where(r % lq == cc % lq, 1.0, 0.0), 0.0).astype(bf16)
    hi = pc.astype(bf16)
    r1 = pc - hi.astype(f32)
    mid = r1.astype(bf16)
    lo = (r1 - mid.astype(f32)).astype(bf16)
    imp = (jnp.dot(hi, same, preferred_element_type=f32) + jnp.dot(mid, same, preferred_element_type=f32)
           + jnp.dot(lo, same, preferred_element_type=f32))
    cur = jnp.right_shift(tpos, 6)
    forced = (blk == 0) | (blk == cur) | (blk == cur - 1)
    base = jnp.where(forced, FORCED_SCORE, jnp.where(blk * CMP_BLOCK <= tpos, 0.0, -1.0))
    score = jnp.where(blk < nb_real, base + imp, -3.0)
    sel_ref[0] = _select_bias(score, min(TOP_K, nb_real))

    nwt = kw_ref.shape[1]
    nw = nwt - lq
    s = jnp.dot(kw_ref[0], q, preferred_element_type=f32)
    wpos = past - nw + lax.broadcasted_iota(i32, (nwt, LANES), 0)
    dist = tpos - wpos
    ok = (dist >= 0) & (dist < WINDOW) & (wpos >= 0)
    pw = _masked_softmax_cols(s, ok)
    ow_ref[0] = _dot_t0(vw_ref[0], pw.astype(bf16))


def _smp_sel_kernel(pt_ref, *refs, n_pg, past):
    del pt_ref
    pages = refs[:n_pg]
    q_ref, sel_ref, new_ref, g_ref, oc_ref, ow_ref, o_ref, m_ref, l_ref, acc_ref = refs[n_pg:]
    i = pl.program_id(1)
    q = q_ref[0]

    @pl.when(i == 0)
    def _():
        m_ref[...] = jnp.full(m_ref.shape, NEG, f32)
        l_ref[...] = jnp.zeros(l_ref.shape, f32)
        acc_ref[...] = jnp.zeros(acc_ref.shape, f32)

    def update(k, v, bias):
        s = jnp.dot(k, q, preferred_element_type=f32) + bias
        m_new = jnp.maximum(m_ref[...], jnp.max(s, axis=0, keepdims=True))
        a = jnp.exp(m_ref[...] - m_new)
        p = jnp.exp(s - m_new)
        l_ref[...] = a * l_ref[...] + jnp.sum(p, axis=0, keepdims=True)
        acc_ref[...] = a * acc_ref[...] + _dot_t0(v, p.astype(bf16))
        m_ref[...] = m_new

    bpp = PAGE_SIZE // CMP_BLOCK
    x = jnp.concatenate([r[0, 0] for r in pages], axis=0)
    b0 = pl.multiple_of(i * (n_pg * bpp), n_pg * bpp)
    update(x[:, :KV_WIDTH].astype(bf16), x[:, KV_WIDTH:].astype(bf16),
           _expand_block_rows(sel_ref[0, pl.ds(b0, n_pg * bpp), :], 1))

    @pl.when(i == pl.num_programs(1) - 1)
    def _():
        xn = new_ref[0]
        nb_past = past // CMP_BLOCK
        lq = LANES // (KV_HEADS * GROUP)
        qoff = jnp.bitwise_and(lax.broadcasted_iota(i32, (CMP_BLOCK, LANES), 1), lq - 1)
        r = lax.broadcasted_iota(i32, (CMP_BLOCK, LANES), 0)
        bias = jnp.where(r <= qoff, jnp.broadcast_to(sel_ref[0, nb_past:nb_past + 1, :], (CMP_BLOCK, LANES)), NEG)
        update(xn[:, :KV_WIDTH].astype(bf16), xn[:, KV_WIDTH:].astype(bf16), bias)
        gt = _sigmoid(g_ref[0])
        osel = acc_ref[...] / jnp.maximum(l_ref[...], 1e-30)
        o_ref[0] = gt[0:1] * oc_ref[0] + gt[1:2] * osel + gt[2:3] * ow_ref[0]


def _nsa_sample(x, cache_cmp, cache_sel, cache_win, page_table, layer, w_in, alpha_exp, n_pg=8):
    b, lq, d = x.shape
    n_pages = page_table.shape[1]
    past = n_pages * PAGE_SIZE
    assert KV_HEADS * GROUP * lq == LANES and past % CMP_BLOCK == 0 and lq <= CMP_BLOCK
    p = _matmul(x.reshape(b * lq, d), w_in, b * lq, "nsa_in_proj_s")
    nq = N_HEADS * HEAD_DIM
    q = (p[:, :nq] * HEAD_DIM ** -0.5).reshape(b, lq, KV_HEADS, GROUP, HEAD_DIM)
    qt = q.transpose(0, 2, 4, 3, 1).reshape(b, KV_HEADS, HEAD_DIM, 1, GROUP * lq)
    qbd = (qt * jnp.eye(KV_HEADS, dtype=f32)[None, :, None, :, None]).reshape(b, KV_WIDTH, LANES).astype(bf16)
    gt = p[:, nq + 6 * KV_WIDTH:nq + 6 * KV_WIDTH + 3 * N_HEADS].reshape(b, lq, KV_HEADS, GROUP, 3)
    gt = gt.transpose(0, 4, 2, 3, 1).reshape(b, 3, LANES)
    kv = p[:, nq:nq + 6 * KV_WIDTH].reshape(b, lq, 3, 2 * KV_WIDTH)
    kv_c, kv_s, kv_w = kv[:, :, 0], kv[:, :, 1], kv[:, :, 2]
    pad_blk = lambda a: jnp.pad(a, ((0, 0), (0, CMP_BLOCK - lq), (0, 0)))

    w2 = 2 * KV_WIDTH
    pool = cache_cmp.shape[1]
    c_past = _compress_paged(cache_cmp.reshape(-1, pool, PAGE_SIZE, w2), layer, page_table, alpha_exp, n_pg,
                             "nsa_compress_paged")
    c_new = _compress(pad_blk(kv_c).reshape(b * CMP_BLOCK, w2), alpha_exp, b * CMP_BLOCK, "nsa_compress_s")
    nb_real = past // CMP_BLOCK + 1
    nbp = -(-nb_real // 8) * 8
    cblk = jnp.concatenate([c_past, c_new[:, None, :], jnp.zeros((b, nbp - nb_real, w2), f32)], axis=1)
    wkv = jnp.concatenate([cache_win[layer].reshape(b, -1, w2), kv_w], axis=1)
    nwt = wkv.shape[1]

    seq3 = lambda i: (i, 0, 0)
    oc, ow, sel = pl.pallas_call(
        functools.partial(_smp_cmp_win_kernel, past=past, lq=lq, nb_real=nb_real), grid=(b,),
        in_specs=[pl.BlockSpec((1, KV_WIDTH, LANES), seq3), pl.BlockSpec((1, nbp, KV_WIDTH), seq3),
                  pl.BlockSpec((1, nbp, KV_WIDTH), seq3), pl.BlockSpec((1, nwt, KV_WIDTH), seq3),
                  pl.BlockSpec((1, nwt, KV_WIDTH), seq3)],
        out_specs=[pl.BlockSpec((1, KV_WIDTH, LANES), seq3), pl.BlockSpec((1, KV_WIDTH, LANES), seq3),
                   pl.BlockSpec((1, nbp, LANES), seq3)],
        out_shape=[jax.ShapeDtypeStruct((b, KV_WIDTH, LANES), f32), jax.ShapeDtypeStruct((b, KV_WIDTH, LANES), f32),
                   jax.ShapeDtypeStruct((b, nbp, LANES), f32)],
        compiler_params=_cp("arbitrary"), name="nsa_sample_cmp_win")(
            qbd, cblk[:, :, :KV_WIDTH].astype(bf16), cblk[:, :, KV_WIDTH:].astype(bf16),
            wkv[:, :, :KV_WIDTH].astype(bf16), wkv[:, :, KV_WIDTH:].astype(bf16))

    def page_map(bb, i, pt, k):
        return (layer, pt[bb, i * n_pg + k], 0, 0)

    seq3p = lambda bb, i, pt: (bb, 0, 0)
    grid_spec = pltpu.PrefetchScalarGridSpec(
        num_scalar_prefetch=1, grid=(b, n_pages // n_pg),
        in_specs=[pl.BlockSpec((1, 1, PAGE_SIZE, w2), functools.partial(page_map, k=k)) for k in range(n_pg)]
        + [pl.BlockSpec((1, KV_WIDTH, LANES), seq3p), pl.BlockSpec((1, nbp, LANES), seq3p),
           pl.BlockSpec((1, CMP_BLOCK, w2), seq3p), pl.BlockSpec((1, 3, LANES), seq3p),
           pl.BlockSpec((1, KV_WIDTH, LANES), seq3p), pl.BlockSpec((1, KV_WIDTH, LANES), seq3p)],
        out_specs=pl.BlockSpec((1, KV_WIDTH, LANES), seq3p),
        scratch_shapes=[pltpu.VMEM((1, LANES), f32), pltpu.VMEM((1, LANES), f32), pltpu.VMEM((KV_WIDTH, LANES), f32)])
    ot = pl.pallas_call(
        functools.partial(_smp_sel_kernel, n_pg=n_pg, past=past), grid_spec=grid_spec,
        out_shape=jax.ShapeDtypeStruct((b, KV_WIDTH, LANES), f32),
        compiler_params=_cp("arbitrary", "arbitrary"), name="nsa_sample_sel")(
            page_table, *([cache_sel.reshape(-1, pool, PAGE_SIZE, w2)] * n_pg), qbd, sel, pad_blk(kv_s), gt, oc, ow)
    ot = ot.reshape(b, KV_HEADS, HEAD_DIM, KV_HEADS, GROUP, lq)
    o = jnp.stack([ot[:, h, :, h] for h in range(KV_HEADS)], axis=1)
    o = o.transpose(0, 4, 1, 3, 2).reshape(b * lq, d)
    return o, kv_c, kv_s, wkv[:, lq:]


def _hgrn_kernel(zq_ref, zf_ref, zi_ref, zg_ref, lb_ref, ng_ref, s0_ref, o_ref, sout_ref, st_ref, *, n_valid):
    t = pl.program_id(2)
    tb = zq_ref.shape[1]
    n_ch = tb // HG_CHUNK

    @pl.when(t == 0)
    def _():
        st_ref[...] = s0_ref[0, 0].T

    zq, zf, zi, zg = zq_ref[0], zf_ref[0], zi_ref[0], zg_ref[0]
    lb = lb_ref[0]
    la = jnp.log(lb)
    l1 = jnp.log1p(-lb)
    bb = l1 + _log_sigmoid(zf)
    log_f = jnp.maximum(la, bb) + jnp.log1p(jnp.exp(-jnp.abs(la - bb)))
    kk = jnp.exp(l1 + _log_sigmoid(-zf))
    if n_valid < tb:
        live = lax.broadcasted_iota(i32, (tb, HG_DK), 0) < n_valid
        log_f = jnp.where(live, log_f, 0.0)
        kk = jnp.where(live, kk, 0.0)

    r = lax.broadcasted_iota(i32, (tb, tb), 0)
    cc = lax.broadcasted_iota(i32, (tb, tb), 1)
    tri = jnp.where(r // HG_CHUNK == cc // HG_CHUNK, jnp.where(cc <= r, 1.0, 0.0), 0.0).astype(bf16)
    hi = log_f.astype(bf16)
    r1 = log_f - hi.astype(f32)
    mid = r1.astype(bf16)
    lo = (r1 - mid.astype(f32)).astype(bf16)
    cum = (jnp.dot(tri, hi, preferred_element_type=f32) + jnp.dot(tri, mid, preferred_element_type=f32)
           + jnp.dot(tri, lo, preferred_element_type=f32))

    causal = lax.broadcasted_iota(i32, (HG_CHUNK, HG_CHUNK), 1) <= lax.broadcasted_iota(i32, (HG_CHUNK, HG_CHUNK), 0)
    mid_row = HG_CHUNK // 2
    st = st_ref[...]
    outs = []
    for ci in range(n_ch):
        sl = slice(ci * HG_CHUNK, (ci + 1) * HG_CHUNK)
        cm, qi, ki, vi = cum[sl], zq[sl], kk[sl], zi[sl]
        ref = cm[mid_row:mid_row + 1]
        last = cm[HG_CHUNK - 1:HG_CHUNK]
        a = lax.dot_general((qi * jnp.exp(cm - ref)).astype(bf16), (ki * jnp.exp(ref - cm)).astype(bf16),
                            (((1,), (1,)), ((), ())), preferred_element_type=f32)
        a = jnp.where(causal, a, 0.0)
        o = jnp.dot(a.astype(bf16), vi.astype(bf16), preferred_element_type=f32)
        o = o + lax.dot_general((qi * jnp.exp(cm)).astype(bf16), st.astype(bf16),
                                (((1,), (1,)), ((), ())), preferred_element_type=f32)
        st = st * jnp.exp(last) + _dot_t0(vi.astype(bf16), (ki * jnp.exp(last - cm)).astype(bf16))
        outs.append(o)
    st_ref[...] = st
    o = jnp.concatenate(outs, axis=0) if n_ch > 1 else outs[0]
    o = o * lax.rsqrt(jnp.mean(o * o, axis=-1, keepdims=True) + RMS_EPS)
    o_ref[0] = (o * ng_ref[0] * (zg * _sigmoid(zg))).astype(o_ref.dtype)

    @pl.when(t == pl.num_programs(2) - 1)
    def _():
        sout_ref[0, 0] = st.T


def _hgrn(p, s0, lb, norm_g, tb, n_valid):
    b, lp, _ = p.shape
    h = HG_HEADS
    col = lambda off: (lambda bb, hh, t: (bb, t, off * h + hh))
    per_head = lambda bb, hh, t: (hh, 0, 0)
    state = lambda bb, hh, t: (bb, hh, 0, 0)
    return pl.pallas_call(
        functools.partial(_hgrn_kernel, n_valid=n_valid), grid=(b, h, lp // tb),
        in_specs=[pl.BlockSpec((1, tb, HG_DK), col(0)), pl.BlockSpec((1, tb, HG_DK), col(1)),
                  pl.BlockSpec((1, tb, HG_DV), col(2)), pl.BlockSpec((1, tb, HG_DV), col(3)),
                  pl.BlockSpec((1, 1, HG_DK), per_head), pl.BlockSpec((1, 1, HG_DV), per_head),
                  pl.BlockSpec((1, 1, HG_DK, HG_DV), state)],
        out_specs=[pl.BlockSpec((1, tb, HG_DV), lambda bb, hh, t: (bb, t, hh)),
                   pl.BlockSpec((1, 1, HG_DK, HG_DV), state)],
        out_shape=[jax.ShapeDtypeStruct((b, lp, h * HG_DV), bf16), jax.ShapeDtypeStruct((b, h, HG_DK, HG_DV), f32)],
        scratch_shapes=[pltpu.VMEM((HG_DV, HG_DK), f32)],
        compiler_params=_cp("arbitrary", "arbitrary", "arbitrary"), name="hgrn_scan")(
            p, p, p, p, lb.reshape(h, 1, HG_DK), norm_g.reshape(h, 1, HG_DV), s0)


def _hgrn_mixer(x, s0, w_in, lb, norm_g, tb):
    b, seq, d = x.shape
    lp = -(-seq // HG_CHUNK) * HG_CHUNK
    tb = min(tb, lp)
    p = _matmul(x.reshape(b * seq, d), w_in, min(256, b * seq), "hgrn_in_proj").reshape(b, seq, 4 * d)
    if lp != seq:
        p = jnp.pad(p, ((0, 0), (0, lp - seq), (0, 0)))
    o, s_out = _hgrn(p, s0, lb, norm_g, tb, seq if lp != seq else lp)
    return o[:, :seq].reshape(b * seq, d), s_out


def _router_kernel(x_ref, whi_ref, wlo_ref, b_ref, cw_ref):
    x = x_ref[...]
    xh = x.astype(bf16)
    xl = (x - xh.astype(f32)).astype(bf16)
    lg = (jnp.dot(xh, whi_ref[...], preferred_element_type=f32) + jnp.dot(xl, whi_ref[...], preferred_element_type=f32)
          + jnp.dot(xh, wlo_ref[...], preferred_element_type=f32)) + b_ref[...]
    ne = N_GROUPS * EXPERTS_PER_GROUP
    lane = lax.broadcasted_iota(i32, lg.shape, 1)
    lanef = lane.astype(f32)
    in_c = (lane >= ne) & (lane < ne + N_GROUPS)
    lc = jnp.where(in_c, lg, -jnp.inf)
    mc = jnp.max(lc, axis=-1, keepdims=True)
    grp = jnp.min(jnp.where(lc == mc, lanef, 1e9), axis=-1, keepdims=True) - float(ne)
    pg = 1.0 / jnp.sum(jnp.where(in_c, jnp.exp(lc - mc), 0.0), axis=-1, keepdims=True)
    e_lo = grp * float(EXPERTS_PER_GROUP)
    in_e = (lanef >= e_lo) & (lanef < e_lo + float(EXPERTS_PER_GROUP))
    le = jnp.where(in_e, lg, -jnp.inf)
    v1 = jnp.max(le, axis=-1, keepdims=True)
    i1 = jnp.min(jnp.where(le == v1, lanef, 1e9), axis=-1, keepdims=True)
    le2 = jnp.where(lanef == i1, -jnp.inf, le)
    v2 = jnp.max(le2, axis=-1, keepdims=True)
    i2 = jnp.min(jnp.where(le2 == v2, lanef, 1e9), axis=-1, keepdims=True)
    e2 = jnp.exp(v2 - v1)
    w1 = pg / (1.0 + e2)
    cw = jnp.where(lanef == i1, w1, 0.0) + jnp.where(lanef == i2, w1 * e2, 0.0)
    for g in range(N_GROUPS):
        cw_ref[g] = cw if g == 0 else pltpu.roll(cw, LANES - g * EXPERTS_PER_GROUP, axis=1)


def _router(x, whi, wlo, bias, tm):
    t, d = x.shape
    return pl.pallas_call(
        _router_kernel, grid=(t // tm,),
        in_specs=[pl.BlockSpec((tm, d), lambda i: (i, 0)), pl.BlockSpec((d, LANES), lambda i: (0, 0)),
                  pl.BlockSpec((d, LANES), lambda i: (0, 0)), pl.BlockSpec((1, LANES), lambda i: (0, 0))],
        out_specs=pl.BlockSpec((N_GROUPS, tm, LANES), lambda i: (0, i, 0)),
        out_shape=jax.ShapeDtypeStruct((N_GROUPS, t, LANES), f32), compiler_params=_cp("arbitrary"),
        name="moe_router")(x, whi, wlo, bias)


def _moe_kernel(x_ref, cw_ref, wg_ref, wu_ref, wd_ref, g_ref, b_ref, o_ref, acc_ref):
    gi = pl.program_id(1)

    @pl.when(gi == 0)
    def _():
        acc_ref[...] = jnp.zeros(acc_ref.shape, f32)

    xb = x_ref[...].astype(bf16)
    cw = cw_ref[0]
    acc = acc_ref[...]
    for e in range(EXPERTS_PER_GROUP):
        hg = jnp.dot(xb, wg_ref[0, e], preferred_element_type=f32)
        hu = jnp.dot(xb, wu_ref[0, e], preferred_element_type=f32)
        h = hg * _sigmoid(hg) * hu * cw[:, e:e + 1]
        acc = acc + jnp.dot(h.astype(bf16), wd_ref[0, e], preferred_element_type=f32)
    acc_ref[...] = acc

    @pl.when(gi == pl.num_programs(1) - 1)
    def _():
        o_ref[...] = _layer_norm(ALPHA * x_ref[...] + acc_ref[...], g_ref[...], b_ref[...])


def _moe_ln(x, cw, wg, wu, wd, g, b, tm):
    t, d = x.shape
    e, hid = wg.shape[1], wg.shape[3]
    row = lambda i, gi: (i, 0)
    fix = lambda i, gi: (0, 0)
    return pl.pallas_call(
        _moe_kernel, grid=(t // tm, N_GROUPS),
        in_specs=[pl.BlockSpec((tm, d), row), pl.BlockSpec((1, tm, LANES), lambda i, gi: (gi, i, 0)),
                  pl.BlockSpec((1, e, d, hid), lambda i, gi: (gi, 0, 0, 0)),
                  pl.BlockSpec((1, e, d, hid), lambda i, gi: (gi, 0, 0, 0)),
                  pl.BlockSpec((1, e, hid, d), lambda i, gi: (gi, 0, 0, 0)),
                  pl.BlockSpec((1, d), fix), pl.BlockSpec((1, d), fix)],
        out_specs=pl.BlockSpec((tm, d), row),
        out_shape=jax.ShapeDtypeStruct((t, d), f32),
        scratch_shapes=[pltpu.VMEM((tm, d), f32)],
        compiler_params=_cp("arbitrary", "arbitrary"), name="moe_dense")(
            x, cw, wg, wu, wd, g.reshape(1, d), b.reshape(1, d))


def _router_weights(w_rc, b_rc, w_re, b_re):
    d = w_rc.shape[0]
    ne = N_GROUPS * EXPERTS_PER_GROUP
    w = jnp.concatenate([w_re.reshape(d, ne), w_rc, jnp.zeros((d, LANES - ne - N_GROUPS), f32)], axis=1)
    bias = jnp.concatenate([b_re.reshape(ne), b_rc, jnp.zeros((LANES - ne - N_GROUPS,), f32)]).reshape(1, LANES)
    whi = w.astype(bf16)
    wlo = (w - whi.astype(f32)).astype(bf16)
    return whi, wlo, bias


def kernel(x_prompt, x_sample, cache_cmp_kv, cache_sel_kv, cache_win_kv, state_hgrn, page_table, attn_w_in, attn_cmp_alpha, attn_w_out, rec_w_in, rec_lb_logits, rec_norm_g, rec_w_out, ln_g, ln_b, moe_w_router_c, moe_b_router_c, moe_w_router_e, moe_b_router_e, moe_w_gate, moe_w_up, moe_w_down):
    b, seq, d = x_prompt.shape
    bs, lq, _ = x_sample.shape
    tp, ts = b * seq, bs * lq
    lb_all = jnp.cumsum(jax.nn.softmax(rec_lb_logits.astype(f32), axis=0), axis=0)
    hp, hs = x_prompt.reshape(tp, d), x_sample.reshape(ts, d)
    kv_out = [[] for _ in range(6)]
    rec_out = [[], []]
    for layer in range(DEPTH):
        j = layer // 2
        if layer % 2 == 0:
            n_in = attn_w_in.shape[2]
            w_in = jnp.pad(attn_w_in[j], ((0, 0), (0, -(-n_in // LANES) * LANES - n_in))).astype(bf16)
            alpha_exp = jnp.repeat(attn_cmp_alpha[j], HEAD_DIM, axis=-1).reshape(CMP_BLOCK, 2 * KV_WIDTH)
            w_out = attn_w_out[j].astype(bf16)
            op, c_p, s_p, w_p = _nsa_prompt(hp.reshape(b, seq, d), w_in, alpha_exp)
            os_, c_s, s_s, w_s = _nsa_sample(hs.reshape(bs, lq, d), cache_cmp_kv, cache_sel_kv, cache_win_kv,
                                             page_table, j, w_in, alpha_exp)
            kvs = (bs, lq, 2, KV_HEADS, HEAD_DIM)
            n_win = min(WINDOW, seq)
            for lst, val in zip(kv_out, (c_p, c_s.reshape(kvs), s_p, s_s.reshape(kvs), w_p[:, seq - n_win:],
                                         w_s.reshape(bs, -1, 2, KV_HEADS, HEAD_DIM))):
                lst.append(val)
        else:
            lb = lb_all[layer] - lb_all[0]
            w_in = rec_w_in[j].astype(bf16)
            w_out = rec_w_out[j].astype(bf16)
            op, st_p = _hgrn_mixer(hp.reshape(b, seq, d), jnp.zeros((b, HG_HEADS, HG_DK, HG_DV), f32), w_in, lb,
                                   rec_norm_g[j], 256)
            os_, st_s = _hgrn_mixer(hs.reshape(bs, lq, d), state_hgrn[j], w_in, lb, rec_norm_g[j], 256)
            rec_out[0].append(st_p)
            rec_out[1].append(st_s)
        hp = _proj_ln(op, w_out, hp, ln_g[layer, 0], ln_b[layer, 0], 512, "out_proj_ln")
        hs = _proj_ln(os_, w_out, hs, ln_g[layer, 0], ln_b[layer, 0], ts, "out_proj_ln_s")
        whi, wlo, rbias = _router_weights(moe_w_router_c[layer], moe_b_router_c[layer], moe_w_router_e[layer],
                                          moe_b_router_e[layer])
        wg, wu, wd = moe_w_gate[layer].astype(bf16), moe_w_up[layer].astype(bf16), moe_w_down[layer].astype(bf16)
        hp = _moe_ln(hp, _router(hp, whi, wlo, rbias, 512), wg, wu, wd, ln_g[layer, 1], ln_b[layer, 1], 512)
        hs = _moe_ln(hs, _router(hs, whi, wlo, rbias, ts), wg, wu, wd, ln_g[layer, 1], ln_b[layer, 1], ts)
    return (hp.reshape(b, seq, d), hs.reshape(bs, lq, d), *[jnp.stack(v) for v in kv_out],
            jnp.stack(rec_out[0]), jnp.stack(rec_out[1]))
```

```python
import functools

import jax
import jax.numpy as jnp
from jax import lax
from jax.experimental import pallas as pl
from jax.experimental.pallas import tpu as pltpu

f32, bf16, i32 = jnp.float32, jnp.bfloat16, jnp.int32

DEPTH = 2
N_HEADS, KV_HEADS, HEAD_DIM = 16, 4, 64
GROUP = N_HEADS // KV_HEADS
KV_WIDTH = KV_HEADS * HEAD_DIM
CMP_BLOCK, TOP_K, WINDOW = 64, 16, 512
FORCED_SCORE = float(GROUP + 1)
PAGE_SIZE = 128
HG_HEADS, HG_DK, HG_DV, HG_CHUNK = 8, 128, 128, 32
N_GROUPS, EXPERTS_PER_GROUP = 4, 8
ALPHA = (2.0 * DEPTH) ** 0.25
LN_EPS, RMS_EPS = 1e-5, 1e-6

NEG = -1e30
LOG2E = 1.4426950408889634
CMP_SHIFT = CMP_BLOCK.bit_length() - 1
assert 1 << CMP_SHIFT == CMP_BLOCK
LANES = 128
VMEM_LIMIT = 56 * 1024 * 1024


def _cp(*sem):
    return pltpu.CompilerParams(dimension_semantics=sem, vmem_limit_bytes=VMEM_LIMIT)


def _sigmoid(x):
    return 1.0 / (1.0 + jnp.exp(-x))


def _log_sigmoid(x):
    return jnp.minimum(x, 0.0) - jnp.log1p(jnp.exp(-jnp.abs(x)))


def _layer_norm(z, g, b):
    mu = jnp.mean(z, axis=-1, keepdims=True)
    d = z - mu
    var = jnp.mean(d * d, axis=-1, keepdims=True)
    return d * lax.rsqrt(var + LN_EPS) * g + b


def _mm_kernel(x_ref, w_ref, o_ref):
    o_ref[...] = jnp.dot(x_ref[...].astype(bf16), w_ref[...], preferred_element_type=f32).astype(o_ref.dtype)


def _matmul(x, w, tm, name):
    m, k = x.shape
    n = w.shape[1]
    return pl.pallas_call(
        _mm_kernel, grid=(m // tm,),
        in_specs=[pl.BlockSpec((tm, k), lambda i: (i, 0)), pl.BlockSpec((k, n), lambda i: (0, 0))],
        out_specs=pl.BlockSpec((tm, n), lambda i: (i, 0)),
        out_shape=jax.ShapeDtypeStruct((m, n), f32), compiler_params=_cp("arbitrary"), name=name)(x, w)


def _proj_ln_kernel(o_ref, w_ref, x_ref, g_ref, b_ref, h_ref):
    y = jnp.dot(o_ref[...].astype(bf16), w_ref[...], preferred_element_type=f32)
    h_ref[...] = _layer_norm(ALPHA * x_ref[...] + y, g_ref[...], b_ref[...])


def _proj_ln(o, w, x, g, b, tm, name):
    m, k = o.shape
    n = w.shape[1]
    row = lambda i: (i, 0)
    fix = lambda i: (0, 0)
    return pl.pallas_call(
        _proj_ln_kernel, grid=(m // tm,),
        in_specs=[pl.BlockSpec((tm, k), row), pl.BlockSpec((k, n), fix), pl.BlockSpec((tm, n), row),
                  pl.BlockSpec((1, n), fix), pl.BlockSpec((1, n), fix)],
        out_specs=pl.BlockSpec((tm, n), row),
        out_shape=jax.ShapeDtypeStruct((m, n), f32), compiler_params=_cp("arbitrary"), name=name)(
            o, w, x, g.reshape(1, n), b.reshape(1, n))


def _split3(x):
    hi = x.astype(bf16)
    r1 = x - hi.astype(f32)
    mid = r1.astype(bf16)
    return hi, mid, (r1 - mid.astype(f32)).astype(bf16)


def _dot_nt(a, b):
    return lax.dot_general(a, b, (((1,), (1,)), ((), ())), preferred_element_type=f32)


def _block_sum_lanes(y, onehot):
    hi, mid, lo = _split3(y)
    return _dot_nt(hi, onehot) + _dot_nt(mid, onehot) + _dot_nt(lo, onehot)


def _compress_kernel(x_ref, a_ref, e_ref, o_ref):
    o_ref[0, 0] = _block_sum_lanes(x_ref[0, 0] * a_ref[0], e_ref[...])


def _compress_t(kvt, alpha_l, onehot):
    b, ch, hd, seq = kvt.shape
    nbp = onehot.shape[0]
    return pl.pallas_call(
        _compress_kernel, grid=(b, ch),
        in_specs=[pl.BlockSpec((1, 1, hd, seq), lambda i, j: (i, j, 0, 0)),
                  pl.BlockSpec((1, 1, seq), lambda i, j: (j, 0, 0)),
                  pl.BlockSpec((nbp, seq), lambda i, j: (0, 0))],
        out_specs=pl.BlockSpec((1, 1, hd, nbp), lambda i, j: (i, j, 0, 0)),
        out_shape=jax.ShapeDtypeStruct((b, ch, hd, nbp), f32),
        compiler_params=_cp("arbitrary", "arbitrary"), name="nsa_compress")(kvt, alpha_l, onehot)


def _compress_paged_kernel(pt_ref, *refs, n_pg):
    del pt_ref
    pages, a_ref, e_ref, o_ref = refs[:n_pg], refs[n_pg], refs[n_pg + 1], refs[n_pg + 2]
    w = 2 * KV_WIDTH
    x = jnp.concatenate([r[0, 0].reshape(w, PAGE_SIZE) for r in pages], axis=1)
    o_ref[0, 0] = _block_sum_lanes(x * a_ref[...], e_ref[...])


def _compress_paged(cache_t, layer, page_table, alpha_rows, n_pg):
    nb_, n_pages = page_table.shape
    bpp = PAGE_SIZE // CMP_BLOCK
    n_tok = n_pg * PAGE_SIZE
    a_t = jnp.tile(alpha_rows, (1, n_pg * bpp))
    onehot = (jnp.arange(n_pg * bpp)[:, None] == jnp.arange(n_tok)[None, :] // CMP_BLOCK).astype(bf16)

    def page_map(b, i, pt, k):
        return (layer, pt[b, i * n_pg + k], 0, 0, 0, 0)

    fix = lambda b, i, pt: (0, 0)
    grid_spec = pltpu.PrefetchScalarGridSpec(
        num_scalar_prefetch=1, grid=(nb_, n_pages // n_pg),
        in_specs=[pl.BlockSpec((1, 1, 2, KV_HEADS, HEAD_DIM, PAGE_SIZE), functools.partial(page_map, k=k))
                  for k in range(n_pg)]
        + [pl.BlockSpec(a_t.shape, fix), pl.BlockSpec(onehot.shape, fix)],
        out_specs=pl.BlockSpec((1, 1, 2 * KV_WIDTH, n_pg * bpp), lambda b, i, pt: (b, i, 0, 0)))
    return pl.pallas_call(
        functools.partial(_compress_paged_kernel, n_pg=n_pg), grid_spec=grid_spec,
        out_shape=jax.ShapeDtypeStruct((nb_, n_pages // n_pg, 2 * KV_WIDTH, n_pg * bpp), f32),
        compiler_params=_cp("arbitrary", "arbitrary"), name="nsa_compress_paged")(
            page_table, *([cache_t] * n_pg), a_t, onehot)


def _select_bias(score, k_sel):
    rowf = lax.broadcasted_iota(i32, score.shape, 0).astype(f32)
    picked = jnp.zeros(score.shape, f32)
    work = score
    for _ in range(k_sel):
        mx = jnp.max(work, axis=0, keepdims=True)
        first = jnp.min(jnp.where(work == mx, rowf, 1e9), axis=0, keepdims=True)
        pick = rowf == first
        picked = jnp.where(pick, 1.0, picked)
        work = jnp.where(pick, -jnp.inf, work)
    return jnp.where(picked > 0.0, 0.0, NEG)


def _masked_softmax_cols(s, mask, exp_fn):
    sm = jnp.where(mask, s, NEG)
    m = jnp.max(sm, axis=0, keepdims=True)
    e = jnp.where(mask, exp_fn(sm - m), 0.0)
    return e / jnp.maximum(jnp.sum(e, axis=0, keepdims=True), 1e-30)


def _nsa_in_proj_kernel(x_ref, wq_ref, wkv_ref, pq_ref, c_ref, s_ref, w_ref):
    xb = x_ref[0].astype(bf16)
    pq_ref[0] = jnp.dot(xb, wq_ref[...], preferred_element_type=f32)
    kvt = _dot_nt(wkv_ref[...], xb)
    w = 2 * KV_WIDTH
    for n, ref in enumerate((c_ref, s_ref, w_ref)):
        ref[0] = kvt[n * w:(n + 1) * w]


def _nsa_in_proj(x, wq, wkv_t, tm):
    b, seq, d = x.shape
    nqg, w = wq.shape[1], 2 * KV_WIDTH
    kv_spec = pl.BlockSpec((1, w, tm), lambda i, j: (i, 0, j))
    kv_shape = jax.ShapeDtypeStruct((b, w, seq), f32)
    return pl.pallas_call(
        _nsa_in_proj_kernel, grid=(b, seq // tm),
        in_specs=[pl.BlockSpec((1, tm, d), lambda i, j: (i, j, 0)), pl.BlockSpec((d, nqg), lambda i, j: (0, 0)),
                  pl.BlockSpec((3 * w, d), lambda i, j: (0, 0))],
        out_specs=[pl.BlockSpec((1, tm, nqg), lambda i, j: (i, j, 0)), kv_spec, kv_spec, kv_spec],
        out_shape=[jax.ShapeDtypeStruct((b, seq, nqg), f32), kv_shape, kv_shape, kv_shape],
        compiler_params=_cp("arbitrary", "arbitrary"), name="nsa_in_proj")(x, wq, wkv_t)


def _nsa_prompt_kernel(qt_ref, g_ref, kct_ref, vct_ref, kst_ref, vst_ref, kwt_ref, vwt_ref, et_ref, o_ref,
                       m_ref, l_ref, acc_ref, *, tq, tk, nb):
    s0 = pl.program_id(1) * tq
    c = GROUP * tq
    nbp = et_ref.shape[1]
    qt = qt_ref[0, 0, 0]
    tq_pos = s0 + lax.broadcasted_iota(i32, (1, tq), 1)
    tpos = jnp.concatenate([tq_pos] * GROUP, axis=1)

    kc = kct_ref[0, 0, 0].T.astype(bf16)
    sc = jnp.dot(kc, qt, preferred_element_type=f32)
    blk_c = lax.broadcasted_iota(i32, (nbp, c), 0)
    pc = _masked_softmax_cols(sc, blk_c * CMP_BLOCK + (CMP_BLOCK - 1) <= tpos, jnp.exp2)
    oc = jnp.dot(vct_ref[0, 0, 0].astype(bf16), pc.astype(bf16), preferred_element_type=f32)

    imp = pc[:, 0:tq]
    for g in range(1, GROUP):
        imp = imp + pc[:, g * tq:(g + 1) * tq]
    blk = lax.broadcasted_iota(i32, (nbp, tq), 0)
    cur = jnp.right_shift(tq_pos, CMP_SHIFT)
    forced = (blk == 0) | (blk == cur) | (blk == cur - 1)
    base = jnp.where(forced, FORCED_SCORE, jnp.where(blk * CMP_BLOCK <= tq_pos, 0.0, -1.0))
    sel = _select_bias(jnp.where(blk < nb, base + imp, -3.0), min(TOP_K, nb)).astype(bf16)
    qa = jnp.concatenate([jnp.concatenate([sel] * GROUP, axis=1), qt], axis=0)

    m_ref[...] = jnp.full(m_ref.shape, NEG, f32)
    l_ref[...] = jnp.zeros(l_ref.shape, f32)
    acc_ref[...] = jnp.zeros(acc_ref.shape, f32)

    def tile(k0, width, causal):
        kk = kst_ref[0, 0, 0, :, pl.ds(k0, width)].T.astype(bf16)
        s = jnp.dot(jnp.concatenate([et_ref[pl.ds(k0, width), :], kk], axis=1), qa,
                    preferred_element_type=f32)
        if causal:
            s = jnp.where(k0 + lax.broadcasted_iota(i32, (width, c), 0) <= tpos, s, NEG)
        m_new = jnp.maximum(m_ref[...], jnp.max(s, axis=0, keepdims=True))
        a = jnp.exp2(m_ref[...] - m_new)
        p = jnp.exp2(s - m_new)
        l_ref[...] = a * l_ref[...] + jnp.sum(p, axis=0, keepdims=True)
        acc_ref[...] = a * acc_ref[...] + jnp.dot(vst_ref[0, 0, 0, :, pl.ds(k0, width)].astype(bf16), p.astype(bf16),
                                                  preferred_element_type=f32)
        m_ref[...] = m_new

    n_full = s0 // tk

    def pair(i, carry):
        tile(pl.multiple_of(i * (2 * tk), 2 * tk), 2 * tk, False)
        return carry

    lax.fori_loop(0, n_full // 2, pair, 0)

    @pl.when(n_full % 2 == 1)
    def _():
        tile(pl.multiple_of((n_full - 1) * tk, tk), tk, False)

    tile(pl.multiple_of(n_full * tk, tk), tk, True)
    osel = acc_ref[...] / jnp.maximum(l_ref[...], 1e-30)

    wt = WINDOW + tq
    k0w = pl.multiple_of(jnp.maximum(s0 - WINDOW, 0), LANES)
    kw = kwt_ref[0, 0, 0, :, pl.ds(k0w, wt)].T.astype(bf16)
    s = jnp.dot(kw, qt, preferred_element_type=f32)
    dist = tpos - (k0w + lax.broadcasted_iota(i32, (wt, c), 0))
    s = jnp.where(dist >= 0, jnp.where(dist < WINDOW, s, NEG), NEG)
    p = jnp.exp2(s - jnp.max(s, axis=0, keepdims=True))
    lw = jnp.sum(p, axis=0, keepdims=True)
    ow = jnp.dot(vwt_ref[0, 0, 0, :, pl.ds(k0w, wt)].astype(bf16), p.astype(bf16), preferred_element_type=f32)
    ow = ow / jnp.maximum(lw, 1e-30)

    gt = _sigmoid(g_ref[0, 0, 0])
    o_ref[0, 0, 0] = (gt[0:1] * oc + gt[1:2] * osel + gt[2:3] * ow).astype(o_ref.dtype)


def _nsa_prompt_attention(qt, gt, cblk_t, kvs_t, kvw_t, onehot_t, tq, tk):
    b, kvh, n_t, hd, c = qt.shape
    seq, nbp = onehot_t.shape
    assert tk % tq == 0
    tile = lambda bh, i: (bh // kvh, bh % kvh, i, 0, 0)
    comp = lambda cc: (lambda bh, i: (bh // kvh, cc, bh % kvh, 0, 0))
    rows = lambda n: (pl.BlockSpec((1, 1, 1, hd, n), comp(0)), pl.BlockSpec((1, 1, 1, hd, n), comp(1)))
    return pl.pallas_call(
        functools.partial(_nsa_prompt_kernel, tq=tq, tk=tk, nb=seq // CMP_BLOCK), grid=(b * kvh, n_t),
        in_specs=[pl.BlockSpec((1, 1, 1, hd, c), tile), pl.BlockSpec((1, 1, 1, 3, c), tile),
                  *rows(nbp), *rows(seq), *rows(seq), pl.BlockSpec((seq, nbp), lambda bh, i: (0, 0))],
        out_specs=pl.BlockSpec((1, 1, 1, hd, c), tile),
        out_shape=jax.ShapeDtypeStruct((b, kvh, n_t, hd, c), bf16),
        scratch_shapes=[pltpu.VMEM((1, c), f32), pltpu.VMEM((1, c), f32), pltpu.VMEM((hd, c), f32)],
        compiler_params=_cp("arbitrary", "arbitrary"), name="nsa_prompt_attn")(
            qt, gt, cblk_t, cblk_t, kvs_t, kvs_t, kvw_t, kvw_t, onehot_t)


def _nsa_prompt(x, wq, wkv_t, alpha_rows, tq=128, tk=512):
    b, seq, d = x.shape
    n_t, nb = seq // tq, seq // CMP_BLOCK
    nq = N_HEADS * HEAD_DIM
    pq, kvc_t, kvs_t, kvw_t = _nsa_in_proj(x, wq, wkv_t, 512)
    five = lambda a: a.reshape(b, 2, KV_HEADS, HEAD_DIM, -1)
    nbp = -(-nb // LANES) * LANES
    onehot = (jnp.arange(nbp)[:, None] == jnp.arange(seq)[None, :] // CMP_BLOCK).astype(bf16)
    alpha_l = jnp.tile(alpha_rows[::HEAD_DIM], (1, nb)).reshape(2 * KV_HEADS, 1, seq)
    cblk_t = _compress_t(kvc_t.reshape(b, 2 * KV_HEADS, HEAD_DIM, seq), alpha_l, onehot)
    qs = (pq[..., :nq] * (HEAD_DIM ** -0.5 * LOG2E)).astype(bf16).reshape(b, n_t, tq, KV_HEADS, GROUP, HEAD_DIM)
    qt = qs.transpose(0, 3, 1, 5, 4, 2).reshape(b, KV_HEADS, n_t, HEAD_DIM, GROUP * tq)
    gt = pq[..., nq:nq + 3 * N_HEADS].reshape(b, n_t, tq, KV_HEADS, GROUP, 3)
    gt = gt.transpose(0, 3, 1, 5, 4, 2).reshape(b, KV_HEADS, n_t, 3, GROUP * tq)
    ot = _nsa_prompt_attention(qt, gt, five(cblk_t), five(kvs_t), five(kvw_t), onehot.T, tq, tk)
    o = ot.reshape(b, KV_HEADS, n_t, HEAD_DIM, GROUP, tq).transpose(0, 2, 5, 1, 4, 3).reshape(b * seq, d)
    return o, kvc_t, kvs_t, kvw_t


def _dot_t0(a, b):
    return lax.dot_general(a, b, (((0,), (0,)), ((), ())), preferred_element_type=f32)


def _smp_cmp_win_kernel(q_ref, kc_ref, vc_ref, cn_ref, a_ref, kw_ref, vw_ref, oc_ref, ow_ref, sel_ref, *, past, lq,
                        nb_real):
    q = q_ref[0]
    nbp = kc_ref.shape[1]
    col = lax.broadcasted_iota(i32, (1, LANES), 1)
    tpos = past + jnp.bitwise_and(col, lq - 1)

    c_new = jnp.sum(cn_ref[0] * a_ref[...], axis=0, keepdims=True).astype(bf16)
    is_new = lax.broadcasted_iota(i32, (nbp, KV_WIDTH), 0) == nb_real - 1
    kc = jnp.where(is_new, c_new[:, :KV_WIDTH], kc_ref[0])
    vc = jnp.where(is_new, c_new[:, KV_WIDTH:], vc_ref[0])
    sc = jnp.dot(kc, q, preferred_element_type=f32)
    blk = lax.broadcasted_iota(i32, (nbp, LANES), 0)
    pc = _masked_softmax_cols(sc, blk * CMP_BLOCK + (CMP_BLOCK - 1) <= tpos, jnp.exp)
    oc_ref[0] = _dot_t0(pc.astype(bf16), vc)

    r = lax.broadcasted_iota(i32, (LANES, LANES), 0)
    cc = lax.broadcasted_iota(i32, (LANES, LANES), 1)
    per_kv = GROUP * lq
    same = jnp.where(r // per_kv == cc // per_kv, jnp.where(r % lq == cc % lq, 1.0, 0.0), 0.0).astype(bf16)
    hi, mid, lo = _split3(pc)
    imp = (jnp.dot(hi, same, preferred_element_type=f32) + jnp.dot(mid, same, preferred_element_type=f32)
           + jnp.dot(lo, same, preferred_element_type=f32))
    cur = jnp.right_shift(tpos, CMP_SHIFT)
    forced = (blk == 0) | (blk == cur) | (blk == cur - 1)
    base = jnp.where(forced, FORCED_SCORE, jnp.where(blk * CMP_BLOCK <= tpos, 0.0, -1.0))
    score = jnp.where(blk < nb_real, base + imp, -3.0)
    sel_ref[0] = _select_bias(score, min(TOP_K, nb_real))

    nwt = kw_ref.shape[1]
    nw = nwt - lq
    s = jnp.dot(kw_ref[0], q, preferred_element_type=f32)
    wpos = past - nw + lax.broadcasted_iota(i32, (nwt, LANES), 0)
    dist = tpos - wpos
    ok = (dist >= 0) & (dist < WINDOW) & (wpos >= 0)
    pw = _masked_softmax_cols(s, ok, jnp.exp)
    ow_ref[0] = _dot_t0(pw.astype(bf16), vw_ref[0])


def _smp_sel_kernel(pt_ref, *refs, n_pg):
    del pt_ref
    pages = refs[:n_pg]
    q_ref, sel_ref, e_ref, kn_ref, vn_ref, bn_ref, g_ref, oc_ref, ow_ref, o_ref, m_ref, l_ref, acc_ref = refs[n_pg:]
    i = pl.program_id(1)

    @pl.when(i == 0)
    def _():
        m_ref[...] = jnp.full(m_ref.shape, NEG, f32)
        l_ref[...] = jnp.zeros(l_ref.shape, f32)
        acc_ref[...] = jnp.zeros(acc_ref.shape, f32)

    def update(s, vt):
        m_new = jnp.maximum(m_ref[...], jnp.max(s, axis=1, keepdims=True))
        a = jnp.exp(m_ref[...] - m_new)
        p = jnp.exp(s - m_new)
        l_ref[...] = a * l_ref[...] + jnp.sum(p, axis=1, keepdims=True)
        acc_ref[...] = a * acc_ref[...] + _dot_nt(p.astype(bf16), vt)
        m_ref[...] = m_new

    qa = jnp.concatenate([q_ref[0], sel_ref[0, 0].astype(bf16)], axis=1)
    kt = jnp.concatenate([r[0, 0, 0].reshape(KV_WIDTH, PAGE_SIZE) for r in pages], axis=1).astype(bf16)
    vt = jnp.concatenate([r[0, 0, 1].reshape(KV_WIDTH, PAGE_SIZE) for r in pages], axis=1).astype(bf16)
    update(jnp.dot(qa, jnp.concatenate([kt, e_ref[...]], axis=0), preferred_element_type=f32), vt)

    @pl.when(i == pl.num_programs(1) - 1)
    def _():
        update(jnp.dot(q_ref[0], kn_ref[0].astype(bf16), preferred_element_type=f32) + bn_ref[0], vn_ref[0].astype(bf16))
        gt = _sigmoid(g_ref[0])
        osel = acc_ref[...] / jnp.maximum(l_ref[...], 1e-30)
        o_ref[0] = gt[:, 0:1] * oc_ref[0] + gt[:, 1:2] * osel + gt[:, 2:3] * ow_ref[0]


def _nsa_sample(x, cache_cmp_t, cache_sel_t, cache_win, page_table, layer, w_in, alpha_rows, n_pg=8):
    b, lq, d = x.shape
    n_pages = page_table.shape[1]
    past = n_pages * PAGE_SIZE
    assert KV_HEADS * GROUP * lq == LANES and past % CMP_BLOCK == 0 and lq <= CMP_BLOCK
    assert n_pages % n_pg == 0 and n_pg * PAGE_SIZE // CMP_BLOCK <= LANES
    p = _matmul(x.reshape(b * lq, d), w_in, b * lq, "nsa_in_proj_s")
    nq = N_HEADS * HEAD_DIM
    w2 = 2 * KV_WIDTH
    q = (p[:, :nq] * HEAD_DIM ** -0.5).reshape(b, lq, KV_HEADS, GROUP, HEAD_DIM)
    eye = jnp.eye(KV_HEADS, dtype=f32)
    qbd = (q.transpose(0, 2, 3, 1, 4)[:, :, :, :, None, :] * eye[None, :, None, None, :, None])
    qbd = qbd.reshape(b, LANES, KV_WIDTH).astype(bf16)
    qbd_t = qbd.transpose(0, 2, 1)
    gt = p[:, nq + 6 * KV_WIDTH:nq + 6 * KV_WIDTH + 3 * N_HEADS].reshape(b, lq, KV_HEADS, GROUP, 3)
    gt = gt.transpose(0, 2, 3, 1, 4).reshape(b, LANES, 3)
    kv = p[:, nq:nq + 6 * KV_WIDTH].reshape(b, lq, 3, w2)
    kv_c, kv_s, kv_w = kv[:, :, 0], kv[:, :, 1], kv[:, :, 2]

    bpp = PAGE_SIZE // CMP_BLOCK
    c_past = _compress_paged(cache_cmp_t, layer, page_table, alpha_rows, n_pg)
    c_past = c_past.transpose(0, 1, 3, 2).reshape(b, past // CMP_BLOCK, w2)
    nb_real = past // CMP_BLOCK + 1
    nbp = -(-nb_real // 8) * 8
    cblk = jnp.pad(c_past, ((0, 0), (0, nbp - nb_real + 1), (0, 0)))
    wkv = jnp.concatenate([cache_win[layer].reshape(b, -1, w2), kv_w], axis=1)
    nwt = wkv.shape[1]

    seq3 = lambda i: (i, 0, 0)
    oc, ow, sel = pl.pallas_call(
        functools.partial(_smp_cmp_win_kernel, past=past, lq=lq, nb_real=nb_real), grid=(b,),
        in_specs=[pl.BlockSpec((1, KV_WIDTH, LANES), seq3), pl.BlockSpec((1, nbp, KV_WIDTH), seq3),
                  pl.BlockSpec((1, nbp, KV_WIDTH), seq3), pl.BlockSpec((1, lq, w2), seq3),
                  pl.BlockSpec((lq, w2), lambda i: (0, 0)), pl.BlockSpec((1, nwt, KV_WIDTH), seq3),
                  pl.BlockSpec((1, nwt, KV_WIDTH), seq3)],
        out_specs=[pl.BlockSpec((1, LANES, KV_WIDTH), seq3), pl.BlockSpec((1, LANES, KV_WIDTH), seq3),
                   pl.BlockSpec((1, nbp, LANES), seq3)],
        out_shape=[jax.ShapeDtypeStruct((b, LANES, KV_WIDTH), f32), jax.ShapeDtypeStruct((b, LANES, KV_WIDTH), f32),
                   jax.ShapeDtypeStruct((b, nbp, LANES), f32)],
        compiler_params=_cp("arbitrary"), name="nsa_sample_cmp_win")(
            qbd_t, cblk[:, :, :KV_WIDTH].astype(bf16), cblk[:, :, KV_WIDTH:].astype(bf16), kv_c, alpha_rows.T[:lq],
            wkv[:, :, :KV_WIDTH].astype(bf16), wkv[:, :, KV_WIDTH:].astype(bf16))

    n_steps, bps = n_pages // n_pg, n_pg * bpp
    sel_r = sel.transpose(0, 2, 1)
    sel_steps = sel_r[:, :, :past // CMP_BLOCK].reshape(b, LANES, n_steps, bps).transpose(0, 2, 1, 3)
    sel_steps = jnp.pad(sel_steps, ((0, 0), (0, 0), (0, 0), (0, LANES - bps)))
    onehot = (jnp.arange(LANES)[:, None] == jnp.arange(n_pg * PAGE_SIZE)[None, :] // CMP_BLOCK).astype(bf16)
    pad_t = lambda a: jnp.pad(a.transpose(0, 2, 1), ((0, 0), (0, 0), (0, LANES - lq)))
    qoff = jnp.arange(LANES) % lq
    readable = jnp.arange(LANES)[None, :] <= qoff[:, None]
    bias_new = jnp.where(readable[None], sel_r[:, :, past // CMP_BLOCK][:, :, None], NEG)

    def page_map(bb, i, pt, k):
        return (layer, pt[bb, i * n_pg + k], 0, 0, 0, 0)

    seq3p = lambda bb, i, pt: (bb, 0, 0)
    grid_spec = pltpu.PrefetchScalarGridSpec(
        num_scalar_prefetch=1, grid=(b, n_steps),
        in_specs=[pl.BlockSpec((1, 1, 2, KV_HEADS, HEAD_DIM, PAGE_SIZE), functools.partial(page_map, k=k))
                  for k in range(n_pg)]
        + [pl.BlockSpec((1, LANES, KV_WIDTH), seq3p), pl.BlockSpec((1, 1, LANES, LANES), lambda bb, i, pt: (bb, i, 0, 0)),
           pl.BlockSpec(onehot.shape, lambda bb, i, pt: (0, 0)),
           pl.BlockSpec((1, KV_WIDTH, LANES), seq3p), pl.BlockSpec((1, KV_WIDTH, LANES), seq3p),
           pl.BlockSpec((1, LANES, LANES), seq3p), pl.BlockSpec((1, LANES, 3), seq3p),
           pl.BlockSpec((1, LANES, KV_WIDTH), seq3p), pl.BlockSpec((1, LANES, KV_WIDTH), seq3p)],
        out_specs=pl.BlockSpec((1, LANES, KV_WIDTH), seq3p),
        scratch_shapes=[pltpu.VMEM((LANES, 1), f32), pltpu.VMEM((LANES, 1), f32), pltpu.VMEM((LANES, KV_WIDTH), f32)])
    o = pl.pallas_call(
        functools.partial(_smp_sel_kernel, n_pg=n_pg), grid_spec=grid_spec,
        out_shape=jax.ShapeDtypeStruct((b, LANES, KV_WIDTH), f32),
        compiler_params=_cp("arbitrary", "arbitrary"), name="nsa_sample_sel")(
            page_table, *([cache_sel_t] * n_pg), qbd, sel_steps, onehot, pad_t(kv_s[:, :, :KV_WIDTH]),
            pad_t(kv_s[:, :, KV_WIDTH:]), bias_new, gt, oc, ow)
    o = o.reshape(b, KV_HEADS, GROUP, lq, KV_HEADS, HEAD_DIM)
    o = jnp.stack([o[:, h, :, :, h] for h in range(KV_HEADS)], axis=1)
    o = o.transpose(0, 3, 1, 2, 4).reshape(b * lq, d)
    return o, kv_c, kv_s, wkv[:, lq:]


def _hgrn_kernel(zq_ref, zf_ref, zi_ref, zg_ref, lb_ref, ng_ref, s0_ref, o_ref, sout_ref, st_ref, *, n_valid):
    t = pl.program_id(2)
    tb = zq_ref.shape[1]
    n_ch = tb // HG_CHUNK

    @pl.when(t == 0)
    def _():
        st_ref[...] = s0_ref[0, 0].T

    zq, zf, zi, zg = zq_ref[0], zf_ref[0], zi_ref[0], zg_ref[0]
    lb = lb_ref[0]
    la = jnp.log(lb)
    l1 = jnp.log1p(-lb)
    bb = l1 + _log_sigmoid(zf)
    log_f = jnp.maximum(la, bb) + jnp.log1p(jnp.exp(-jnp.abs(la - bb)))
    kk = jnp.exp(l1 + _log_sigmoid(-zf))
    if n_valid < tb:
        live = lax.broadcasted_iota(i32, (tb, HG_DK), 0) < n_valid
        log_f = jnp.where(live, log_f, 0.0)
        kk = jnp.where(live, kk, 0.0)

    r = lax.broadcasted_iota(i32, (tb, tb), 0)
    cc = lax.broadcasted_iota(i32, (tb, tb), 1)
    tri = jnp.where(r // HG_CHUNK == cc // HG_CHUNK, jnp.where(cc <= r, 1.0, 0.0), 0.0).astype(bf16)
    hi, mid, lo = _split3(log_f)
    cum = (jnp.dot(tri, hi, preferred_element_type=f32) + jnp.dot(tri, mid, preferred_element_type=f32)
           + jnp.dot(tri, lo, preferred_element_type=f32))

    causal = lax.broadcasted_iota(i32, (HG_CHUNK, HG_CHUNK), 1) <= lax.broadcasted_iota(i32, (HG_CHUNK, HG_CHUNK), 0)
    mid_row = HG_CHUNK // 2
    st = st_ref[...]
    outs = []
    for ci in range(n_ch):
        sl = slice(ci * HG_CHUNK, (ci + 1) * HG_CHUNK)
        cm, qi, ki, vi = cum[sl], zq[sl], kk[sl], zi[sl]
        ref = cm[mid_row:mid_row + 1]
        last = cm[HG_CHUNK - 1:HG_CHUNK]
        a = _dot_nt((qi * jnp.exp(cm - ref)).astype(bf16), (ki * jnp.exp(ref - cm)).astype(bf16))
        a = jnp.where(causal, a, 0.0)
        o = jnp.dot(a.astype(bf16), vi.astype(bf16), preferred_element_type=f32)
        o = o + _dot_nt((qi * jnp.exp(cm)).astype(bf16), st.astype(bf16))
        st = st * jnp.exp(last) + _dot_t0(vi.astype(bf16), (ki * jnp.exp(last - cm)).astype(bf16))
        outs.append(o)
    st_ref[...] = st
    o = jnp.concatenate(outs, axis=0) if n_ch > 1 else outs[0]
    o = o * lax.rsqrt(jnp.mean(o * o, axis=-1, keepdims=True) + RMS_EPS)
    o_ref[0] = (o * ng_ref[0] * (zg * _sigmoid(zg))).astype(o_ref.dtype)

    @pl.when(t == pl.num_programs(2) - 1)
    def _():
        sout_ref[0, 0] = st.T


def _hgrn(p, s0, lb, norm_g, tb, n_valid):
    b, lp, _ = p.shape
    h = HG_HEADS
    col = lambda off: (lambda bb, hh, t: (bb, t, off * h + hh))
    per_head = lambda bb, hh, t: (hh, 0, 0)
    state = lambda bb, hh, t: (bb, hh, 0, 0)
    return pl.pallas_call(
        functools.partial(_hgrn_kernel, n_valid=n_valid), grid=(b, h, lp // tb),
        in_specs=[pl.BlockSpec((1, tb, HG_DK), col(0)), pl.BlockSpec((1, tb, HG_DK), col(1)),
                  pl.BlockSpec((1, tb, HG_DV), col(2)), pl.BlockSpec((1, tb, HG_DV), col(3)),
                  pl.BlockSpec((1, 1, HG_DK), per_head), pl.BlockSpec((1, 1, HG_DV), per_head),
                  pl.BlockSpec((1, 1, HG_DK, HG_DV), state)],
        out_specs=[pl.BlockSpec((1, tb, HG_DV), lambda bb, hh, t: (bb, t, hh)),
                   pl.BlockSpec((1, 1, HG_DK, HG_DV), state)],
        out_shape=[jax.ShapeDtypeStruct((b, lp, h * HG_DV), bf16), jax.ShapeDtypeStruct((b, h, HG_DK, HG_DV), f32)],
        scratch_shapes=[pltpu.VMEM((HG_DV, HG_DK), f32)],
        compiler_params=_cp("arbitrary", "arbitrary", "arbitrary"), name="hgrn_scan")(
            p, p, p, p, lb.reshape(h, 1, HG_DK), norm_g.reshape(h, 1, HG_DV), s0)


def _hgrn_mixer(x, s0, w_in, lb, norm_g, tb):
    b, seq, d = x.shape
    lp = -(-seq // HG_CHUNK) * HG_CHUNK
    tb = min(tb, lp)
    p = _matmul(x.reshape(b * seq, d), w_in, min(256, b * seq), "hgrn_in_proj").reshape(b, seq, 4 * d)
    if lp != seq:
        p = jnp.pad(p, ((0, 0), (0, lp - seq), (0, 0)))
    o, s_out = _hgrn(p, s0, lb, norm_g, tb, seq if lp != seq else lp)
    return o[:, :seq].reshape(b * seq, d), s_out


def _router_kernel(x_ref, whi_ref, wlo_ref, b_ref, cw_ref):
    x = x_ref[...]
    xh = x.astype(bf16)
    xl = (x - xh.astype(f32)).astype(bf16)
    lg = (jnp.dot(xh, whi_ref[...], preferred_element_type=f32) + jnp.dot(xl, whi_ref[...], preferred_element_type=f32)
          + jnp.dot(xh, wlo_ref[...], preferred_element_type=f32)) + b_ref[...]
    ne = N_GROUPS * EXPERTS_PER_GROUP
    lane = lax.broadcasted_iota(i32, lg.shape, 1)
    lanef = lane.astype(f32)
    in_c = (lane >= ne) & (lane < ne + N_GROUPS)
    lc = jnp.where(in_c, lg, -jnp.inf)
    mc = jnp.max(lc, axis=-1, keepdims=True)
    grp = jnp.min(jnp.where(lc == mc, lanef, 1e9), axis=-1, keepdims=True) - float(ne)
    pg = 1.0 / jnp.sum(jnp.where(in_c, jnp.exp(lc - mc), 0.0), axis=-1, keepdims=True)
    e_lo = grp * float(EXPERTS_PER_GROUP)
    in_e = (lanef >= e_lo) & (lanef < e_lo + float(EXPERTS_PER_GROUP))
    le = jnp.where(in_e, lg, -jnp.inf)
    v1 = jnp.max(le, axis=-1, keepdims=True)
    i1 = jnp.min(jnp.where(le == v1, lanef, 1e9), axis=-1, keepdims=True)
    le2 = jnp.where(lanef == i1, -jnp.inf, le)
    v2 = jnp.max(le2, axis=-1, keepdims=True)
    i2 = jnp.min(jnp.where(le2 == v2, lanef, 1e9), axis=-1, keepdims=True)
    e2 = jnp.exp(v2 - v1)
    w1 = pg / (1.0 + e2)
    cw = jnp.where(lanef == i1, w1, 0.0) + jnp.where(lanef == i2, w1 * e2, 0.0)
    for g in range(N_GROUPS):
        cw_ref[g] = cw if g == 0 else pltpu.roll(cw, LANES - g * EXPERTS_PER_GROUP, axis=1)


def _router(x, whi, wlo, bias, tm):
    t, d = x.shape
    return pl.pallas_call(
        _router_kernel, grid=(t // tm,),
        in_specs=[pl.BlockSpec((tm, d), lambda i: (i, 0)), pl.BlockSpec((d, LANES), lambda i: (0, 0)),
                  pl.BlockSpec((d, LANES), lambda i: (0, 0)), pl.BlockSpec((1, LANES), lambda i: (0, 0))],
        out_specs=pl.BlockSpec((N_GROUPS, tm, LANES), lambda i: (0, i, 0)),
        out_shape=jax.ShapeDtypeStruct((N_GROUPS, t, LANES), f32), compiler_params=_cp("arbitrary"),
        name="moe_router")(x, whi, wlo, bias)


def _moe_kernel(x_ref, cw_ref, wg_ref, wu_ref, wd_ref, g_ref, b_ref, o_ref, acc_ref):
    gi = pl.program_id(1)

    @pl.when(gi == 0)
    def _():
        acc_ref[...] = jnp.zeros(acc_ref.shape, f32)

    xb = x_ref[...].astype(bf16)
    cw = cw_ref[0]
    acc = acc_ref[...]
    for e in range(EXPERTS_PER_GROUP):
        hg = jnp.dot(xb, wg_ref[0, e], preferred_element_type=f32)
        hu = jnp.dot(xb, wu_ref[0, e], preferred_element_type=f32)
        h = hg * _sigmoid(hg) * hu * cw[:, e:e + 1]
        acc = acc + jnp.dot(h.astype(bf16), wd_ref[0, e], preferred_element_type=f32)
    acc_ref[...] = acc

    @pl.when(gi == pl.num_programs(1) - 1)
    def _():
        o_ref[...] = _layer_norm(ALPHA * x_ref[...] + acc_ref[...], g_ref[...], b_ref[...])


def _moe_ln(x, cw, wg, wu, wd, g, b, tm):
    t, d = x.shape
    e, hid = wg.shape[1], wg.shape[3]
    row = lambda i, gi: (i, 0)
    fix = lambda i, gi: (0, 0)
    return pl.pallas_call(
        _moe_kernel, grid=(t // tm, N_GROUPS),
        in_specs=[pl.BlockSpec((tm, d), row), pl.BlockSpec((1, tm, LANES), lambda i, gi: (gi, i, 0)),
                  pl.BlockSpec((1, e, d, hid), lambda i, gi: (gi, 0, 0, 0)),
                  pl.BlockSpec((1, e, d, hid), lambda i, gi: (gi, 0, 0, 0)),
                  pl.BlockSpec((1, e, hid, d), lambda i, gi: (gi, 0, 0, 0)),
                  pl.BlockSpec((1, d), fix), pl.BlockSpec((1, d), fix)],
        out_specs=pl.BlockSpec((tm, d), row),
        out_shape=jax.ShapeDtypeStruct((t, d), f32),
        scratch_shapes=[pltpu.VMEM((tm, d), f32)],
        compiler_params=_cp("arbitrary", "arbitrary"), name="moe_dense")(
            x, cw, wg, wu, wd, g.reshape(1, d), b.reshape(1, d))


def _router_weights(w_rc, b_rc, w_re, b_re):
    d = w_rc.shape[0]
    ne = N_GROUPS * EXPERTS_PER_GROUP
    w = jnp.concatenate([w_re.reshape(d, ne), w_rc, jnp.zeros((d, LANES - ne - N_GROUPS), f32)], axis=1)
    bias = jnp.concatenate([b_re.reshape(ne), b_rc, jnp.zeros((LANES - ne - N_GROUPS,), f32)]).reshape(1, LANES)
    whi = w.astype(bf16)
    wlo = (w - whi.astype(f32)).astype(bf16)
    return whi, wlo, bias


def _token_major(kv_t):
    b, _, n = kv_t.shape
    return kv_t.reshape(b, 2, KV_HEADS, HEAD_DIM, n).transpose(0, 4, 1, 2, 3)


def kernel(x_prompt, x_sample, cache_cmp_kv, cache_sel_kv, cache_win_kv, state_hgrn, page_table, attn_w_in, attn_cmp_alpha, attn_w_out, rec_w_in, rec_lb_logits, rec_norm_g, rec_w_out, ln_g, ln_b, moe_w_router_c, moe_b_router_c, moe_w_router_e, moe_b_router_e, moe_w_gate, moe_w_up, moe_w_down):
    b, seq, d = x_prompt.shape
    bs, lq, _ = x_sample.shape
    tp, ts = b * seq, bs * lq
    lb_all = jnp.cumsum(jax.nn.softmax(rec_lb_logits.astype(f32), axis=0), axis=0)
    hp, hs = x_prompt.reshape(tp, d), x_sample.reshape(ts, d)
    cache_cmp_t = cache_cmp_kv.transpose(0, 1, 3, 4, 5, 2)
    cache_sel_t = cache_sel_kv.transpose(0, 1, 3, 4, 5, 2)
    kv_out = [[] for _ in range(6)]
    rec_out = [[], []]
    for layer in range(DEPTH):
        j = layer // 2
        if layer % 2 == 0:
            nq, n_in = N_HEADS * HEAD_DIM, attn_w_in.shape[2]
            w_all = attn_w_in[j]
            w_in = jnp.pad(w_all, ((0, 0), (0, -(-n_in // LANES) * LANES - n_in))).astype(bf16)
            w_qg = jnp.concatenate([w_all[:, :nq], w_all[:, nq + 6 * KV_WIDTH:]], axis=1)
            w_qg = jnp.pad(w_qg, ((0, 0), (0, -(-w_qg.shape[1] // LANES) * LANES - w_qg.shape[1]))).astype(bf16)
            w_kv_t = w_all[:, nq:nq + 6 * KV_WIDTH].T.astype(bf16)
            alpha_rows = jnp.repeat(attn_cmp_alpha[j], HEAD_DIM, axis=-1).reshape(CMP_BLOCK, 2 * KV_WIDTH).T
            w_out = attn_w_out[j].astype(bf16)
            op, c_p, s_p, w_p = _nsa_prompt(hp.reshape(b, seq, d), w_qg, w_kv_t, alpha_rows)
            os_, c_s, s_s, w_s = _nsa_sample(hs.reshape(bs, lq, d), cache_cmp_t, cache_sel_t, cache_win_kv,
                                             page_table, j, w_in, alpha_rows)
            kvs = (bs, lq, 2, KV_HEADS, HEAD_DIM)
            n_win = min(WINDOW, seq)
            for lst, val in zip(kv_out, (_token_major(c_p), c_s.reshape(kvs), _token_major(s_p), s_s.reshape(kvs),
                                         _token_major(w_p[:, :, seq - n_win:]),
                                         w_s.reshape(bs, -1, 2, KV_HEADS, HEAD_DIM))):
                lst.append(val)
        else:
            lb = lb_all[layer] - lb_all[0]
            w_in = rec_w_in[j].astype(bf16)
            w_out = rec_w_out[j].astype(bf16)
            op, st_p = _hgrn_mixer(hp.reshape(b, seq, d), jnp.zeros((b, HG_HEADS, HG_DK, HG_DV), f32), w_in, lb,
                                   rec_norm_g[j], 256)
            os_, st_s = _hgrn_mixer(hs.reshape(bs, lq, d), state_hgrn[j], w_in, lb, rec_norm_g[j], 256)
            rec_out[0].append(st_p)
            rec_out[1].append(st_s)
        hp = _proj_ln(op, w_out, hp, ln_g[layer, 0], ln_b[layer, 0], 512, "out_proj_ln")
        hs = _proj_ln(os_, w_out, hs, ln_g[layer, 0], ln_b[layer, 0], ts, "out_proj_ln_s")
        whi, wlo, rbias = _router_weights(moe_w_router_c[layer], moe_b_router_c[layer], moe_w_router_e[layer],
                                          moe_b_router_e[layer])
        wg, wu, wd = moe_w_gate[layer].astype(bf16), moe_w_up[layer].astype(bf16), moe_w_down[layer].astype(bf16)
        hp = _moe_ln(hp, _router(hp, whi, wlo, rbias, 512), wg, wu, wd, ln_g[layer, 1], ln_b[layer, 1], 512)
        hs = _moe_ln(hs, _router(hs, whi, wlo, rbias, ts), wg, wu, wd, ln_g[layer, 1], ln_b[layer, 1], ts)
    return (hp.reshape(b, seq, d), hs.reshape(bs, lq, d), *[jnp.stack(v) for v in kv_out],
            jnp.stack(rec_out[0]), jnp.stack(rec_out[1]))
```

```python
import functools

import jax
import jax.numpy as jnp
from jax import lax
from jax.experimental import pallas as pl
from jax.experimental.pallas import tpu as pltpu

f32, bf16, i32 = jnp.float32, jnp.bfloat16, jnp.int32

DEPTH = 2
N_HEADS, KV_HEADS, HEAD_DIM = 16, 4, 64
GROUP = N_HEADS // KV_HEADS
KV_WIDTH = KV_HEADS * HEAD_DIM
CMP_BLOCK, TOP_K, WINDOW = 64, 16, 512
FORCED_SCORE = float(GROUP + 1)
PAGE_SIZE = 128
HG_HEADS, HG_DK, HG_DV, HG_CHUNK = 8, 128, 128, 32
N_GROUPS, EXPERTS_PER_GROUP = 4, 8
ALPHA = (2.0 * DEPTH) ** 0.25
LN_EPS, RMS_EPS = 1e-5, 1e-6

NEG = -1e30
LOG2E = 1.4426950408889634
CMP_SHIFT = CMP_BLOCK.bit_length() - 1
assert 1 << CMP_SHIFT == CMP_BLOCK
LANES = 128
HG_HB = 8
VMEM_LIMIT = 56 * 1024 * 1024


def _cp(*sem):
    return pltpu.CompilerParams(dimension_semantics=sem, vmem_limit_bytes=VMEM_LIMIT)


def _sigmoid(x):
    return 1.0 / (1.0 + jnp.exp(-x))


def _log_sigmoid(x):
    return jnp.minimum(x, 0.0) - jnp.log1p(jnp.exp(-jnp.abs(x)))


def _layer_norm(z, g, b):
    mu = jnp.mean(z, axis=-1, keepdims=True)
    d = z - mu
    var = jnp.mean(d * d, axis=-1, keepdims=True)
    return d * lax.rsqrt(var + LN_EPS) * g + b


def _mm_kernel(x_ref, w_ref, o_ref):
    o_ref[...] = jnp.dot(x_ref[...].astype(bf16), w_ref[...], preferred_element_type=f32).astype(o_ref.dtype)


def _matmul(x, w, tm, name):
    m, k = x.shape
    n = w.shape[1]
    return pl.pallas_call(
        _mm_kernel, grid=(m // tm,),
        in_specs=[pl.BlockSpec((tm, k), lambda i: (i, 0)), pl.BlockSpec((k, n), lambda i: (0, 0))],
        out_specs=pl.BlockSpec((tm, n), lambda i: (i, 0)),
        out_shape=jax.ShapeDtypeStruct((m, n), f32), compiler_params=_cp("arbitrary"), name=name)(x, w)


def _proj_ln_kernel(o_ref, w_ref, x_ref, g_ref, b_ref, h_ref):
    y = jnp.dot(o_ref[...].astype(bf16), w_ref[...], preferred_element_type=f32)
    h_ref[...] = _layer_norm(ALPHA * x_ref[...] + y, g_ref[...], b_ref[...])


def _proj_ln(o, w, x, g, b, tm, name):
    m, k = o.shape
    n = w.shape[1]
    row = lambda i: (i, 0)
    fix = lambda i: (0, 0)
    return pl.pallas_call(
        _proj_ln_kernel, grid=(m // tm,),
        in_specs=[pl.BlockSpec((tm, k), row), pl.BlockSpec((k, n), fix), pl.BlockSpec((tm, n), row),
                  pl.BlockSpec((1, n), fix), pl.BlockSpec((1, n), fix)],
        out_specs=pl.BlockSpec((tm, n), row),
        out_shape=jax.ShapeDtypeStruct((m, n), f32), compiler_params=_cp("arbitrary"), name=name)(
            o, w, x, g.reshape(1, n), b.reshape(1, n))


def _split3(x):
    hi = x.astype(bf16)
    r1 = x - hi.astype(f32)
    mid = r1.astype(bf16)
    return hi, mid, (r1 - mid.astype(f32)).astype(bf16)


def _dot_nt(a, b):
    return lax.dot_general(a, b, (((1,), (1,)), ((), ())), preferred_element_type=f32)


def _block_sum_lanes(y, onehot):
    hi = y.astype(bf16)
    return _dot_nt(hi, onehot) + _dot_nt((y - hi.astype(f32)).astype(bf16), onehot)


def _compress_kernel(x_ref, a_ref, e_ref, o_ref):
    o_ref[0, 0] = _block_sum_lanes(x_ref[0, 0] * a_ref[0], e_ref[...])


def _compress_t(kvt, alpha_l, onehot):
    b, ch, hd, seq = kvt.shape
    nbp = onehot.shape[0]
    return pl.pallas_call(
        _compress_kernel, grid=(b, ch),
        in_specs=[pl.BlockSpec((1, 1, hd, seq), lambda i, j: (i, j, 0, 0)),
                  pl.BlockSpec((1, 1, seq), lambda i, j: (j, 0, 0)),
                  pl.BlockSpec((nbp, seq), lambda i, j: (0, 0))],
        out_specs=pl.BlockSpec((1, 1, hd, nbp), lambda i, j: (i, j, 0, 0)),
        out_shape=jax.ShapeDtypeStruct((b, ch, hd, nbp), f32),
        compiler_params=_cp("arbitrary", "arbitrary"), name="nsa_compress")(kvt, alpha_l, onehot)


def _compress_paged_kernel(pt_ref, *refs, n_pg):
    del pt_ref
    pages, a_ref, e_ref, o_ref = refs[:n_pg], refs[n_pg], refs[n_pg + 1], refs[n_pg + 2]
    w = 2 * KV_WIDTH
    x = jnp.concatenate([r[0, 0].reshape(w, PAGE_SIZE) for r in pages], axis=1)
    o_ref[0, 0] = _block_sum_lanes(x * a_ref[...], e_ref[...])


def _compress_paged(cache_t, layer, page_table, alpha_rows, n_pg):
    nb_, n_pages = page_table.shape
    bpp = PAGE_SIZE // CMP_BLOCK
    n_tok = n_pg * PAGE_SIZE
    a_t = jnp.tile(alpha_rows, (1, n_pg * bpp))
    onehot = (jnp.arange(n_pg * bpp)[:, None] == jnp.arange(n_tok)[None, :] // CMP_BLOCK).astype(bf16)

    def page_map(b, i, pt, k):
        return (layer, pt[b, i * n_pg + k], 0, 0, 0, 0)

    fix = lambda b, i, pt: (0, 0)
    grid_spec = pltpu.PrefetchScalarGridSpec(
        num_scalar_prefetch=1, grid=(nb_, n_pages // n_pg),
        in_specs=[pl.BlockSpec((1, 1, 2, KV_HEADS, HEAD_DIM, PAGE_SIZE), functools.partial(page_map, k=k))
                  for k in range(n_pg)]
        + [pl.BlockSpec(a_t.shape, fix), pl.BlockSpec(onehot.shape, fix)],
        out_specs=pl.BlockSpec((1, 1, 2 * KV_WIDTH, n_pg * bpp), lambda b, i, pt: (b, i, 0, 0)))
    return pl.pallas_call(
        functools.partial(_compress_paged_kernel, n_pg=n_pg), grid_spec=grid_spec,
        out_shape=jax.ShapeDtypeStruct((nb_, n_pages // n_pg, 2 * KV_WIDTH, n_pg * bpp), f32),
        compiler_params=_cp("arbitrary", "arbitrary"), name="nsa_compress_paged")(
            page_table, *([cache_t] * n_pg), a_t, onehot)


def _select_bias(score, k_sel):
    rowf = lax.broadcasted_iota(i32, score.shape, 0).astype(f32)
    picked = jnp.zeros(score.shape, f32)
    work = score
    for _ in range(k_sel):
        mx = jnp.max(work, axis=0, keepdims=True)
        first = jnp.min(jnp.where(work == mx, rowf, 1e9), axis=0, keepdims=True)
        pick = rowf == first
        picked = jnp.where(pick, 1.0, picked)
        work = jnp.where(pick, -jnp.inf, work)
    return jnp.where(picked > 0.0, 0.0, NEG)


def _masked_softmax_cols(s, mask, exp_fn):
    sm = jnp.where(mask, s, NEG)
    m = jnp.max(sm, axis=0, keepdims=True)
    e = jnp.where(mask, exp_fn(sm - m), 0.0)
    return e / jnp.maximum(jnp.sum(e, axis=0, keepdims=True), 1e-30)


def _nsa_in_proj_kernel(x_ref, wq_ref, wkv_ref, pq_ref, c_ref, s_ref, w_ref):
    xb = x_ref[0].astype(bf16)
    pq_ref[0] = jnp.dot(xb, wq_ref[...], preferred_element_type=f32)
    kvt = _dot_nt(wkv_ref[...], xb)
    w = 2 * KV_WIDTH
    for n, ref in enumerate((c_ref, s_ref, w_ref)):
        ref[0] = kvt[n * w:(n + 1) * w]


def _nsa_in_proj(x, wq, wkv_t, tm):
    b, seq, d = x.shape
    nqg, w = wq.shape[1], 2 * KV_WIDTH
    kv_spec = pl.BlockSpec((1, w, tm), lambda i, j: (i, 0, j))
    kv_shape = jax.ShapeDtypeStruct((b, w, seq), f32)
    return pl.pallas_call(
        _nsa_in_proj_kernel, grid=(b, seq // tm),
        in_specs=[pl.BlockSpec((1, tm, d), lambda i, j: (i, j, 0)), pl.BlockSpec((d, nqg), lambda i, j: (0, 0)),
                  pl.BlockSpec((3 * w, d), lambda i, j: (0, 0))],
        out_specs=[pl.BlockSpec((1, tm, nqg), lambda i, j: (i, j, 0)), kv_spec, kv_spec, kv_spec],
        out_shape=[jax.ShapeDtypeStruct((b, seq, nqg), f32), kv_shape, kv_shape, kv_shape],
        compiler_params=_cp("arbitrary", "arbitrary"), name="nsa_in_proj")(x, wq, wkv_t)


def _nsa_prompt_kernel(qt_ref, g_ref, kct_ref, vct_ref, kst_ref, vst_ref, kwt_ref, vwt_ref, et_ref, o_ref,
                       m_ref, l_ref, acc_ref, sa_ref, sb_ref, *, tq, tk, nb):
    s0 = pl.program_id(1) * tq
    c = GROUP * tq
    nbp = et_ref.shape[1]
    qt = qt_ref[0, 0, 0]
    tq_pos = s0 + lax.broadcasted_iota(i32, (1, tq), 1)
    tpos = jnp.concatenate([tq_pos] * GROUP, axis=1)

    kc = kct_ref[0, 0, 0].T.astype(bf16)
    sc = jnp.dot(kc, qt, preferred_element_type=f32)
    blk_c = lax.broadcasted_iota(i32, (nbp, c), 0)
    pc = _masked_softmax_cols(sc, blk_c * CMP_BLOCK + (CMP_BLOCK - 1) <= tpos, jnp.exp2)
    oc = jnp.dot(vct_ref[0, 0, 0].astype(bf16), pc.astype(bf16), preferred_element_type=f32)

    imp = pc[:, 0:tq]
    for g in range(1, GROUP):
        imp = imp + pc[:, g * tq:(g + 1) * tq]
    blk = lax.broadcasted_iota(i32, (nbp, tq), 0)
    cur = jnp.right_shift(tq_pos, CMP_SHIFT)
    forced = (blk == 0) | (blk == cur) | (blk == cur - 1)
    base = jnp.where(forced, FORCED_SCORE, jnp.where(blk * CMP_BLOCK <= tq_pos, 0.0, -1.0))
    sel = _select_bias(jnp.where(blk < nb, base + imp, -3.0), min(TOP_K, nb)).astype(bf16)
    qa = jnp.concatenate([jnp.concatenate([sel] * GROUP, axis=1), qt], axis=0)

    m_ref[...] = jnp.full(m_ref.shape, NEG, f32)
    l_ref[...] = jnp.zeros(l_ref.shape, f32)
    acc_ref[...] = jnp.zeros(acc_ref.shape, f32)

    def produce(j, s_ref):
        k0 = pl.multiple_of(j * tk, tk)
        kk = kst_ref[0, 0, 0, :, pl.ds(k0, tk)].T.astype(bf16)
        s_ref[...] = jnp.dot(jnp.concatenate([et_ref[pl.ds(k0, tk), :], kk], axis=1), qa, preferred_element_type=f32)

    def consume(j, s_ref, causal):
        k0 = pl.multiple_of(j * tk, tk)
        s = s_ref[...]
        if causal:
            s = jnp.where(k0 + lax.broadcasted_iota(i32, (tk, c), 0) <= tpos, s, NEG)
        m_new = jnp.maximum(m_ref[...], jnp.max(s, axis=0, keepdims=True))
        a = jnp.exp2(m_ref[...] - m_new)
        p = jnp.exp2(s - m_new)
        l_ref[...] = a * l_ref[...] + jnp.sum(p, axis=0, keepdims=True)
        acc_ref[...] = a * acc_ref[...] + jnp.dot(vst_ref[0, 0, 0, :, pl.ds(k0, tk)].astype(bf16), p.astype(bf16),
                                                  preferred_element_type=f32)
        m_ref[...] = m_new

    n_full = s0 // tk
    produce(0, sa_ref)

    def pair(i, carry):
        produce(2 * i + 1, sb_ref)
        consume(2 * i, sa_ref, False)
        produce(2 * i + 2, sa_ref)
        consume(2 * i + 1, sb_ref, False)
        return carry

    lax.fori_loop(0, n_full // 2, pair, 0)

    @pl.when(n_full % 2 == 1)
    def _():
        produce(n_full, sb_ref)
        consume(n_full - 1, sa_ref, False)
        consume(n_full, sb_ref, True)

    @pl.when(n_full % 2 == 0)
    def _():
        consume(n_full, sa_ref, True)

    osel = acc_ref[...] / jnp.maximum(l_ref[...], 1e-30)

    wt = WINDOW + tq
    k0w = pl.multiple_of(jnp.maximum(s0 - WINDOW, 0), LANES)
    kw = kwt_ref[0, 0, 0, :, pl.ds(k0w, wt)].T.astype(bf16)
    s = jnp.dot(kw, qt, preferred_element_type=f32)
    dist = tpos - (k0w + lax.broadcasted_iota(i32, (wt, c), 0))
    s = jnp.where(dist >= 0, jnp.where(dist < WINDOW, s, NEG), NEG)
    p = jnp.exp2(s - jnp.max(s, axis=0, keepdims=True))
    lw = jnp.sum(p, axis=0, keepdims=True)
    ow = jnp.dot(vwt_ref[0, 0, 0, :, pl.ds(k0w, wt)].astype(bf16), p.astype(bf16), preferred_element_type=f32)
    ow = ow / jnp.maximum(lw, 1e-30)

    gt = _sigmoid(g_ref[0, 0, 0])
    o_ref[0, 0, 0] = (gt[0:1] * oc + gt[1:2] * osel + gt[2:3] * ow).astype(o_ref.dtype)


def _nsa_prompt_attention(qt, gt, cblk_t, kvs_t, kvw_t, onehot_t, tq, tk):
    b, kvh, n_t, hd, c = qt.shape
    seq, nbp = onehot_t.shape
    assert tk % tq == 0
    tile = lambda bh, i: (bh // kvh, bh % kvh, i, 0, 0)
    comp = lambda cc: (lambda bh, i: (bh // kvh, cc, bh % kvh, 0, 0))
    rows = lambda n: (pl.BlockSpec((1, 1, 1, hd, n), comp(0)), pl.BlockSpec((1, 1, 1, hd, n), comp(1)))
    return pl.pallas_call(
        functools.partial(_nsa_prompt_kernel, tq=tq, tk=tk, nb=seq // CMP_BLOCK), grid=(b * kvh, n_t),
        in_specs=[pl.BlockSpec((1, 1, 1, hd, c), tile), pl.BlockSpec((1, 1, 1, 3, c), tile),
                  *rows(nbp), *rows(seq), *rows(seq), pl.BlockSpec((seq, nbp), lambda bh, i: (0, 0))],
        out_specs=pl.BlockSpec((1, 1, 1, hd, c), tile),
        out_shape=jax.ShapeDtypeStruct((b, kvh, n_t, hd, c), bf16),
        scratch_shapes=[pltpu.VMEM((1, c), f32), pltpu.VMEM((1, c), f32), pltpu.VMEM((hd, c), f32),
                        pltpu.VMEM((tk, c), f32), pltpu.VMEM((tk, c), f32)],
        compiler_params=_cp("arbitrary", "arbitrary"), name="nsa_prompt_attn")(
            qt, gt, cblk_t, cblk_t, kvs_t, kvs_t, kvw_t, kvw_t, onehot_t)


def _nsa_prompt(x, wq, wkv_t, alpha_rows, tq=128, tk=512):
    b, seq, d = x.shape
    n_t, nb = seq // tq, seq // CMP_BLOCK
    nq = N_HEADS * HEAD_DIM
    pq, kvc_t, kvs_t, kvw_t = _nsa_in_proj(x, wq, wkv_t, 512)
    five = lambda a: a.reshape(b, 2, KV_HEADS, HEAD_DIM, -1)
    nbp = -(-nb // LANES) * LANES
    onehot = (jnp.arange(nbp)[:, None] == jnp.arange(seq)[None, :] // CMP_BLOCK).astype(bf16)
    alpha_l = jnp.tile(alpha_rows[::HEAD_DIM], (1, nb)).reshape(2 * KV_HEADS, 1, seq)
    cblk_t = _compress_t(kvc_t.reshape(b, 2 * KV_HEADS, HEAD_DIM, seq), alpha_l, onehot)
    qs = (pq[..., :nq] * (HEAD_DIM ** -0.5 * LOG2E)).astype(bf16).reshape(b, n_t, tq, KV_HEADS, GROUP, HEAD_DIM)
    qt = qs.transpose(0, 3, 1, 5, 4, 2).reshape(b, KV_HEADS, n_t, HEAD_DIM, GROUP * tq)
    gt = pq[..., nq:nq + 3 * N_HEADS].reshape(b, n_t, tq, KV_HEADS, GROUP, 3)
    gt = gt.transpose(0, 3, 1, 5, 4, 2).reshape(b, KV_HEADS, n_t, 3, GROUP * tq)
    ot = _nsa_prompt_attention(qt, gt, five(cblk_t), five(kvs_t), five(kvw_t), onehot.T, tq, tk)
    o = ot.reshape(b, KV_HEADS, n_t, HEAD_DIM, GROUP, tq).transpose(0, 2, 5, 1, 4, 3).reshape(b * seq, d)
    return o, kvc_t, kvs_t, kvw_t


def _dot_t0(a, b):
    return lax.dot_general(a, b, (((0,), (0,)), ((), ())), preferred_element_type=f32)


def _smp_cmp_win_kernel(q_ref, kc_ref, vc_ref, cn_ref, a_ref, kw_ref, vw_ref, oc_ref, ow_ref, sel_ref, *, past, lq,
                        nb_real):
    q = q_ref[0]
    nbp = kc_ref.shape[1]
    col = lax.broadcasted_iota(i32, (1, LANES), 1)
    tpos = past + jnp.bitwise_and(col, lq - 1)

    c_new = jnp.sum(cn_ref[0] * a_ref[...], axis=0, keepdims=True).astype(bf16)
    is_new = lax.broadcasted_iota(i32, (nbp, KV_WIDTH), 0) == nb_real - 1
    kc = jnp.where(is_new, c_new[:, :KV_WIDTH], kc_ref[0])
    vc = jnp.where(is_new, c_new[:, KV_WIDTH:], vc_ref[0])
    sc = jnp.dot(kc, q, preferred_element_type=f32)
    blk = lax.broadcasted_iota(i32, (nbp, LANES), 0)
    pc = _masked_softmax_cols(sc, blk * CMP_BLOCK + (CMP_BLOCK - 1) <= tpos, jnp.exp)
    oc_ref[0] = _dot_t0(pc.astype(bf16), vc)

    r = lax.broadcasted_iota(i32, (LANES, LANES), 0)
    cc = lax.broadcasted_iota(i32, (LANES, LANES), 1)
    per_kv = GROUP * lq
    same = jnp.where(r // per_kv == cc // per_kv, jnp.where(r % lq == cc % lq, 1.0, 0.0), 0.0).astype(bf16)
    hi, mid, lo = _split3(pc)
    imp = (jnp.dot(hi, same, preferred_element_type=f32) + jnp.dot(mid, same, preferred_element_type=f32)
           + jnp.dot(lo, same, preferred_element_type=f32))
    cur = jnp.right_shift(tpos, CMP_SHIFT)
    forced = (blk == 0) | (blk == cur) | (blk == cur - 1)
    base = jnp.where(forced, FORCED_SCORE, jnp.where(blk * CMP_BLOCK <= tpos, 0.0, -1.0))
    score = jnp.where(blk < nb_real, base + imp, -3.0)
    sel_ref[0] = _select_bias(score, min(TOP_K, nb_real))

    nwt = kw_ref.shape[1]
    nw = nwt - lq
    s = jnp.dot(kw_ref[0], q, preferred_element_type=f32)
    wpos = past - nw + lax.broadcasted_iota(i32, (nwt, LANES), 0)
    dist = tpos - wpos
    ok = (dist >= 0) & (dist < WINDOW) & (wpos >= 0)
    pw = _masked_softmax_cols(s, ok, jnp.exp)
    ow_ref[0] = _dot_t0(pw.astype(bf16), vw_ref[0])


def _smp_sel_kernel(pt_ref, *refs, n_pg):
    del pt_ref
    pages = refs[:n_pg]
    q_ref, sel_ref, e_ref, kn_ref, vn_ref, bn_ref, g_ref, oc_ref, ow_ref, o_ref, m_ref, l_ref, acc_ref = refs[n_pg:]
    i = pl.program_id(1)

    @pl.when(i == 0)
    def _():
        m_ref[...] = jnp.full(m_ref.shape, NEG, f32)
        l_ref[...] = jnp.zeros(l_ref.shape, f32)
        acc_ref[...] = jnp.zeros(acc_ref.shape, f32)

    def update(s, vt):
        m_new = jnp.maximum(m_ref[...], jnp.max(s, axis=1, keepdims=True))
        a = jnp.exp(m_ref[...] - m_new)
        p = jnp.exp(s - m_new)
        l_ref[...] = a * l_ref[...] + jnp.sum(p, axis=1, keepdims=True)
        acc_ref[...] = a * acc_ref[...] + _dot_nt(p.astype(bf16), vt)
        m_ref[...] = m_new

    qa = jnp.concatenate([q_ref[0], sel_ref[0, 0].astype(bf16)], axis=1)
    kt = jnp.concatenate([r[0, 0, 0].reshape(KV_WIDTH, PAGE_SIZE) for r in pages], axis=1).astype(bf16)
    vt = jnp.concatenate([r[0, 0, 1].reshape(KV_WIDTH, PAGE_SIZE) for r in pages], axis=1).astype(bf16)
    update(jnp.dot(qa, jnp.concatenate([kt, e_ref[...]], axis=0), preferred_element_type=f32), vt)

    @pl.when(i == pl.num_programs(1) - 1)
    def _():
        update(jnp.dot(q_ref[0], kn_ref[0].astype(bf16), preferred_element_type=f32) + bn_ref[0], vn_ref[0].astype(bf16))
        gt = _sigmoid(g_ref[0])
        osel = acc_ref[...] / jnp.maximum(l_ref[...], 1e-30)
        o_ref[0] = gt[:, 0:1] * oc_ref[0] + gt[:, 1:2] * osel + gt[:, 2:3] * ow_ref[0]


def _nsa_sample(x, cache_cmp_t, cache_sel_t, cache_win, page_table, layer, w_in, alpha_rows, n_pg=16):
    b, lq, d = x.shape
    n_pages = page_table.shape[1]
    past = n_pages * PAGE_SIZE
    assert KV_HEADS * GROUP * lq == LANES and past % CMP_BLOCK == 0 and lq <= CMP_BLOCK
    assert n_pages % n_pg == 0 and n_pg * PAGE_SIZE // CMP_BLOCK <= LANES
    p = _matmul(x.reshape(b * lq, d), w_in, b * lq, "nsa_in_proj_s")
    nq = N_HEADS * HEAD_DIM
    w2 = 2 * KV_WIDTH
    q = (p[:, :nq] * HEAD_DIM ** -0.5).reshape(b, lq, KV_HEADS, GROUP, HEAD_DIM)
    eye = jnp.eye(KV_HEADS, dtype=f32)
    qbd = (q.transpose(0, 2, 3, 1, 4)[:, :, :, :, None, :] * eye[None, :, None, None, :, None])
    qbd = qbd.reshape(b, LANES, KV_WIDTH).astype(bf16)
    qbd_t = qbd.transpose(0, 2, 1)
    gt = p[:, nq + 6 * KV_WIDTH:nq + 6 * KV_WIDTH + 3 * N_HEADS].reshape(b, lq, KV_HEADS, GROUP, 3)
    gt = gt.transpose(0, 2, 3, 1, 4).reshape(b, LANES, 3)
    kv = p[:, nq:nq + 6 * KV_WIDTH].reshape(b, lq, 3, w2)
    kv_c, kv_s, kv_w = kv[:, :, 0], kv[:, :, 1], kv[:, :, 2]

    bpp = PAGE_SIZE // CMP_BLOCK
    c_past = _compress_paged(cache_cmp_t, layer, page_table, alpha_rows, n_pg)
    c_past = c_past.transpose(0, 1, 3, 2).reshape(b, past // CMP_BLOCK, w2)
    nb_real = past // CMP_BLOCK + 1
    nbp = -(-nb_real // 8) * 8
    cblk = jnp.pad(c_past, ((0, 0), (0, nbp - nb_real + 1), (0, 0)))
    wkv = jnp.concatenate([cache_win[layer].reshape(b, -1, w2), kv_w], axis=1)
    nwt = wkv.shape[1]

    seq3 = lambda i: (i, 0, 0)
    oc, ow, sel = pl.pallas_call(
        functools.partial(_smp_cmp_win_kernel, past=past, lq=lq, nb_real=nb_real), grid=(b,),
        in_specs=[pl.BlockSpec((1, KV_WIDTH, LANES), seq3), pl.BlockSpec((1, nbp, KV_WIDTH), seq3),
                  pl.BlockSpec((1, nbp, KV_WIDTH), seq3), pl.BlockSpec((1, lq, w2), seq3),
                  pl.BlockSpec((lq, w2), lambda i: (0, 0)), pl.BlockSpec((1, nwt, KV_WIDTH), seq3),
                  pl.BlockSpec((1, nwt, KV_WIDTH), seq3)],
        out_specs=[pl.BlockSpec((1, LANES, KV_WIDTH), seq3), pl.BlockSpec((1, LANES, KV_WIDTH), seq3),
                   pl.BlockSpec((1, nbp, LANES), seq3)],
        out_shape=[jax.ShapeDtypeStruct((b, LANES, KV_WIDTH), f32), jax.ShapeDtypeStruct((b, LANES, KV_WIDTH), f32),
                   jax.ShapeDtypeStruct((b, nbp, LANES), f32)],
        compiler_params=_cp("arbitrary"), name="nsa_sample_cmp_win")(
            qbd_t, cblk[:, :, :KV_WIDTH].astype(bf16), cblk[:, :, KV_WIDTH:].astype(bf16), kv_c, alpha_rows.T[:lq],
            wkv[:, :, :KV_WIDTH].astype(bf16), wkv[:, :, KV_WIDTH:].astype(bf16))

    n_steps, bps = n_pages // n_pg, n_pg * bpp
    sel_r = sel.transpose(0, 2, 1)
    sel_steps = sel_r[:, :, :past // CMP_BLOCK].reshape(b, LANES, n_steps, bps).transpose(0, 2, 1, 3)
    sel_steps = jnp.pad(sel_steps, ((0, 0), (0, 0), (0, 0), (0, LANES - bps)))
    onehot = (jnp.arange(LANES)[:, None] == jnp.arange(n_pg * PAGE_SIZE)[None, :] // CMP_BLOCK).astype(bf16)
    pad_t = lambda a: jnp.pad(a.transpose(0, 2, 1), ((0, 0), (0, 0), (0, LANES - lq)))
    qoff = jnp.arange(LANES) % lq
    readable = jnp.arange(LANES)[None, :] <= qoff[:, None]
    bias_new = jnp.where(readable[None], sel_r[:, :, past // CMP_BLOCK][:, :, None], NEG)

    def page_map(bb, i, pt, k):
        return (layer, pt[bb, i * n_pg + k], 0, 0, 0, 0)

    seq3p = lambda bb, i, pt: (bb, 0, 0)
    grid_spec = pltpu.PrefetchScalarGridSpec(
        num_scalar_prefetch=1, grid=(b, n_steps),
        in_specs=[pl.BlockSpec((1, 1, 2, KV_HEADS, HEAD_DIM, PAGE_SIZE), functools.partial(page_map, k=k))
                  for k in range(n_pg)]
        + [pl.BlockSpec((1, LANES, KV_WIDTH), seq3p), pl.BlockSpec((1, 1, LANES, LANES), lambda bb, i, pt: (bb, i, 0, 0)),
           pl.BlockSpec(onehot.shape, lambda bb, i, pt: (0, 0)),
           pl.BlockSpec((1, KV_WIDTH, LANES), seq3p), pl.BlockSpec((1, KV_WIDTH, LANES), seq3p),
           pl.BlockSpec((1, LANES, LANES), seq3p), pl.BlockSpec((1, LANES, 3), seq3p),
           pl.BlockSpec((1, LANES, KV_WIDTH), seq3p), pl.BlockSpec((1, LANES, KV_WIDTH), seq3p)],
        out_specs=pl.BlockSpec((1, LANES, KV_WIDTH), seq3p),
        scratch_shapes=[pltpu.VMEM((LANES, 1), f32), pltpu.VMEM((LANES, 1), f32), pltpu.VMEM((LANES, KV_WIDTH), f32)])
    o = pl.pallas_call(
        functools.partial(_smp_sel_kernel, n_pg=n_pg), grid_spec=grid_spec,
        out_shape=jax.ShapeDtypeStruct((b, LANES, KV_WIDTH), f32),
        compiler_params=_cp("arbitrary", "arbitrary"), name="nsa_sample_sel")(
            page_table, *([cache_sel_t] * n_pg), qbd, sel_steps, onehot, pad_t(kv_s[:, :, :KV_WIDTH]),
            pad_t(kv_s[:, :, KV_WIDTH:]), bias_new, gt, oc, ow)
    o = o.reshape(b, KV_HEADS, GROUP, lq, KV_HEADS, HEAD_DIM)
    o = jnp.stack([o[:, h, :, :, h] for h in range(KV_HEADS)], axis=1)
    o = o.transpose(0, 3, 1, 2, 4).reshape(b * lq, d)
    return o, kv_c, kv_s, wkv[:, lq:]


def _hgrn_kernel(zq_ref, zf_ref, zi_ref, zg_ref, lb_ref, ng_ref, s0_ref, o_ref, sout_ref, st_ref, *, n_valid, hb):
    t = pl.program_id(2)
    tb = zq_ref.shape[1]
    n_ch = tb // HG_CHUNK

    @pl.when(t == 0)
    def _():
        for hh in range(hb):
            st_ref[hh] = s0_ref[0, hh].T

    r = lax.broadcasted_iota(i32, (tb, tb), 0)
    cc = lax.broadcasted_iota(i32, (tb, tb), 1)
    tri = jnp.where(r // HG_CHUNK == cc // HG_CHUNK, jnp.where(cc <= r, 1.0, 0.0), 0.0).astype(bf16)
    causal = lax.broadcasted_iota(i32, (HG_CHUNK, HG_CHUNK), 1) <= lax.broadcasted_iota(i32, (HG_CHUNK, HG_CHUNK), 0)
    mid_row = HG_CHUNK // 2

    zf = zf_ref[0]
    lb = lb_ref[0]
    la = jnp.log(lb)
    l1 = jnp.log1p(-lb)
    bb = l1 + _log_sigmoid(zf)
    log_f = jnp.maximum(la, bb) + jnp.log1p(jnp.exp(-jnp.abs(la - bb)))
    kk = jnp.exp(l1 + _log_sigmoid(-zf))
    if n_valid < tb:
        live = lax.broadcasted_iota(i32, zf.shape, 0) < n_valid
        log_f = jnp.where(live, log_f, 0.0)
        kk = jnp.where(live, kk, 0.0)
    hi, mid, lo = _split3(log_f)
    cum = (jnp.dot(tri, hi, preferred_element_type=f32) + jnp.dot(tri, mid, preferred_element_type=f32)
           + jnp.dot(tri, lo, preferred_element_type=f32))
    zq, zi, zg = zq_ref[0], zi_ref[0], zg_ref[0]
    gate = ng_ref[0] * (zg * _sigmoid(zg))

    qs, o_in, upd, dec = {}, {}, {}, {}
    for hh in range(hb):
        hs = slice(hh * HG_DK, (hh + 1) * HG_DK)
        for ci in range(n_ch):
            sl = slice(ci * HG_CHUNK, (ci + 1) * HG_CHUNK)
            cm, qi, ki, vi = cum[sl, hs], zq[sl, hs], kk[sl, hs], zi[sl, hs]
            ref = cm[mid_row:mid_row + 1]
            last = cm[HG_CHUNK - 1:HG_CHUNK]
            a = _dot_nt((qi * jnp.exp(cm - ref)).astype(bf16), (ki * jnp.exp(ref - cm)).astype(bf16))
            a = jnp.where(causal, a, 0.0)
            o_in[hh, ci] = jnp.dot(a.astype(bf16), vi.astype(bf16), preferred_element_type=f32)
            qs[hh, ci] = (qi * jnp.exp(cm)).astype(bf16)
            upd[hh, ci] = _dot_t0(vi.astype(bf16), (ki * jnp.exp(last - cm)).astype(bf16))
            dec[hh, ci] = jnp.exp(last)
    for hh in range(hb):
        hs = slice(hh * HG_DK, (hh + 1) * HG_DK)
        st = st_ref[hh]
        outs = []
        for ci in range(n_ch):
            outs.append(o_in[hh, ci] + _dot_nt(qs[hh, ci], st.astype(bf16)))
            st = st * dec[hh, ci] + upd[hh, ci]
        st_ref[hh] = st
        o = jnp.concatenate(outs, axis=0) if n_ch > 1 else outs[0]
        o = o * lax.rsqrt(jnp.mean(o * o, axis=-1, keepdims=True) + RMS_EPS)
        o_ref[0, :, hs] = (o * gate[:, hs]).astype(o_ref.dtype)

        @pl.when(t == pl.num_programs(2) - 1)
        def _():
            sout_ref[0, hh] = st.T


def _hgrn(p, s0, lb, norm_g, tb, n_valid, hb):
    b, lp, _ = p.shape
    ng = HG_HEADS // hb
    w = hb * HG_DK
    col = lambda off: (lambda bb, hh, t: (bb, t, off * ng + hh))
    per_head = lambda bb, hh, t: (0, hh)
    state = lambda bb, hh, t: (bb, hh, 0, 0)
    return pl.pallas_call(
        functools.partial(_hgrn_kernel, n_valid=n_valid, hb=hb), grid=(b, ng, lp // tb),
        in_specs=[pl.BlockSpec((1, tb, w), col(0)), pl.BlockSpec((1, tb, w), col(1)),
                  pl.BlockSpec((1, tb, w), col(2)), pl.BlockSpec((1, tb, w), col(3)),
                  pl.BlockSpec((1, w), per_head), pl.BlockSpec((1, w), per_head),
                  pl.BlockSpec((1, hb, HG_DK, HG_DV), state)],
        out_specs=[pl.BlockSpec((1, tb, w), lambda bb, hh, t: (bb, t, hh)),
                   pl.BlockSpec((1, hb, HG_DK, HG_DV), state)],
        out_shape=[jax.ShapeDtypeStruct((b, lp, HG_HEADS * HG_DV), bf16),
                   jax.ShapeDtypeStruct((b, HG_HEADS, HG_DK, HG_DV), f32)],
        scratch_shapes=[pltpu.VMEM((hb, HG_DV, HG_DK), f32)],
        compiler_params=_cp("arbitrary", "arbitrary", "arbitrary"), name="hgrn_scan")(
            p, p, p, p, lb.reshape(1, -1), norm_g.reshape(1, -1), s0)


def _hgrn_mixer(x, s0, w_in, lb, norm_g, tb):
    b, seq, d = x.shape
    lp = -(-seq // HG_CHUNK) * HG_CHUNK
    tb = min(tb, lp)
    p = _matmul(x.reshape(b * seq, d), w_in, min(256, b * seq), "hgrn_in_proj").reshape(b, seq, 4 * d)
    if lp != seq:
        p = jnp.pad(p, ((0, 0), (0, lp - seq), (0, 0)))
    o, s_out = _hgrn(p, s0, lb, norm_g, tb, seq if lp != seq else lp, HG_HB)
    return o[:, :seq].reshape(b * seq, d), s_out


def _router_kernel(x_ref, whi_ref, wlo_ref, b_ref, cw_ref):
    x = x_ref[...]
    xh = x.astype(bf16)
    xl = (x - xh.astype(f32)).astype(bf16)
    lg = (jnp.dot(xh, whi_ref[...], preferred_element_type=f32) + jnp.dot(xl, whi_ref[...], preferred_element_type=f32)
          + jnp.dot(xh, wlo_ref[...], preferred_element_type=f32)) + b_ref[...]
    ne = N_GROUPS * EXPERTS_PER_GROUP
    lane = lax.broadcasted_iota(i32, lg.shape, 1)
    lanef = lane.astype(f32)
    in_c = (lane >= ne) & (lane < ne + N_GROUPS)
    lc = jnp.where(in_c, lg, -jnp.inf)
    mc = jnp.max(lc, axis=-1, keepdims=True)
    grp = jnp.min(jnp.where(lc == mc, lanef, 1e9), axis=-1, keepdims=True) - float(ne)
    pg = 1.0 / jnp.sum(jnp.where(in_c, jnp.exp(lc - mc), 0.0), axis=-1, keepdims=True)
    e_lo = grp * float(EXPERTS_PER_GROUP)
    in_e = (lanef >= e_lo) & (lanef < e_lo + float(EXPERTS_PER_GROUP))
    le = jnp.where(in_e, lg, -jnp.inf)
    v1 = jnp.max(le, axis=-1, keepdims=True)
    i1 = jnp.min(jnp.where(le == v1, lanef, 1e9), axis=-1, keepdims=True)
    le2 = jnp.where(lanef == i1, -jnp.inf, le)
    v2 = jnp.max(le2, axis=-1, keepdims=True)
    i2 = jnp.min(jnp.where(le2 == v2, lanef, 1e9), axis=-1, keepdims=True)
    e2 = jnp.exp(v2 - v1)
    w1 = pg / (1.0 + e2)
    cw = jnp.where(lanef == i1, w1, 0.0) + jnp.where(lanef == i2, w1 * e2, 0.0)
    for g in range(N_GROUPS):
        cw_ref[g] = cw if g == 0 else pltpu.roll(cw, LANES - g * EXPERTS_PER_GROUP, axis=1)


def _router(x, whi, wlo, bias, tm):
    t, d = x.shape
    return pl.pallas_call(
        _router_kernel, grid=(t // tm,),
        in_specs=[pl.BlockSpec((tm, d), lambda i: (i, 0)), pl.BlockSpec((d, LANES), lambda i: (0, 0)),
                  pl.BlockSpec((d, LANES), lambda i: (0, 0)), pl.BlockSpec((1, LANES), lambda i: (0, 0))],
        out_specs=pl.BlockSpec((N_GROUPS, tm, LANES), lambda i: (0, i, 0)),
        out_shape=jax.ShapeDtypeStruct((N_GROUPS, t, LANES), f32), compiler_params=_cp("arbitrary"),
        name="moe_router")(x, whi, wlo, bias)


def _moe_kernel(x_ref, cw_ref, wg_ref, wu_ref, wd_ref, g_ref, b_ref, o_ref, acc_ref):
    gi = pl.program_id(1)

    @pl.when(gi == 0)
    def _():
        acc_ref[...] = jnp.zeros(acc_ref.shape, f32)

    xb = x_ref[...].astype(bf16)
    cw = cw_ref[0]
    acc = acc_ref[...]
    for e in range(EXPERTS_PER_GROUP):
        hg = jnp.dot(xb, wg_ref[0, e], preferred_element_type=f32)
        hu = jnp.dot(xb, wu_ref[0, e], preferred_element_type=f32)
        h = hg * _sigmoid(hg) * hu * cw[:, e:e + 1]
        acc = acc + jnp.dot(h.astype(bf16), wd_ref[0, e], preferred_element_type=f32)
    acc_ref[...] = acc

    @pl.when(gi == pl.num_programs(1) - 1)
    def _():
        o_ref[...] = _layer_norm(ALPHA * x_ref[...] + acc_ref[...], g_ref[...], b_ref[...])


def _moe_ln(x, cw, wg, wu, wd, g, b, tm):
    t, d = x.shape
    e, hid = wg.shape[1], wg.shape[3]
    row = lambda i, gi: (i, 0)
    fix = lambda i, gi: (0, 0)
    return pl.pallas_call(
        _moe_kernel, grid=(t // tm, N_GROUPS),
        in_specs=[pl.BlockSpec((tm, d), row), pl.BlockSpec((1, tm, LANES), lambda i, gi: (gi, i, 0)),
                  pl.BlockSpec((1, e, d, hid), lambda i, gi: (gi, 0, 0, 0)),
                  pl.BlockSpec((1, e, d, hid), lambda i, gi: (gi, 0, 0, 0)),
                  pl.BlockSpec((1, e, hid, d), lambda i, gi: (gi, 0, 0, 0)),
                  pl.BlockSpec((1, d), fix), pl.BlockSpec((1, d), fix)],
        out_specs=pl.BlockSpec((tm, d), row),
        out_shape=jax.ShapeDtypeStruct((t, d), f32),
        scratch_shapes=[pltpu.VMEM((tm, d), f32)],
        compiler_params=_cp("arbitrary", "arbitrary"), name="moe_dense")(
            x, cw, wg, wu, wd, g.reshape(1, d), b.reshape(1, d))


def _router_weights(w_rc, b_rc, w_re, b_re):
    d = w_rc.shape[0]
    ne = N_GROUPS * EXPERTS_PER_GROUP
    w = jnp.concatenate([w_re.reshape(d, ne), w_rc, jnp.zeros((d, LANES - ne - N_GROUPS), f32)], axis=1)
    bias = jnp.concatenate([b_re.reshape(ne), b_rc, jnp.zeros((LANES - ne - N_GROUPS,), f32)]).reshape(1, LANES)
    whi = w.astype(bf16)
    wlo = (w - whi.astype(f32)).astype(bf16)
    return whi, wlo, bias


def _token_major(kv_t):
    b, _, n = kv_t.shape
    return kv_t.reshape(b, 2, KV_HEADS, HEAD_DIM, n).transpose(0, 4, 1, 2, 3)


def kernel(x_prompt, x_sample, cache_cmp_kv, cache_sel_kv, cache_win_kv, state_hgrn, page_table, attn_w_in, attn_cmp_alpha, attn_w_out, rec_w_in, rec_lb_logits, rec_norm_g, rec_w_out, ln_g, ln_b, moe_w_router_c, moe_b_router_c, moe_w_router_e, moe_b_router_e, moe_w_gate, moe_w_up, moe_w_down):
    b, seq, d = x_prompt.shape
    bs, lq, _ = x_sample.shape
    tp, ts = b * seq, bs * lq
    lb_all = jnp.cumsum(jax.nn.softmax(rec_lb_logits.astype(f32), axis=0), axis=0)
    hp, hs = x_prompt.reshape(tp, d), x_sample.reshape(ts, d)
    cache_cmp_t = cache_cmp_kv.transpose(0, 1, 3, 4, 5, 2)
    cache_sel_t = cache_sel_kv.transpose(0, 1, 3, 4, 5, 2)
    kv_out = [[] for _ in range(6)]
    rec_out = [[], []]
    for layer in range(DEPTH):
        j = layer // 2
        if layer % 2 == 0:
            nq, n_in = N_HEADS * HEAD_DIM, attn_w_in.shape[2]
            w_all = attn_w_in[j]
            w_in = jnp.pad(w_all, ((0, 0), (0, -(-n_in // LANES) * LANES - n_in))).astype(bf16)
            w_qg = jnp.concatenate([w_all[:, :nq], w_all[:, nq + 6 * KV_WIDTH:]], axis=1)
            w_qg = jnp.pad(w_qg, ((0, 0), (0, -(-w_qg.shape[1] // LANES) * LANES - w_qg.shape[1]))).astype(bf16)
            w_kv_t = w_all[:, nq:nq + 6 * KV_WIDTH].T.astype(bf16)
            alpha_rows = jnp.repeat(attn_cmp_alpha[j], HEAD_DIM, axis=-1).reshape(CMP_BLOCK, 2 * KV_WIDTH).T
            w_out = attn_w_out[j].astype(bf16)
            op, c_p, s_p, w_p = _nsa_prompt(hp.reshape(b, seq, d), w_qg, w_kv_t, alpha_rows)
            os_, c_s, s_s, w_s = _nsa_sample(hs.reshape(bs, lq, d), cache_cmp_t, cache_sel_t, cache_win_kv,
                                             page_table, j, w_in, alpha_rows)
            kvs = (bs, lq, 2, KV_HEADS, HEAD_DIM)
            n_win = min(WINDOW, seq)
            for lst, val in zip(kv_out, (_token_major(c_p), c_s.reshape(kvs), _token_major(s_p), s_s.reshape(kvs),
                                         _token_major(w_p[:, :, seq - n_win:]),
                                         w_s.reshape(bs, -1, 2, KV_HEADS, HEAD_DIM))):
                lst.append(val)
        else:
            lb = lb_all[layer] - lb_all[0]
            w_in = rec_w_in[j].astype(bf16)
            w_out = rec_w_out[j].astype(bf16)
            op, st_p = _hgrn_mixer(hp.reshape(b, seq, d), jnp.zeros((b, HG_HEADS, HG_DK, HG_DV), f32), w_in, lb,
                                   rec_norm_g[j], 256)
            os_, st_s = _hgrn_mixer(hs.reshape(bs, lq, d), state_hgrn[j], w_in, lb, rec_norm_g[j], 256)
            rec_out[0].append(st_p)
            rec_out[1].append(st_s)
        hp = _proj_ln(op, w_out, hp, ln_g[layer, 0], ln_b[layer, 0], 512, "out_proj_ln")
        hs = _proj_ln(os_, w_out, hs, ln_g[layer, 0], ln_b[layer, 0], ts, "out_proj_ln_s")
        whi, wlo, rbias = _router_weights(moe_w_router_c[layer], moe_b_router_c[layer], moe_w_router_e[layer],
                                          moe_b_router_e[layer])
        wg, wu, wd = moe_w_gate[layer].astype(bf16), moe_w_up[layer].astype(bf16), moe_w_down[layer].astype(bf16)
        hp = _moe_ln(hp, _router(hp, whi, wlo, rbias, 512), wg, wu, wd, ln_g[layer, 1], ln_b[layer, 1], 512)
        hs = _moe_ln(hs, _router(hs, whi, wlo, rbias, ts), wg, wu, wd, ln_g[layer, 1], ln_b[layer, 1], ts)
    return (hp.reshape(b, seq, d), hs.reshape(bs, lq, d), *[jnp.stack(v) for v in kv_out],
            jnp.stack(rec_out[0]), jnp.stack(rec_out[1]))
```

```python
import functools

import jax
import jax.numpy as jnp
from jax import lax
from jax.experimental import pallas as pl
from jax.experimental.pallas import tpu as pltpu

f32, bf16, i32 = jnp.float32, jnp.bfloat16, jnp.int32

DEPTH = 2
N_HEADS, KV_HEADS, HEAD_DIM = 16, 4, 64
GROUP = N_HEADS // KV_HEADS
KV_WIDTH = KV_HEADS * HEAD_DIM
CMP_BLOCK, TOP_K, WINDOW = 64, 16, 512
FORCED_SCORE = float(GROUP + 1)
PAGE_SIZE = 128
HG_HEADS, HG_DK, HG_DV, HG_CHUNK = 8, 128, 128, 32
N_GROUPS, EXPERTS_PER_GROUP = 4, 8
ALPHA = (2.0 * DEPTH) ** 0.25
LN_EPS, RMS_EPS = 1e-5, 1e-6

NEG = -1e30
LOG2E = 1.4426950408889634
CMP_SHIFT = CMP_BLOCK.bit_length() - 1
assert 1 << CMP_SHIFT == CMP_BLOCK
LANES = 128
HG_HB = 8
VMEM_LIMIT = 56 * 1024 * 1024


def _cp(*sem):
    return pltpu.CompilerParams(dimension_semantics=sem, vmem_limit_bytes=VMEM_LIMIT)


def _sigmoid(x):
    return 1.0 / (1.0 + jnp.exp(-x))


def _log_sigmoid(x):
    return jnp.minimum(x, 0.0) - jnp.log1p(jnp.exp(-jnp.abs(x)))


def _layer_norm(z, g, b):
    mu = jnp.mean(z, axis=-1, keepdims=True)
    d = z - mu
    var = jnp.mean(d * d, axis=-1, keepdims=True)
    return d * lax.rsqrt(var + LN_EPS) * g + b


def _mm_kernel(x_ref, w_ref, o_ref):
    o_ref[...] = jnp.dot(x_ref[...].astype(bf16), w_ref[...], preferred_element_type=f32).astype(o_ref.dtype)


def _matmul(x, w, tm, name):
    m, k = x.shape
    n = w.shape[1]
    return pl.pallas_call(
        _mm_kernel, grid=(m // tm,),
        in_specs=[pl.BlockSpec((tm, k), lambda i: (i, 0)), pl.BlockSpec((k, n), lambda i: (0, 0))],
        out_specs=pl.BlockSpec((tm, n), lambda i: (i, 0)),
        out_shape=jax.ShapeDtypeStruct((m, n), f32), compiler_params=_cp("arbitrary"), name=name)(x, w)


def _proj_ln_kernel(o_ref, w_ref, x_ref, g_ref, b_ref, h_ref):
    y = jnp.dot(o_ref[...].astype(bf16), w_ref[...], preferred_element_type=f32)
    h_ref[...] = _layer_norm(ALPHA * x_ref[...] + y, g_ref[...], b_ref[...])


def _proj_ln(o, w, x, g, b, tm, name):
    m, k = o.shape
    n = w.shape[1]
    row = lambda i: (i, 0)
    fix = lambda i: (0, 0)
    return pl.pallas_call(
        _proj_ln_kernel, grid=(m // tm,),
        in_specs=[pl.BlockSpec((tm, k), row), pl.BlockSpec((k, n), fix), pl.BlockSpec((tm, n), row),
                  pl.BlockSpec((1, n), fix), pl.BlockSpec((1, n), fix)],
        out_specs=pl.BlockSpec((tm, n), row),
        out_shape=jax.ShapeDtypeStruct((m, n), f32), compiler_params=_cp("arbitrary"), name=name)(
            o, w, x, g.reshape(1, n), b.reshape(1, n))


def _split3(x):
    hi = x.astype(bf16)
    r1 = x - hi.astype(f32)
    mid = r1.astype(bf16)
    return hi, mid, (r1 - mid.astype(f32)).astype(bf16)


def _dot_nt(a, b):
    return lax.dot_general(a, b, (((1,), (1,)), ((), ())), preferred_element_type=f32)


def _block_sum_lanes(y, onehot):
    hi = y.astype(bf16)
    return _dot_nt(hi, onehot) + _dot_nt((y - hi.astype(f32)).astype(bf16), onehot)


def _compress_kernel(x_ref, a_ref, e_ref, o_ref):
    o_ref[0, 0] = _block_sum_lanes(x_ref[0, 0] * a_ref[0], e_ref[...])


def _compress_t(kvt, alpha_l, onehot):
    b, ch, hd, seq = kvt.shape
    nbp = onehot.shape[0]
    return pl.pallas_call(
        _compress_kernel, grid=(b, ch),
        in_specs=[pl.BlockSpec((1, 1, hd, seq), lambda i, j: (i, j, 0, 0)),
                  pl.BlockSpec((1, 1, seq), lambda i, j: (j, 0, 0)),
                  pl.BlockSpec((nbp, seq), lambda i, j: (0, 0))],
        out_specs=pl.BlockSpec((1, 1, hd, nbp), lambda i, j: (i, j, 0, 0)),
        out_shape=jax.ShapeDtypeStruct((b, ch, hd, nbp), f32),
        compiler_params=_cp("arbitrary", "arbitrary"), name="nsa_compress")(kvt, alpha_l, onehot)


def _compress_paged_kernel(pt_ref, *refs, n_pg):
    del pt_ref
    pages, a_ref, e_ref, o_ref = refs[:n_pg], refs[n_pg], refs[n_pg + 1], refs[n_pg + 2]
    w = 2 * KV_WIDTH
    x = jnp.concatenate([r[0, 0].reshape(w, PAGE_SIZE) for r in pages], axis=1)
    o_ref[0, 0] = _block_sum_lanes(x * a_ref[...], e_ref[...])


def _compress_paged(cache_t, layer, page_table, alpha_rows, n_pg):
    nb_, n_pages = page_table.shape
    bpp = PAGE_SIZE // CMP_BLOCK
    n_tok = n_pg * PAGE_SIZE
    a_t = jnp.tile(alpha_rows, (1, n_pg * bpp))
    onehot = (jnp.arange(n_pg * bpp)[:, None] == jnp.arange(n_tok)[None, :] // CMP_BLOCK).astype(bf16)

    def page_map(b, i, pt, k):
        return (layer, pt[b, i * n_pg + k], 0, 0, 0, 0)

    fix = lambda b, i, pt: (0, 0)
    grid_spec = pltpu.PrefetchScalarGridSpec(
        num_scalar_prefetch=1, grid=(nb_, n_pages // n_pg),
        in_specs=[pl.BlockSpec((1, 1, 2, KV_HEADS, HEAD_DIM, PAGE_SIZE), functools.partial(page_map, k=k))
                  for k in range(n_pg)]
        + [pl.BlockSpec(a_t.shape, fix), pl.BlockSpec(onehot.shape, fix)],
        out_specs=pl.BlockSpec((1, 1, 2 * KV_WIDTH, n_pg * bpp), lambda b, i, pt: (b, i, 0, 0)))
    return pl.pallas_call(
        functools.partial(_compress_paged_kernel, n_pg=n_pg), grid_spec=grid_spec,
        out_shape=jax.ShapeDtypeStruct((nb_, n_pages // n_pg, 2 * KV_WIDTH, n_pg * bpp), f32),
        compiler_params=_cp("arbitrary", "arbitrary"), name="nsa_compress_paged")(
            page_table, *([cache_t] * n_pg), a_t, onehot)


def _select_bias(score, k_sel):
    rowf = lax.broadcasted_iota(i32, score.shape, 0).astype(f32)
    picked = jnp.zeros(score.shape, f32)
    work = score
    for _ in range(k_sel):
        mx = jnp.max(work, axis=0, keepdims=True)
        first = jnp.min(jnp.where(work == mx, rowf, 1e9), axis=0, keepdims=True)
        pick = rowf == first
        picked = jnp.where(pick, 1.0, picked)
        work = jnp.where(pick, -jnp.inf, work)
    return jnp.where(picked > 0.0, 0.0, NEG)


def _masked_softmax_cols(s, mask, exp_fn):
    sm = jnp.where(mask, s, NEG)
    m = jnp.max(sm, axis=0, keepdims=True)
    e = jnp.where(mask, exp_fn(sm - m), 0.0)
    return e / jnp.maximum(jnp.sum(e, axis=0, keepdims=True), 1e-30)


def _nsa_in_proj_kernel(x_ref, wq_ref, wkv_ref, pq_ref, c_ref, s_ref, w_ref):
    xb = x_ref[0].astype(bf16)
    pq_ref[0] = jnp.dot(xb, wq_ref[...], preferred_element_type=f32)
    kvt = _dot_nt(wkv_ref[...], xb)
    w = 2 * KV_WIDTH
    for n, ref in enumerate((c_ref, s_ref, w_ref)):
        ref[0] = kvt[n * w:(n + 1) * w]


def _nsa_in_proj(x, wq, wkv_t, tm):
    b, seq, d = x.shape
    nqg, w = wq.shape[1], 2 * KV_WIDTH
    kv_spec = pl.BlockSpec((1, w, tm), lambda i, j: (i, 0, j))
    kv_shape = jax.ShapeDtypeStruct((b, w, seq), f32)
    return pl.pallas_call(
        _nsa_in_proj_kernel, grid=(b, seq // tm),
        in_specs=[pl.BlockSpec((1, tm, d), lambda i, j: (i, j, 0)), pl.BlockSpec((d, nqg), lambda i, j: (0, 0)),
                  pl.BlockSpec((3 * w, d), lambda i, j: (0, 0))],
        out_specs=[pl.BlockSpec((1, tm, nqg), lambda i, j: (i, j, 0)), kv_spec, kv_spec, kv_spec],
        out_shape=[jax.ShapeDtypeStruct((b, seq, nqg), f32), kv_shape, kv_shape, kv_shape],
        compiler_params=_cp("arbitrary", "arbitrary"), name="nsa_in_proj")(x, wq, wkv_t)


def _nsa_prompt_kernel(qt_ref, g_ref, kct_ref, vct_ref, kst_ref, vst_ref, kwt_ref, vwt_ref, et_ref, o_ref,
                       m_ref, l_ref, acc_ref, sa_ref, sb_ref, *, tq, tk, nb):
    s0 = pl.program_id(1) * tq
    c = GROUP * tq
    nbp = et_ref.shape[1]
    qt = qt_ref[0, 0, 0]
    tq_pos = s0 + lax.broadcasted_iota(i32, (1, tq), 1)
    tpos = jnp.concatenate([tq_pos] * GROUP, axis=1)

    kc = kct_ref[0, 0, 0].T.astype(bf16)
    sc = jnp.dot(kc, qt, preferred_element_type=f32)
    blk_c = lax.broadcasted_iota(i32, (nbp, c), 0)
    pc = _masked_softmax_cols(sc, blk_c * CMP_BLOCK + (CMP_BLOCK - 1) <= tpos, jnp.exp2)
    oc = jnp.dot(vct_ref[0, 0, 0].astype(bf16), pc.astype(bf16), preferred_element_type=f32)

    imp = pc[:, 0:tq]
    for g in range(1, GROUP):
        imp = imp + pc[:, g * tq:(g + 1) * tq]
    blk = lax.broadcasted_iota(i32, (nbp, tq), 0)
    cur = jnp.right_shift(tq_pos, CMP_SHIFT)
    forced = (blk == 0) | (blk == cur) | (blk == cur - 1)
    base = jnp.where(forced, FORCED_SCORE, jnp.where(blk * CMP_BLOCK <= tq_pos, 0.0, -1.0))
    sel = _select_bias(jnp.where(blk < nb, base + imp, -3.0), min(TOP_K, nb)).astype(bf16)
    qa = jnp.concatenate([jnp.concatenate([sel] * GROUP, axis=1), qt], axis=0)

    m_ref[...] = jnp.full(m_ref.shape, NEG, f32)
    l_ref[...] = jnp.zeros(l_ref.shape, f32)
    acc_ref[...] = jnp.zeros(acc_ref.shape, f32)

    def produce(j, s_ref):
        k0 = pl.multiple_of(j * tk, tk)
        kk = kst_ref[0, 0, 0, :, pl.ds(k0, tk)].T.astype(bf16)
        s_ref[...] = jnp.dot(jnp.concatenate([et_ref[pl.ds(k0, tk), :], kk], axis=1), qa, preferred_element_type=f32)

    def consume(j, s_ref, causal):
        k0 = pl.multiple_of(j * tk, tk)
        s = s_ref[...]
        if causal:
            s = jnp.where(k0 + lax.broadcasted_iota(i32, (tk, c), 0) <= tpos, s, NEG)
        m_new = jnp.maximum(m_ref[...], jnp.max(s, axis=0, keepdims=True))
        a = jnp.exp2(m_ref[...] - m_new)
        p = jnp.exp2(s - m_new)
        l_ref[...] = a * l_ref[...] + jnp.sum(p, axis=0, keepdims=True)
        acc_ref[...] = a * acc_ref[...] + jnp.dot(vst_ref[0, 0, 0, :, pl.ds(k0, tk)].astype(bf16), p.astype(bf16),
                                                  preferred_element_type=f32)
        m_ref[...] = m_new

    n_full = s0 // tk
    produce(0, sa_ref)

    def pair(i, carry):
        produce(2 * i + 1, sb_ref)
        consume(2 * i, sa_ref, False)
        produce(2 * i + 2, sa_ref)
        consume(2 * i + 1, sb_ref, False)
        return carry

    lax.fori_loop(0, n_full // 2, pair, 0)

    @pl.when(n_full % 2 == 1)
    def _():
        produce(n_full, sb_ref)
        consume(n_full - 1, sa_ref, False)
        consume(n_full, sb_ref, True)

    @pl.when(n_full % 2 == 0)
    def _():
        consume(n_full, sa_ref, True)

    osel = acc_ref[...] / jnp.maximum(l_ref[...], 1e-30)

    wt = WINDOW + tq
    k0w = pl.multiple_of(jnp.maximum(s0 - WINDOW, 0), LANES)
    kw = kwt_ref[0, 0, 0, :, pl.ds(k0w, wt)].T.astype(bf16)
    s = jnp.dot(kw, qt, preferred_element_type=f32)
    dist = tpos - (k0w + lax.broadcasted_iota(i32, (wt, c), 0))
    s = jnp.where(dist >= 0, jnp.where(dist < WINDOW, s, NEG), NEG)
    p = jnp.exp2(s - jnp.max(s, axis=0, keepdims=True))
    lw = jnp.sum(p, axis=0, keepdims=True)
    ow = jnp.dot(vwt_ref[0, 0, 0, :, pl.ds(k0w, wt)].astype(bf16), p.astype(bf16), preferred_element_type=f32)
    ow = ow / jnp.maximum(lw, 1e-30)

    gt = _sigmoid(g_ref[0, 0, 0])
    o_ref[0, 0, 0] = (gt[0:1] * oc + gt[1:2] * osel + gt[2:3] * ow).astype(o_ref.dtype)


def _nsa_prompt_attention(qt, gt, cblk_t, kvs_t, kvw_t, onehot_t, tq, tk):
    b, kvh, n_t, hd, c = qt.shape
    seq, nbp = onehot_t.shape
    assert tk % tq == 0
    tile = lambda bh, i: (bh // kvh, bh % kvh, i, 0, 0)
    comp = lambda cc: (lambda bh, i: (bh // kvh, cc, bh % kvh, 0, 0))
    rows = lambda n: (pl.BlockSpec((1, 1, 1, hd, n), comp(0)), pl.BlockSpec((1, 1, 1, hd, n), comp(1)))
    return pl.pallas_call(
        functools.partial(_nsa_prompt_kernel, tq=tq, tk=tk, nb=seq // CMP_BLOCK), grid=(b * kvh, n_t),
        in_specs=[pl.BlockSpec((1, 1, 1, hd, c), tile), pl.BlockSpec((1, 1, 1, 3, c), tile),
                  *rows(nbp), *rows(seq), *rows(seq), pl.BlockSpec((seq, nbp), lambda bh, i: (0, 0))],
        out_specs=pl.BlockSpec((1, 1, 1, hd, c), tile),
        out_shape=jax.ShapeDtypeStruct((b, kvh, n_t, hd, c), bf16),
        scratch_shapes=[pltpu.VMEM((1, c), f32), pltpu.VMEM((1, c), f32), pltpu.VMEM((hd, c), f32),
                        pltpu.VMEM((tk, c), f32), pltpu.VMEM((tk, c), f32)],
        compiler_params=_cp("arbitrary", "arbitrary"), name="nsa_prompt_attn")(
            qt, gt, cblk_t, cblk_t, kvs_t, kvs_t, kvw_t, kvw_t, onehot_t)


def _nsa_prompt(x, wq, wkv_t, alpha_rows, tq=128, tk=512):
    b, seq, d = x.shape
    n_t, nb = seq // tq, seq // CMP_BLOCK
    nq = N_HEADS * HEAD_DIM
    pq, kvc_t, kvs_t, kvw_t = _nsa_in_proj(x, wq, wkv_t, 512)
    five = lambda a: a.reshape(b, 2, KV_HEADS, HEAD_DIM, -1)
    nbp = -(-nb // LANES) * LANES
    onehot = (jnp.arange(nbp)[:, None] == jnp.arange(seq)[None, :] // CMP_BLOCK).astype(bf16)
    alpha_l = jnp.tile(alpha_rows[::HEAD_DIM], (1, nb)).reshape(2 * KV_HEADS, 1, seq)
    cblk_t = _compress_t(kvc_t.reshape(b, 2 * KV_HEADS, HEAD_DIM, seq), alpha_l, onehot)
    qs = (pq[..., :nq] * (HEAD_DIM ** -0.5 * LOG2E)).astype(bf16).reshape(b, n_t, tq, KV_HEADS, GROUP, HEAD_DIM)
    qt = qs.transpose(0, 3, 1, 5, 4, 2).reshape(b, KV_HEADS, n_t, HEAD_DIM, GROUP * tq)
    gt = pq[..., nq:nq + 3 * N_HEADS].reshape(b, n_t, tq, KV_HEADS, GROUP, 3)
    gt = gt.transpose(0, 3, 1, 5, 4, 2).reshape(b, KV_HEADS, n_t, 3, GROUP * tq)
    ot = _nsa_prompt_attention(qt, gt, five(cblk_t), five(kvs_t), five(kvw_t), onehot.T, tq, tk)
    o = ot.reshape(b, KV_HEADS, n_t, HEAD_DIM, GROUP, tq).transpose(0, 2, 5, 1, 4, 3).reshape(b * seq, d)
    return o, kvc_t, kvs_t, kvw_t


def _dot_t0(a, b):
    return lax.dot_general(a, b, (((0,), (0,)), ((), ())), preferred_element_type=f32)


def _smp_cmp_win_kernel(q_ref, kc_ref, vc_ref, cn_ref, a_ref, kw_ref, vw_ref, oc_ref, ow_ref, sel_ref, *, past, lq,
                        nb_real):
    q = q_ref[0]
    nbp = kc_ref.shape[1]
    col = lax.broadcasted_iota(i32, (1, LANES), 1)
    tpos = past + jnp.bitwise_and(col, lq - 1)

    c_new = jnp.sum(cn_ref[0] * a_ref[...], axis=0, keepdims=True).astype(bf16)
    is_new = lax.broadcasted_iota(i32, (nbp, KV_WIDTH), 0) == nb_real - 1
    kc = jnp.where(is_new, c_new[:, :KV_WIDTH], kc_ref[0])
    vc = jnp.where(is_new, c_new[:, KV_WIDTH:], vc_ref[0])
    sc = jnp.dot(kc, q, preferred_element_type=f32)
    blk = lax.broadcasted_iota(i32, (nbp, LANES), 0)
    pc = _masked_softmax_cols(sc, blk * CMP_BLOCK + (CMP_BLOCK - 1) <= tpos, jnp.exp)
    oc_ref[0] = _dot_t0(pc.astype(bf16), vc)

    r = lax.broadcasted_iota(i32, (LANES, LANES), 0)
    cc = lax.broadcasted_iota(i32, (LANES, LANES), 1)
    per_kv = GROUP * lq
    same = jnp.where(r // per_kv == cc // per_kv, jnp.where(r % lq == cc % lq, 1.0, 0.0), 0.0).astype(bf16)
    hi, mid, lo = _split3(pc)
    imp = (jnp.dot(hi, same, preferred_element_type=f32) + jnp.dot(mid, same, preferred_element_type=f32)
           + jnp.dot(lo, same, preferred_element_type=f32))
    cur = jnp.right_shift(tpos, CMP_SHIFT)
    forced = (blk == 0) | (blk == cur) | (blk == cur - 1)
    base = jnp.where(forced, FORCED_SCORE, jnp.where(blk * CMP_BLOCK <= tpos, 0.0, -1.0))
    score = jnp.where(blk < nb_real, base + imp, -3.0)
    sel_ref[0] = _select_bias(score, min(TOP_K, nb_real))

    nwt = kw_ref.shape[1]
    nw = nwt - lq
    s = jnp.dot(kw_ref[0], q, preferred_element_type=f32)
    wpos = past - nw + lax.broadcasted_iota(i32, (nwt, LANES), 0)
    dist = tpos - wpos
    ok = (dist >= 0) & (dist < WINDOW) & (wpos >= 0)
    pw = _masked_softmax_cols(s, ok, jnp.exp)
    ow_ref[0] = _dot_t0(pw.astype(bf16), vw_ref[0])


def _smp_sel_kernel(pt_ref, *refs, n_pg):
    del pt_ref
    pages = refs[:n_pg]
    q_ref, sel_ref, e_ref, kn_ref, vn_ref, bn_ref, g_ref, oc_ref, ow_ref, o_ref, m_ref, l_ref, acc_ref = refs[n_pg:]
    i = pl.program_id(1)

    @pl.when(i == 0)
    def _():
        m_ref[...] = jnp.full(m_ref.shape, NEG, f32)
        l_ref[...] = jnp.zeros(l_ref.shape, f32)
        acc_ref[...] = jnp.zeros(acc_ref.shape, f32)

    def update(s, vt):
        m_new = jnp.maximum(m_ref[...], jnp.max(s, axis=1, keepdims=True))
        a = jnp.exp(m_ref[...] - m_new)
        p = jnp.exp(s - m_new)
        l_ref[...] = a * l_ref[...] + jnp.sum(p, axis=1, keepdims=True)
        acc_ref[...] = a * acc_ref[...] + _dot_nt(p.astype(bf16), vt)
        m_ref[...] = m_new

    qa = jnp.concatenate([q_ref[0], sel_ref[0, 0].astype(bf16)], axis=1)
    kt = jnp.concatenate([r[0, 0, 0].reshape(KV_WIDTH, PAGE_SIZE) for r in pages], axis=1).astype(bf16)
    vt = jnp.concatenate([r[0, 0, 1].reshape(KV_WIDTH, PAGE_SIZE) for r in pages], axis=1).astype(bf16)
    update(jnp.dot(qa, jnp.concatenate([kt, e_ref[...]], axis=0), preferred_element_type=f32), vt)

    @pl.when(i == pl.num_programs(1) - 1)
    def _():
        update(jnp.dot(q_ref[0], kn_ref[0].astype(bf16), preferred_element_type=f32) + bn_ref[0], vn_ref[0].astype(bf16))
        gt = _sigmoid(g_ref[0])
        osel = acc_ref[...] / jnp.maximum(l_ref[...], 1e-30)
        o_ref[0] = gt[:, 0:1] * oc_ref[0] + gt[:, 1:2] * osel + gt[:, 2:3] * ow_ref[0]


def _nsa_sample(x, cache_cmp_t, cache_sel_t, cache_win, page_table, layer, w_in, alpha_rows, n_pg=16):
    b, lq, d = x.shape
    n_pages = page_table.shape[1]
    past = n_pages * PAGE_SIZE
    assert KV_HEADS * GROUP * lq == LANES and past % CMP_BLOCK == 0 and lq <= CMP_BLOCK
    assert n_pages % n_pg == 0 and n_pg * PAGE_SIZE // CMP_BLOCK <= LANES
    p = _matmul(x.reshape(b * lq, d), w_in, b * lq, "nsa_in_proj_s")
    nq = N_HEADS * HEAD_DIM
    w2 = 2 * KV_WIDTH
    q = (p[:, :nq] * HEAD_DIM ** -0.5).reshape(b, lq, KV_HEADS, GROUP, HEAD_DIM)
    eye = jnp.eye(KV_HEADS, dtype=f32)
    qbd = (q.transpose(0, 2, 3, 1, 4)[:, :, :, :, None, :] * eye[None, :, None, None, :, None])
    qbd = qbd.reshape(b, LANES, KV_WIDTH).astype(bf16)
    qbd_t = qbd.transpose(0, 2, 1)
    gt = p[:, nq + 6 * KV_WIDTH:nq + 6 * KV_WIDTH + 3 * N_HEADS].reshape(b, lq, KV_HEADS, GROUP, 3)
    gt = gt.transpose(0, 2, 3, 1, 4).reshape(b, LANES, 3)
    kv = p[:, nq:nq + 6 * KV_WIDTH].reshape(b, lq, 3, w2)
    kv_c, kv_s, kv_w = kv[:, :, 0], kv[:, :, 1], kv[:, :, 2]

    bpp = PAGE_SIZE // CMP_BLOCK
    c_past = _compress_paged(cache_cmp_t, layer, page_table, alpha_rows, n_pg)
    c_past = c_past.transpose(0, 1, 3, 2).reshape(b, past // CMP_BLOCK, w2)
    nb_real = past // CMP_BLOCK + 1
    nbp = -(-nb_real // 8) * 8
    cblk = jnp.pad(c_past, ((0, 0), (0, nbp - nb_real + 1), (0, 0)))
    wkv = jnp.concatenate([cache_win[layer].reshape(b, -1, w2), kv_w], axis=1)
    nwt = wkv.shape[1]

    seq3 = lambda i: (i, 0, 0)
    oc, ow, sel = pl.pallas_call(
        functools.partial(_smp_cmp_win_kernel, past=past, lq=lq, nb_real=nb_real), grid=(b,),
        in_specs=[pl.BlockSpec((1, KV_WIDTH, LANES), seq3), pl.BlockSpec((1, nbp, KV_WIDTH), seq3),
                  pl.BlockSpec((1, nbp, KV_WIDTH), seq3), pl.BlockSpec((1, lq, w2), seq3),
                  pl.BlockSpec((lq, w2), lambda i: (0, 0)), pl.BlockSpec((1, nwt, KV_WIDTH), seq3),
                  pl.BlockSpec((1, nwt, KV_WIDTH), seq3)],
        out_specs=[pl.BlockSpec((1, LANES, KV_WIDTH), seq3), pl.BlockSpec((1, LANES, KV_WIDTH), seq3),
                   pl.BlockSpec((1, nbp, LANES), seq3)],
        out_shape=[jax.ShapeDtypeStruct((b, LANES, KV_WIDTH), f32), jax.ShapeDtypeStruct((b, LANES, KV_WIDTH), f32),
                   jax.ShapeDtypeStruct((b, nbp, LANES), f32)],
        compiler_params=_cp("arbitrary"), name="nsa_sample_cmp_win")(
            qbd_t, cblk[:, :, :KV_WIDTH].astype(bf16), cblk[:, :, KV_WIDTH:].astype(bf16), kv_c, alpha_rows.T[:lq],
            wkv[:, :, :KV_WIDTH].astype(bf16), wkv[:, :, KV_WIDTH:].astype(bf16))

    n_steps, bps = n_pages // n_pg, n_pg * bpp
    sel_r = sel.transpose(0, 2, 1)
    sel_steps = sel_r[:, :, :past // CMP_BLOCK].reshape(b, LANES, n_steps, bps).transpose(0, 2, 1, 3)
    sel_steps = jnp.pad(sel_steps, ((0, 0), (0, 0), (0, 0), (0, LANES - bps)))
    onehot = (jnp.arange(LANES)[:, None] == jnp.arange(n_pg * PAGE_SIZE)[None, :] // CMP_BLOCK).astype(bf16)
    pad_t = lambda a: jnp.pad(a.transpose(0, 2, 1), ((0, 0), (0, 0), (0, LANES - lq)))
    qoff = jnp.arange(LANES) % lq
    readable = jnp.arange(LANES)[None, :] <= qoff[:, None]
    bias_new = jnp.where(readable[None], sel_r[:, :, past // CMP_BLOCK][:, :, None], NEG)

    def page_map(bb, i, pt, k):
        return (layer, pt[bb, i * n_pg + k], 0, 0, 0, 0)

    seq3p = lambda bb, i, pt: (bb, 0, 0)
    grid_spec = pltpu.PrefetchScalarGridSpec(
        num_scalar_prefetch=1, grid=(b, n_steps),
        in_specs=[pl.BlockSpec((1, 1, 2, KV_HEADS, HEAD_DIM, PAGE_SIZE), functools.partial(page_map, k=k))
                  for k in range(n_pg)]
        + [pl.BlockSpec((1, LANES, KV_WIDTH), seq3p), pl.BlockSpec((1, 1, LANES, LANES), lambda bb, i, pt: (bb, i, 0, 0)),
           pl.BlockSpec(onehot.shape, lambda bb, i, pt: (0, 0)),
           pl.BlockSpec((1, KV_WIDTH, LANES), seq3p), pl.BlockSpec((1, KV_WIDTH, LANES), seq3p),
           pl.BlockSpec((1, LANES, LANES), seq3p), pl.BlockSpec((1, LANES, 3), seq3p),
           pl.BlockSpec((1, LANES, KV_WIDTH), seq3p), pl.BlockSpec((1, LANES, KV_WIDTH), seq3p)],
        out_specs=pl.BlockSpec((1, LANES, KV_WIDTH), seq3p),
        scratch_shapes=[pltpu.VMEM((LANES, 1), f32), pltpu.VMEM((LANES, 1), f32), pltpu.VMEM((LANES, KV_WIDTH), f32)])
    o = pl.pallas_call(
        functools.partial(_smp_sel_kernel, n_pg=n_pg), grid_spec=grid_spec,
        out_shape=jax.ShapeDtypeStruct((b, LANES, KV_WIDTH), f32),
        compiler_params=_cp("arbitrary", "arbitrary"), name="nsa_sample_sel")(
            page_table, *([cache_sel_t] * n_pg), qbd, sel_steps, onehot, pad_t(kv_s[:, :, :KV_WIDTH]),
            pad_t(kv_s[:, :, KV_WIDTH:]), bias_new, gt, oc, ow)
    o = o.reshape(b, KV_HEADS, GROUP, lq, KV_HEADS, HEAD_DIM)
    o = jnp.stack([o[:, h, :, :, h] for h in range(KV_HEADS)], axis=1)
    o = o.transpose(0, 3, 1, 2, 4).reshape(b * lq, d)
    return o, kv_c, kv_s, wkv[:, lq:]


def _hgrn_kernel(zq_ref, zf_ref, zi_ref, zg_ref, lb_ref, ng_ref, s0_ref, o_ref, sout_ref, st_ref, *, n_valid, hb):
    t = pl.program_id(2)
    tb = zq_ref.shape[1]
    n_ch = tb // HG_CHUNK

    @pl.when(t == 0)
    def _():
        for hh in range(hb):
            st_ref[hh] = s0_ref[0, hh].T

    r = lax.broadcasted_iota(i32, (tb, tb), 0)
    cc = lax.broadcasted_iota(i32, (tb, tb), 1)
    tri = jnp.where(r // HG_CHUNK == cc // HG_CHUNK, jnp.where(cc <= r, 1.0, 0.0), 0.0).astype(bf16)
    causal = lax.broadcasted_iota(i32, (HG_CHUNK, HG_CHUNK), 1) <= lax.broadcasted_iota(i32, (HG_CHUNK, HG_CHUNK), 0)
    mid_row = HG_CHUNK // 2

    zf = zf_ref[0]
    lb = lb_ref[0]
    la = jnp.log(lb)
    l1 = jnp.log1p(-lb)
    bb = l1 + _log_sigmoid(zf)
    log_f = jnp.maximum(la, bb) + jnp.log1p(jnp.exp(-jnp.abs(la - bb)))
    kk = jnp.exp(l1 + _log_sigmoid(-zf))
    if n_valid < tb:
        live = lax.broadcasted_iota(i32, zf.shape, 0) < n_valid
        log_f = jnp.where(live, log_f, 0.0)
        kk = jnp.where(live, kk, 0.0)
    hi, mid, lo = _split3(log_f)
    cum = (jnp.dot(tri, hi, preferred_element_type=f32) + jnp.dot(tri, mid, preferred_element_type=f32)
           + jnp.dot(tri, lo, preferred_element_type=f32))
    zq, zi, zg = zq_ref[0], zi_ref[0], zg_ref[0]
    gate = ng_ref[0] * (zg * _sigmoid(zg))

    qs, o_in, upd, dec = {}, {}, {}, {}
    for hh in range(hb):
        hs = slice(hh * HG_DK, (hh + 1) * HG_DK)
        for ci in range(n_ch):
            sl = slice(ci * HG_CHUNK, (ci + 1) * HG_CHUNK)
            cm, qi, ki, vi = cum[sl, hs], zq[sl, hs], kk[sl, hs], zi[sl, hs]
            ref = cm[mid_row:mid_row + 1]
            last = cm[HG_CHUNK - 1:HG_CHUNK]
            a = _dot_nt((qi * jnp.exp(cm - ref)).astype(bf16), (ki * jnp.exp(ref - cm)).astype(bf16))
            a = jnp.where(causal, a, 0.0)
            o_in[hh, ci] = jnp.dot(a.astype(bf16), vi.astype(bf16), preferred_element_type=f32)
            qs[hh, ci] = (qi * jnp.exp(cm)).astype(bf16)
            upd[hh, ci] = _dot_t0(vi.astype(bf16), (ki * jnp.exp(last - cm)).astype(bf16))
            dec[hh, ci] = jnp.exp(last)
    for hh in range(hb):
        hs = slice(hh * HG_DK, (hh + 1) * HG_DK)
        st = st_ref[hh]
        outs = []
        for ci in range(n_ch):
            outs.append(o_in[hh, ci] + _dot_nt(qs[hh, ci], st.astype(bf16)))
            st = st * dec[hh, ci] + upd[hh, ci]
        st_ref[hh] = st
        o = jnp.concatenate(outs, axis=0) if n_ch > 1 else outs[0]
        o = o * lax.rsqrt(jnp.mean(o * o, axis=-1, keepdims=True) + RMS_EPS)
        o_ref[0, :, hs] = (o * gate[:, hs]).astype(o_ref.dtype)

        @pl.when(t == pl.num_programs(2) - 1)
        def _():
            sout_ref[0, hh] = st.T


def _hgrn(p, s0, lb, norm_g, tb, n_valid, hb):
    b, lp, _ = p.shape
    ng = HG_HEADS // hb
    w = hb * HG_DK
    col = lambda off: (lambda bb, hh, t: (bb, t, off * ng + hh))
    per_head = lambda bb, hh, t: (0, hh)
    state = lambda bb, hh, t: (bb, hh, 0, 0)
    return pl.pallas_call(
        functools.partial(_hgrn_kernel, n_valid=n_valid, hb=hb), grid=(b, ng, lp // tb),
        in_specs=[pl.BlockSpec((1, tb, w), col(0)), pl.BlockSpec((1, tb, w), col(1)),
                  pl.BlockSpec((1, tb, w), col(2)), pl.BlockSpec((1, tb, w), col(3)),
                  pl.BlockSpec((1, w), per_head), pl.BlockSpec((1, w), per_head),
                  pl.BlockSpec((1, hb, HG_DK, HG_DV), state)],
        out_specs=[pl.BlockSpec((1, tb, w), lambda bb, hh, t: (bb, t, hh)),
                   pl.BlockSpec((1, hb, HG_DK, HG_DV), state)],
        out_shape=[jax.ShapeDtypeStruct((b, lp, HG_HEADS * HG_DV), bf16),
                   jax.ShapeDtypeStruct((b, HG_HEADS, HG_DK, HG_DV), f32)],
        scratch_shapes=[pltpu.VMEM((hb, HG_DV, HG_DK), f32)],
        compiler_params=_cp("arbitrary", "arbitrary", "arbitrary"), name="hgrn_scan")(
            p, p, p, p, lb.reshape(1, -1), norm_g.reshape(1, -1), s0)


def _hgrn_mixer(x, s0, w_in, lb, norm_g, tb):
    b, seq, d = x.shape
    lp = -(-seq // HG_CHUNK) * HG_CHUNK
    tb = min(tb, lp)
    p = _matmul(x.reshape(b * seq, d), w_in, min(256, b * seq), "hgrn_in_proj").reshape(b, seq, 4 * d)
    if lp != seq:
        p = jnp.pad(p, ((0, 0), (0, lp - seq), (0, 0)))
    o, s_out = _hgrn(p, s0, lb, norm_g, tb, seq if lp != seq else lp, HG_HB)
    return o[:, :seq].reshape(b * seq, d), s_out


def _router_logits(x, whi_ref, wlo_ref, b_ref):
    xh = x.astype(bf16)
    xl = (x - xh.astype(f32)).astype(bf16)
    return (jnp.dot(xh, whi_ref[...], preferred_element_type=f32) + jnp.dot(xl, whi_ref[...], preferred_element_type=f32)
            + jnp.dot(xh, wlo_ref[...], preferred_element_type=f32)) + b_ref[...]


def _route(lg):
    ne = N_GROUPS * EXPERTS_PER_GROUP
    lane = lax.broadcasted_iota(i32, lg.shape, 1)
    lanef = lane.astype(f32)
    in_c = (lane >= ne) & (lane < ne + N_GROUPS)
    lc = jnp.where(in_c, lg, -jnp.inf)
    mc = jnp.max(lc, axis=-1, keepdims=True)
    grp = jnp.min(jnp.where(lc == mc, lanef, 1e9), axis=-1, keepdims=True) - float(ne)
    pg = 1.0 / jnp.sum(jnp.where(in_c, jnp.exp(lc - mc), 0.0), axis=-1, keepdims=True)
    e_lo = grp * float(EXPERTS_PER_GROUP)
    in_e = (lanef >= e_lo) & (lanef < e_lo + float(EXPERTS_PER_GROUP))
    le = jnp.where(in_e, lg, -jnp.inf)
    v1 = jnp.max(le, axis=-1, keepdims=True)
    i1 = jnp.min(jnp.where(le == v1, lanef, 1e9), axis=-1, keepdims=True)
    le2 = jnp.where(lanef == i1, -jnp.inf, le)
    v2 = jnp.max(le2, axis=-1, keepdims=True)
    i2 = jnp.min(jnp.where(le2 == v2, lanef, 1e9), axis=-1, keepdims=True)
    e2 = jnp.exp(v2 - v1)
    w1 = pg / (1.0 + e2)
    return jnp.where(lanef == i1, w1, 0.0) + jnp.where(lanef == i2, w1 * e2, 0.0), grp


def _router_kernel(x_ref, whi_ref, wlo_ref, b_ref, cw_ref):
    cw, _ = _route(_router_logits(x_ref[...], whi_ref, wlo_ref, b_ref))
    for g in range(N_GROUPS):
        cw_ref[g] = cw if g == 0 else pltpu.roll(cw, LANES - g * EXPERTS_PER_GROUP, axis=1)


def _router(x, whi, wlo, bias, tm):
    t, d = x.shape
    return pl.pallas_call(
        _router_kernel, grid=(t // tm,),
        in_specs=[pl.BlockSpec((tm, d), lambda i: (i, 0)), pl.BlockSpec((d, LANES), lambda i: (0, 0)),
                  pl.BlockSpec((d, LANES), lambda i: (0, 0)), pl.BlockSpec((1, LANES), lambda i: (0, 0))],
        out_specs=pl.BlockSpec((N_GROUPS, tm, LANES), lambda i: (0, i, 0)),
        out_shape=jax.ShapeDtypeStruct((N_GROUPS, t, LANES), f32), compiler_params=_cp("arbitrary"),
        name="moe_router")(x, whi, wlo, bias)


def _moe_kernel(x_ref, cw_ref, wg_ref, wu_ref, wd_ref, g_ref, b_ref, o_ref, acc_ref):
    gi = pl.program_id(1)

    @pl.when(gi == 0)
    def _():
        acc_ref[...] = jnp.zeros(acc_ref.shape, f32)

    xb = x_ref[...].astype(bf16)
    cw = cw_ref[0]
    acc = acc_ref[...]
    for e in range(EXPERTS_PER_GROUP):
        hg = jnp.dot(xb, wg_ref[0, e], preferred_element_type=f32)
        hu = jnp.dot(xb, wu_ref[0, e], preferred_element_type=f32)
        h = hg * _sigmoid(hg) * hu * cw[:, e:e + 1]
        acc = acc + jnp.dot(h.astype(bf16), wd_ref[0, e], preferred_element_type=f32)
    acc_ref[...] = acc

    @pl.when(gi == pl.num_programs(1) - 1)
    def _():
        o_ref[...] = _layer_norm(ALPHA * x_ref[...] + acc_ref[...], g_ref[...], b_ref[...])


def _moe_ln(x, cw, wg, wu, wd, g, b, tm):
    t, d = x.shape
    e, hid = wg.shape[1], wg.shape[3]
    row = lambda i, gi: (i, 0)
    fix = lambda i, gi: (0, 0)
    return pl.pallas_call(
        _moe_kernel, grid=(t // tm, N_GROUPS),
        in_specs=[pl.BlockSpec((tm, d), row), pl.BlockSpec((1, tm, LANES), lambda i, gi: (gi, i, 0)),
                  pl.BlockSpec((1, e, d, hid), lambda i, gi: (gi, 0, 0, 0)),
                  pl.BlockSpec((1, e, d, hid), lambda i, gi: (gi, 0, 0, 0)),
                  pl.BlockSpec((1, e, hid, d), lambda i, gi: (gi, 0, 0, 0)),
                  pl.BlockSpec((1, d), fix), pl.BlockSpec((1, d), fix)],
        out_specs=pl.BlockSpec((tm, d), row),
        out_shape=jax.ShapeDtypeStruct((t, d), f32),
        scratch_shapes=[pltpu.VMEM((tm, d), f32)],
        compiler_params=_cp("arbitrary", "arbitrary"), name="moe_dense")(
            x, cw, wg, wu, wd, g.reshape(1, d), b.reshape(1, d))


META_GROUP, META_RANK = EXPERTS_PER_GROUP, EXPERTS_PER_GROUP + 1


def _router_sort_kernel(x_ref, whi_ref, wlo_ref, b_ref, xc_ref, cnt_ref, carry_ref):
    @pl.when(pl.program_id(0) == 0)
    def _():
        carry_ref[...] = jnp.zeros(carry_ref.shape, f32)

    x = x_ref[...]
    tm, d = x.shape
    cw, grp = _route(_router_logits(x, whi_ref, wlo_ref, b_ref))
    cw8 = cw
    for g in range(1, N_GROUPS):
        cw8 = jnp.where(grp == float(g), pltpu.roll(cw, LANES - g * EXPERTS_PER_GROUP, axis=1), cw8)
    lanef = lax.broadcasted_iota(i32, (tm, LANES), 1).astype(f32)
    onehot = jnp.where(lanef == grp, 1.0, 0.0)
    before = lax.broadcasted_iota(i32, (tm, tm), 1) < lax.broadcasted_iota(i32, (tm, tm), 0)
    prefix = jnp.dot(before.astype(bf16), onehot.astype(bf16), preferred_element_type=f32) + carry_ref[...]
    rank = jnp.sum(onehot * prefix, axis=-1, keepdims=True)
    carry_ref[...] = carry_ref[...] + jnp.sum(onehot, axis=0, keepdims=True)
    cnt_ref[...] = carry_ref[...]
    xc_ref[:, :d] = x
    xc_ref[:, d:] = cw8 + jnp.where(lanef == float(META_GROUP), grp, 0.0) + jnp.where(lanef == float(META_RANK), rank, 0.0)


def _router_sort(x, whi, wlo, bias, tm):
    t, d = x.shape
    fix = lambda i: (0, 0)
    return pl.pallas_call(
        _router_sort_kernel, grid=(t // tm,),
        in_specs=[pl.BlockSpec((tm, d), lambda i: (i, 0)), pl.BlockSpec((d, LANES), fix), pl.BlockSpec((d, LANES), fix),
                  pl.BlockSpec((1, LANES), fix)],
        out_specs=[pl.BlockSpec((tm, d + LANES), lambda i: (i, 0)), pl.BlockSpec((1, LANES), fix)],
        out_shape=[jax.ShapeDtypeStruct((t, d + LANES), f32), jax.ShapeDtypeStruct((1, LANES), f32)],
        scratch_shapes=[pltpu.VMEM((1, LANES), f32)],
        compiler_params=_cp("arbitrary"), name="moe_router_sort")(x, whi, wlo, bias)


def _row_copy(src, i, dst, j, sem):
    return pltpu.make_async_copy(src.at[pl.ds(i, 1)], dst.at[pl.ds(j, 1)], sem)


def _dispatch_kernel(pos_ref, xc_ref, zero_ref, xs_ref, sem):
    del zero_ref
    tm = xc_ref.shape[0]
    base = pl.program_id(0) * tm

    def start(r, c):
        _row_copy(xc_ref, r, xs_ref, pos_ref[base + r], sem).start()
        return c

    def wait(r, c):
        _row_copy(xc_ref, 0, xs_ref, 0, sem).wait()
        return c

    lax.fori_loop(0, tm, start, 0, unroll=8)
    lax.fori_loop(0, tm, wait, 0, unroll=8)


def _dispatch(xc, pos, n_rows, tm):
    t, w = xc.shape
    grid_spec = pltpu.PrefetchScalarGridSpec(
        num_scalar_prefetch=1, grid=(t // tm,),
        in_specs=[pl.BlockSpec((tm, w), lambda i, pos: (i, 0)), pl.BlockSpec(memory_space=pl.ANY)],
        out_specs=pl.BlockSpec(memory_space=pl.ANY),
        scratch_shapes=[pltpu.SemaphoreType.DMA(())])
    return pl.pallas_call(
        _dispatch_kernel, grid_spec=grid_spec, out_shape=jax.ShapeDtypeStruct((n_rows, w), f32),
        input_output_aliases={2: 0}, compiler_params=_cp("arbitrary"), name="moe_dispatch")(
            pos, xc, jnp.zeros((n_rows, w), f32))


def _moe_sorted_kernel(tg_ref, xs_ref, wg_ref, wu_ref, wd_ref, ys_ref):
    del tg_ref
    d = ys_ref.shape[1]
    xb = xs_ref[:, :d].astype(bf16)
    cw = xs_ref[:, d:]
    acc = jnp.zeros(ys_ref.shape, f32)
    for e in range(EXPERTS_PER_GROUP):
        hg = jnp.dot(xb, wg_ref[0, e], preferred_element_type=f32)
        hu = jnp.dot(xb, wu_ref[0, e], preferred_element_type=f32)
        h = hg * _sigmoid(hg) * hu * cw[:, e:e + 1]
        acc = acc + jnp.dot(h.astype(bf16), wd_ref[0, e], preferred_element_type=f32)
    ys_ref[...] = acc


def _moe_sorted(xs, tile_group, wg, wu, wd, tm):
    n_rows, w = xs.shape
    e, d, hid = wg.shape[1], wg.shape[2], wg.shape[3]
    grp4 = lambda i, tg: (tg[i], 0, 0, 0)
    grid_spec = pltpu.PrefetchScalarGridSpec(
        num_scalar_prefetch=1, grid=(n_rows // tm,),
        in_specs=[pl.BlockSpec((tm, w), lambda i, tg: (i, 0)), pl.BlockSpec((1, e, d, hid), grp4),
                  pl.BlockSpec((1, e, d, hid), grp4), pl.BlockSpec((1, e, hid, d), grp4)],
        out_specs=pl.BlockSpec((tm, d), lambda i, tg: (i, 0)))
    return pl.pallas_call(
        _moe_sorted_kernel, grid_spec=grid_spec, out_shape=jax.ShapeDtypeStruct((n_rows, d), f32),
        compiler_params=_cp("arbitrary"), name="moe_sorted")(tile_group, xs, wg, wu, wd)


def _combine_kernel(pos_ref, x_ref, ys_ref, g_ref, b_ref, o_ref, ybuf, sem):
    tm = x_ref.shape[0]
    i = pl.program_id(0)
    n = pl.num_programs(0)

    def fetch(step, slot):
        def start(r, c):
            _row_copy(ys_ref, pos_ref[step * tm + r], ybuf.at[slot], r, sem.at[slot]).start()
            return c
        lax.fori_loop(0, tm, start, 0, unroll=8)

    @pl.when(i == 0)
    def _():
        fetch(0, 0)

    @pl.when(i + 1 < n)
    def _():
        fetch(i + 1, (i + 1) % 2)

    slot = i % 2

    def wait(r, c):
        _row_copy(ys_ref, 0, ybuf.at[slot], 0, sem.at[slot]).wait()
        return c

    lax.fori_loop(0, tm, wait, 0, unroll=8)
    o_ref[...] = _layer_norm(ALPHA * x_ref[...] + ybuf[slot], g_ref[...], b_ref[...])


def _combine_ln(x, ys, pos, g, b, tm):
    t, d = x.shape
    fix = lambda i, pos: (0, 0)
    grid_spec = pltpu.PrefetchScalarGridSpec(
        num_scalar_prefetch=1, grid=(t // tm,),
        in_specs=[pl.BlockSpec((tm, d), lambda i, pos: (i, 0)), pl.BlockSpec(memory_space=pl.ANY),
                  pl.BlockSpec((1, d), fix), pl.BlockSpec((1, d), fix)],
        out_specs=pl.BlockSpec((tm, d), lambda i, pos: (i, 0)),
        scratch_shapes=[pltpu.VMEM((2, tm, d), f32), pltpu.SemaphoreType.DMA((2,))])
    return pl.pallas_call(
        _combine_kernel, grid_spec=grid_spec, out_shape=jax.ShapeDtypeStruct((t, d), f32),
        compiler_params=_cp("arbitrary"), name="moe_combine_ln")(pos, x, ys, g.reshape(1, d), b.reshape(1, d))


def _moe_ln_sorted(x, whi, wlo, rbias, wg, wu, wd, g, b, tm=512):
    t, d = x.shape
    xc, cnt = _router_sort(x, whi, wlo, rbias, tm)
    grp = xc[:, d + META_GROUP].astype(i32)
    rank = xc[:, d + META_RANK].astype(i32)
    counts = cnt[0, :N_GROUPS].astype(i32)
    padded = (counts + tm - 1) // tm * tm
    ends = jnp.cumsum(padded)
    start = ends - padded
    pos = rank + sum(jnp.where(grp == gi, start[gi], 0) for gi in range(N_GROUPS))
    n_rows = t + N_GROUPS * tm
    tile_start = jnp.arange(n_rows // tm, dtype=i32) * tm
    tile_group = jnp.minimum(sum((tile_start >= ends[gi]).astype(i32) for gi in range(N_GROUPS)), N_GROUPS - 1)
    xs = _dispatch(xc, pos, n_rows, tm)
    ys = _moe_sorted(xs, tile_group, wg, wu, wd, tm)
    return _combine_ln(x, ys, pos, g, b, tm)


def _router_weights(w_rc, b_rc, w_re, b_re):
    d = w_rc.shape[0]
    ne = N_GROUPS * EXPERTS_PER_GROUP
    w = jnp.concatenate([w_re.reshape(d, ne), w_rc, jnp.zeros((d, LANES - ne - N_GROUPS), f32)], axis=1)
    bias = jnp.concatenate([b_re.reshape(ne), b_rc, jnp.zeros((LANES - ne - N_GROUPS,), f32)]).reshape(1, LANES)
    whi = w.astype(bf16)
    wlo = (w - whi.astype(f32)).astype(bf16)
    return whi, wlo, bias


def _token_major(kv_t):
    b, _, n = kv_t.shape
    return kv_t.reshape(b, 2, KV_HEADS, HEAD_DIM, n).transpose(0, 4, 1, 2, 3)


def kernel(x_prompt, x_sample, cache_cmp_kv, cache_sel_kv, cache_win_kv, state_hgrn, page_table, attn_w_in, attn_cmp_alpha, attn_w_out, rec_w_in, rec_lb_logits, rec_norm_g, rec_w_out, ln_g, ln_b, moe_w_router_c, moe_b_router_c, moe_w_router_e, moe_b_router_e, moe_w_gate, moe_w_up, moe_w_down):
    b, seq, d = x_prompt.shape
    bs, lq, _ = x_sample.shape
    tp, ts = b * seq, bs * lq
    lb_all = jnp.cumsum(jax.nn.softmax(rec_lb_logits.astype(f32), axis=0), axis=0)
    hp, hs = x_prompt.reshape(tp, d), x_sample.reshape(ts, d)
    cache_cmp_t = cache_cmp_kv.transpose(0, 1, 3, 4, 5, 2)
    cache_sel_t = cache_sel_kv.transpose(0, 1, 3, 4, 5, 2)
    kv_out = [[] for _ in range(6)]
    rec_out = [[], []]
    for layer in range(DEPTH):
        j = layer // 2
        if layer % 2 == 0:
            nq, n_in = N_HEADS * HEAD_DIM, attn_w_in.shape[2]
            w_all = attn_w_in[j]
            w_in = jnp.pad(w_all, ((0, 0), (0, -(-n_in // LANES) * LANES - n_in))).astype(bf16)
            w_qg = jnp.concatenate([w_all[:, :nq], w_all[:, nq + 6 * KV_WIDTH:]], axis=1)
            w_qg = jnp.pad(w_qg, ((0, 0), (0, -(-w_qg.shape[1] // LANES) * LANES - w_qg.shape[1]))).astype(bf16)
            w_kv_t = w_all[:, nq:nq + 6 * KV_WIDTH].T.astype(bf16)
            alpha_rows = jnp.repeat(attn_cmp_alpha[j], HEAD_DIM, axis=-1).reshape(CMP_BLOCK, 2 * KV_WIDTH).T
            w_out = attn_w_out[j].astype(bf16)
            op, c_p, s_p, w_p = _nsa_prompt(hp.reshape(b, seq, d), w_qg, w_kv_t, alpha_rows)
            os_, c_s, s_s, w_s = _nsa_sample(hs.reshape(bs, lq, d), cache_cmp_t, cache_sel_t, cache_win_kv,
                                             page_table, j, w_in, alpha_rows)
            kvs = (bs, lq, 2, KV_HEADS, HEAD_DIM)
            n_win = min(WINDOW, seq)
            for lst, val in zip(kv_out, (_token_major(c_p), c_s.reshape(kvs), _token_major(s_p), s_s.reshape(kvs),
                                         _token_major(w_p[:, :, seq - n_win:]),
                                         w_s.reshape(bs, -1, 2, KV_HEADS, HEAD_DIM))):
                lst.append(val)
        else:
            lb = lb_all[layer] - lb_all[0]
            w_in = rec_w_in[j].astype(bf16)
            w_out = rec_w_out[j].astype(bf16)
            op, st_p = _hgrn_mixer(hp.reshape(b, seq, d), jnp.zeros((b, HG_HEADS, HG_DK, HG_DV), f32), w_in, lb,
                                   rec_norm_g[j], 256)
            os_, st_s = _hgrn_mixer(hs.reshape(bs, lq, d), state_hgrn[j], w_in, lb, rec_norm_g[j], 256)
            rec_out[0].append(st_p)
            rec_out[1].append(st_s)
        hp = _proj_ln(op, w_out, hp, ln_g[layer, 0], ln_b[layer, 0], 512, "out_proj_ln")
        hs = _proj_ln(os_, w_out, hs, ln_g[layer, 0], ln_b[layer, 0], ts, "out_proj_ln_s")
        whi, wlo, rbias = _router_weights(moe_w_router_c[layer], moe_b_router_c[layer], moe_w_router_e[layer],
                                          moe_b_router_e[layer])
        wg, wu, wd = moe_w_gate[layer].astype(bf16), moe_w_up[layer].astype(bf16), moe_w_down[layer].astype(bf16)
        hp = _moe_ln_sorted(hp, whi, wlo, rbias, wg, wu, wd, ln_g[layer, 1], ln_b[layer, 1])
        hs = _moe_ln(hs, _router(hs, whi, wlo, rbias, ts), wg, wu, wd, ln_g[layer, 1], ln_b[layer, 1], ts)
    return (hp.reshape(b, seq, d), hs.reshape(bs, lq, d), *[jnp.stack(v) for v in kv_out],
            jnp.stack(rec_out[0]), jnp.stack(rec_out[1]))
```

```python
import functools

import jax
import jax.numpy as jnp
from jax import lax
from jax.experimental import pallas as pl
from jax.experimental.pallas import tpu as pltpu

f32, bf16, i32 = jnp.float32, jnp.bfloat16, jnp.int32

DEPTH = 2
N_HEADS, KV_HEADS, HEAD_DIM = 16, 4, 64
GROUP = N_HEADS // KV_HEADS
KV_WIDTH = KV_HEADS * HEAD_DIM
CMP_BLOCK, TOP_K, WINDOW = 64, 16, 512
FORCED_SCORE = float(GROUP + 1)
PAGE_SIZE = 128
HG_HEADS, HG_DK, HG_DV, HG_CHUNK = 8, 128, 128, 32
N_GROUPS, EXPERTS_PER_GROUP = 4, 8
ALPHA = (2.0 * DEPTH) ** 0.25
LN_EPS, RMS_EPS = 1e-5, 1e-6

NEG = -1e30
LOG2E = 1.4426950408889634
CMP_SHIFT = CMP_BLOCK.bit_length() - 1
assert 1 << CMP_SHIFT == CMP_BLOCK
LANES = 128
ONES_ROWS = 16
HG_HB = 8
VMEM_LIMIT = 56 * 1024 * 1024


def _cp(*sem):
    return pltpu.CompilerParams(dimension_semantics=sem, vmem_limit_bytes=VMEM_LIMIT)


def _sigmoid(x):
    return 1.0 / (1.0 + jnp.exp(-x))


def _log_sigmoid(x):
    return jnp.minimum(x, 0.0) - jnp.log1p(jnp.exp(-jnp.abs(x)))


def _layer_norm(z, g, b):
    mu = jnp.mean(z, axis=-1, keepdims=True)
    d = z - mu
    var = jnp.mean(d * d, axis=-1, keepdims=True)
    return d * lax.rsqrt(var + LN_EPS) * g + b


def _mm_kernel(x_ref, w_ref, o_ref):
    o_ref[...] = jnp.dot(x_ref[...].astype(bf16), w_ref[...], preferred_element_type=f32).astype(o_ref.dtype)


def _matmul(x, w, tm, name):
    m, k = x.shape
    n = w.shape[1]
    return pl.pallas_call(
        _mm_kernel, grid=(m // tm,),
        in_specs=[pl.BlockSpec((tm, k), lambda i: (i, 0)), pl.BlockSpec((k, n), lambda i: (0, 0))],
        out_specs=pl.BlockSpec((tm, n), lambda i: (i, 0)),
        out_shape=jax.ShapeDtypeStruct((m, n), f32), compiler_params=_cp("arbitrary"), name=name)(x, w)


def _proj_ln_kernel(o_ref, w_ref, x_ref, g_ref, b_ref, h_ref):
    y = jnp.dot(o_ref[...].astype(bf16), w_ref[...], preferred_element_type=f32)
    h_ref[...] = _layer_norm(ALPHA * x_ref[...] + y, g_ref[...], b_ref[...])


def _proj_ln(o, w, x, g, b, tm, name):
    m, k = o.shape
    n = w.shape[1]
    row = lambda i: (i, 0)
    fix = lambda i: (0, 0)
    return pl.pallas_call(
        _proj_ln_kernel, grid=(m // tm,),
        in_specs=[pl.BlockSpec((tm, k), row), pl.BlockSpec((k, n), fix), pl.BlockSpec((tm, n), row),
                  pl.BlockSpec((1, n), fix), pl.BlockSpec((1, n), fix)],
        out_specs=pl.BlockSpec((tm, n), row),
        out_shape=jax.ShapeDtypeStruct((m, n), f32), compiler_params=_cp("arbitrary"), name=name)(
            o, w, x, g.reshape(1, n), b.reshape(1, n))


def _split3(x):
    hi = x.astype(bf16)
    r1 = x - hi.astype(f32)
    mid = r1.astype(bf16)
    return hi, mid, (r1 - mid.astype(f32)).astype(bf16)


def _dot_nt(a, b):
    return lax.dot_general(a, b, (((1,), (1,)), ((), ())), preferred_element_type=f32)


def _block_sum_lanes(y, onehot):
    hi = y.astype(bf16)
    return _dot_nt(hi, onehot) + _dot_nt((y - hi.astype(f32)).astype(bf16), onehot)


def _compress_kernel(x_ref, a_ref, e_ref, o_ref):
    o_ref[0, 0] = _block_sum_lanes(x_ref[0, 0] * a_ref[0], e_ref[...])


def _compress_t(kvt, alpha_l, onehot):
    b, ch, hd, seq = kvt.shape
    nbp = onehot.shape[0]
    return pl.pallas_call(
        _compress_kernel, grid=(b, ch),
        in_specs=[pl.BlockSpec((1, 1, hd, seq), lambda i, j: (i, j, 0, 0)),
                  pl.BlockSpec((1, 1, seq), lambda i, j: (j, 0, 0)),
                  pl.BlockSpec((nbp, seq), lambda i, j: (0, 0))],
        out_specs=pl.BlockSpec((1, 1, hd, nbp), lambda i, j: (i, j, 0, 0)),
        out_shape=jax.ShapeDtypeStruct((b, ch, hd, nbp), f32),
        compiler_params=_cp("arbitrary", "arbitrary"), name="nsa_compress")(kvt, alpha_l, onehot)


def _compress_paged_kernel(pt_ref, *refs, n_pg):
    del pt_ref
    pages, a_ref, e_ref, o_ref = refs[:n_pg], refs[n_pg], refs[n_pg + 1], refs[n_pg + 2]
    w = 2 * KV_WIDTH
    x = jnp.concatenate([r[0, 0].reshape(w, PAGE_SIZE) for r in pages], axis=1)
    o_ref[0, 0] = _block_sum_lanes(x * a_ref[...], e_ref[...])


def _compress_paged(cache_t, layer, page_table, alpha_rows, n_pg):
    nb_, n_pages = page_table.shape
    bpp = PAGE_SIZE // CMP_BLOCK
    n_tok = n_pg * PAGE_SIZE
    a_t = jnp.tile(alpha_rows, (1, n_pg * bpp))
    onehot = (jnp.arange(n_pg * bpp)[:, None] == jnp.arange(n_tok)[None, :] // CMP_BLOCK).astype(bf16)

    def page_map(b, i, pt, k):
        return (layer, pt[b, i * n_pg + k], 0, 0, 0, 0)

    fix = lambda b, i, pt: (0, 0)
    grid_spec = pltpu.PrefetchScalarGridSpec(
        num_scalar_prefetch=1, grid=(nb_, n_pages // n_pg),
        in_specs=[pl.BlockSpec((1, 1, 2, KV_HEADS, HEAD_DIM, PAGE_SIZE), functools.partial(page_map, k=k))
                  for k in range(n_pg)]
        + [pl.BlockSpec(a_t.shape, fix), pl.BlockSpec(onehot.shape, fix)],
        out_specs=pl.BlockSpec((1, 1, 2 * KV_WIDTH, n_pg * bpp), lambda b, i, pt: (b, i, 0, 0)))
    return pl.pallas_call(
        functools.partial(_compress_paged_kernel, n_pg=n_pg), grid_spec=grid_spec,
        out_shape=jax.ShapeDtypeStruct((nb_, n_pages // n_pg, 2 * KV_WIDTH, n_pg * bpp), f32),
        compiler_params=_cp("arbitrary", "arbitrary"), name="nsa_compress_paged")(
            page_table, *([cache_t] * n_pg), a_t, onehot)


def _select_bias(score, k_sel):
    rowf = lax.broadcasted_iota(i32, score.shape, 0).astype(f32)
    picked = jnp.zeros(score.shape, f32)
    work = score
    for _ in range(k_sel):
        mx = jnp.max(work, axis=0, keepdims=True)
        first = jnp.min(jnp.where(work == mx, rowf, 1e9), axis=0, keepdims=True)
        pick = rowf == first
        picked = jnp.where(pick, 1.0, picked)
        work = jnp.where(pick, -jnp.inf, work)
    return jnp.where(picked > 0.0, 0.0, NEG)


def _masked_softmax_cols(s, mask, exp_fn):
    sm = jnp.where(mask, s, NEG)
    m = jnp.max(sm, axis=0, keepdims=True)
    e = jnp.where(mask, exp_fn(sm - m), 0.0)
    return e / jnp.maximum(jnp.sum(e, axis=0, keepdims=True), 1e-30)


def _nsa_in_proj_kernel(x_ref, wq_ref, wkv_ref, pq_ref, c_ref, s_ref, w_ref):
    xb = x_ref[0].astype(bf16)
    pq_ref[0] = jnp.dot(xb, wq_ref[...], preferred_element_type=f32)
    kvt = _dot_nt(wkv_ref[...], xb)
    w = 2 * KV_WIDTH
    for n, ref in enumerate((c_ref, s_ref, w_ref)):
        ref[0] = kvt[n * w:(n + 1) * w]


def _nsa_in_proj(x, wq, wkv_t, tm):
    b, seq, d = x.shape
    nqg, w = wq.shape[1], 2 * KV_WIDTH
    kv_spec = pl.BlockSpec((1, w, tm), lambda i, j: (i, 0, j))
    kv_shape = jax.ShapeDtypeStruct((b, w, seq), f32)
    return pl.pallas_call(
        _nsa_in_proj_kernel, grid=(b, seq // tm),
        in_specs=[pl.BlockSpec((1, tm, d), lambda i, j: (i, j, 0)), pl.BlockSpec((d, nqg), lambda i, j: (0, 0)),
                  pl.BlockSpec((3 * w, d), lambda i, j: (0, 0))],
        out_specs=[pl.BlockSpec((1, tm, nqg), lambda i, j: (i, j, 0)), kv_spec, kv_spec, kv_spec],
        out_shape=[jax.ShapeDtypeStruct((b, seq, nqg), f32), kv_shape, kv_shape, kv_shape],
        compiler_params=_cp("arbitrary", "arbitrary"), name="nsa_in_proj")(x, wq, wkv_t)


def _nsa_prompt_kernel(qt_ref, g_ref, kct_ref, vct_ref, kst_ref, vst_ref, kwt_ref, vwt_ref, et_ref, o_ref,
                       m_ref, acc_ref, sa_ref, sb_ref, ka_ref, va_ref, kw_ref, vwa_ref, *, tq, tk, nb):
    s0 = pl.program_id(1) * tq
    c = GROUP * tq
    nbp = et_ref.shape[1]
    qt = qt_ref[0, 0, 0]
    tq_pos = s0 + lax.broadcasted_iota(i32, (1, tq), 1)
    tpos = jnp.concatenate([tq_pos] * GROUP, axis=1)

    kc = kct_ref[0, 0, 0].T.astype(bf16)
    sc = jnp.dot(kc, qt, preferred_element_type=f32)
    blk_c = lax.broadcasted_iota(i32, (nbp, c), 0)
    pc = _masked_softmax_cols(sc, blk_c * CMP_BLOCK + (CMP_BLOCK - 1) <= tpos, jnp.exp2)
    oc = jnp.dot(vct_ref[0, 0, 0].astype(bf16), pc.astype(bf16), preferred_element_type=f32)

    imp = pc[:, 0:tq]
    for g in range(1, GROUP):
        imp = imp + pc[:, g * tq:(g + 1) * tq]
    blk = lax.broadcasted_iota(i32, (nbp, tq), 0)
    cur = jnp.right_shift(tq_pos, CMP_SHIFT)
    forced = (blk == 0) | (blk == cur) | (blk == cur - 1)
    base = jnp.where(forced, FORCED_SCORE, jnp.where(blk * CMP_BLOCK <= tq_pos, 0.0, -1.0))
    sel = _select_bias(jnp.where(blk < nb, base + imp, -3.0), min(TOP_K, nb)).astype(bf16)
    qa = jnp.concatenate([jnp.concatenate([sel] * GROUP, axis=1), qt], axis=0)

    m_ref[...] = jnp.full(m_ref.shape, NEG, f32)
    acc_ref[...] = jnp.zeros(acc_ref.shape, f32)

    @pl.when(pl.program_id(1) == 0)
    def _():
        ones_rows = jnp.ones((ONES_ROWS, tk), bf16)

        def fill(t, carry):
            k0 = pl.multiple_of(t * tk, tk)
            ka_ref[pl.ds(k0, tk), :] = jnp.concatenate(
                [et_ref[pl.ds(k0, tk), :], kst_ref[0, 0, 0, :, pl.ds(k0, tk)].T.astype(bf16)], axis=1)
            va_ref[:, pl.ds(k0, tk)] = jnp.concatenate([vst_ref[0, 0, 0, :, pl.ds(k0, tk)].astype(bf16), ones_rows], axis=0)
            kw_ref[pl.ds(k0, tk), :] = kwt_ref[0, 0, 0, :, pl.ds(k0, tk)].T.astype(bf16)
            vwa_ref[:, pl.ds(k0, tk)] = jnp.concatenate([vwt_ref[0, 0, 0, :, pl.ds(k0, tk)].astype(bf16), ones_rows], axis=0)
            return carry

        lax.fori_loop(0, ka_ref.shape[0] // tk, fill, 0)

    def produce(j, s_ref):
        k0 = pl.multiple_of(j * tk, tk)
        s_ref[...] = jnp.dot(ka_ref[pl.ds(k0, tk), :], qa, preferred_element_type=f32)

    def consume(j, s_ref, width, causal):
        k0 = pl.multiple_of(j * tk, tk)
        s = s_ref[0:width, :]
        if causal:
            tail = jnp.where(k0 + (width - tq) + lax.broadcasted_iota(i32, (tq, c), 0) <= tpos, s[width - tq:], NEG)
            s = tail if width == tq else jnp.concatenate([s[:width - tq], tail], axis=0)
        m_new = jnp.maximum(m_ref[...], jnp.max(s, axis=0, keepdims=True))
        a = jnp.exp2(m_ref[...] - m_new)
        p = jnp.exp2(s - m_new).astype(bf16)
        acc_ref[...] = a * acc_ref[...] + jnp.dot(va_ref[:, pl.ds(k0, width)], p, preferred_element_type=f32)
        m_ref[...] = m_new

    n_full = s0 // tk
    produce(0, sa_ref)

    def pair(i, carry):
        produce(2 * i + 1, sb_ref)
        consume(2 * i, sa_ref, tk, False)
        produce(2 * i + 2, sa_ref)
        consume(2 * i + 1, sb_ref, tk, False)
        return carry

    lax.fori_loop(0, n_full // 2, pair, 0)
    odd = n_full % 2 == 1

    @pl.when(odd)
    def _():
        produce(n_full, sb_ref)
        consume(n_full - 1, sa_ref, tk, False)

    for wi in range(1, tk // tq + 1):
        here = s0 - n_full * tk == (wi - 1) * tq

        @pl.when(here & odd)
        def _():
            consume(n_full, sb_ref, wi * tq, True)

        @pl.when(here & jnp.logical_not(odd))
        def _():
            consume(n_full, sa_ref, wi * tq, True)

    osel = acc_ref[0:HEAD_DIM] / jnp.maximum(acc_ref[HEAD_DIM:HEAD_DIM + 1], 1e-30)

    wt = WINDOW + tq
    k0w = pl.multiple_of(jnp.maximum(s0 - WINDOW, 0), LANES)
    s = jnp.dot(kw_ref[pl.ds(k0w, wt), :], qt, preferred_element_type=f32)
    dist = tpos - (k0w + lax.broadcasted_iota(i32, (wt, c), 0))
    s = jnp.where(dist >= 0, jnp.where(dist < WINDOW, s, NEG), NEG)
    p = jnp.exp2(s - jnp.max(s, axis=0, keepdims=True)).astype(bf16)
    ow = jnp.dot(vwa_ref[:, pl.ds(k0w, wt)], p, preferred_element_type=f32)
    ow = ow[0:HEAD_DIM] / jnp.maximum(ow[HEAD_DIM:HEAD_DIM + 1], 1e-30)

    gt = _sigmoid(g_ref[0, 0, 0])
    o_ref[0, 0, 0] = (gt[0:1] * oc + gt[1:2] * osel + gt[2:3] * ow).astype(o_ref.dtype)


def _nsa_prompt_attention(qt, gt, cblk_t, kvs_t, kvw_t, onehot_t, tq, tk):
    b, kvh, n_t, hd, c = qt.shape
    seq, nbp = onehot_t.shape
    assert tk % tq == 0
    tile = lambda bh, i: (bh // kvh, bh % kvh, i, 0, 0)
    comp = lambda cc: (lambda bh, i: (bh // kvh, cc, bh % kvh, 0, 0))
    rows = lambda n: (pl.BlockSpec((1, 1, 1, hd, n), comp(0)), pl.BlockSpec((1, 1, 1, hd, n), comp(1)))
    return pl.pallas_call(
        functools.partial(_nsa_prompt_kernel, tq=tq, tk=tk, nb=seq // CMP_BLOCK), grid=(b * kvh, n_t),
        in_specs=[pl.BlockSpec((1, 1, 1, hd, c), tile), pl.BlockSpec((1, 1, 1, 3, c), tile),
                  *rows(nbp), *rows(seq), *rows(seq), pl.BlockSpec((seq, nbp), lambda bh, i: (0, 0))],
        out_specs=pl.BlockSpec((1, 1, 1, hd, c), tile),
        out_shape=jax.ShapeDtypeStruct((b, kvh, n_t, hd, c), bf16),
        scratch_shapes=[pltpu.VMEM((1, c), f32), pltpu.VMEM((hd + ONES_ROWS, c), f32),
                        pltpu.VMEM((tk, c), f32), pltpu.VMEM((tk, c), f32),
                        pltpu.VMEM((seq, nbp + hd), bf16), pltpu.VMEM((hd + ONES_ROWS, seq), bf16),
                        pltpu.VMEM((seq, hd), bf16), pltpu.VMEM((hd + ONES_ROWS, seq), bf16)],
        compiler_params=_cp("arbitrary", "arbitrary"), name="nsa_prompt_attn")(
            qt, gt, cblk_t, cblk_t, kvs_t, kvs_t, kvw_t, kvw_t, onehot_t)


def _nsa_prompt(x, wq, wkv_t, alpha_rows, tq=128, tk=512):
    b, seq, d = x.shape
    n_t, nb = seq // tq, seq // CMP_BLOCK
    nq = N_HEADS * HEAD_DIM
    pq, kvc_t, kvs_t, kvw_t = _nsa_in_proj(x, wq, wkv_t, 512)
    five = lambda a: a.reshape(b, 2, KV_HEADS, HEAD_DIM, -1)
    nbp = -(-nb // LANES) * LANES
    onehot = (jnp.arange(nbp)[:, None] == jnp.arange(seq)[None, :] // CMP_BLOCK).astype(bf16)
    alpha_l = jnp.tile(alpha_rows[::HEAD_DIM], (1, nb)).reshape(2 * KV_HEADS, 1, seq)
    cblk_t = _compress_t(kvc_t.reshape(b, 2 * KV_HEADS, HEAD_DIM, seq), alpha_l, onehot)
    qs = (pq[..., :nq] * (HEAD_DIM ** -0.5 * LOG2E)).astype(bf16).reshape(b, n_t, tq, KV_HEADS, GROUP, HEAD_DIM)
    qt = qs.transpose(0, 3, 1, 5, 4, 2).reshape(b, KV_HEADS, n_t, HEAD_DIM, GROUP * tq)
    gt = pq[..., nq:nq + 3 * N_HEADS].reshape(b, n_t, tq, KV_HEADS, GROUP, 3)
    gt = gt.transpose(0, 3, 1, 5, 4, 2).reshape(b, KV_HEADS, n_t, 3, GROUP * tq)
    ot = _nsa_prompt_attention(qt, gt, five(cblk_t), five(kvs_t), five(kvw_t), onehot.T, tq, tk)
    o = ot.reshape(b, KV_HEADS, n_t, HEAD_DIM, GROUP, tq).transpose(0, 2, 5, 1, 4, 3).reshape(b * seq, d)
    return o, kvc_t, kvs_t, kvw_t


def _dot_t0(a, b):
    return lax.dot_general(a, b, (((0,), (0,)), ((), ())), preferred_element_type=f32)


def _smp_cmp_win_kernel(q_ref, kc_ref, vc_ref, cn_ref, a_ref, kw_ref, vw_ref, oc_ref, ow_ref, sel_ref, *, past, lq,
                        nb_real):
    q = q_ref[0]
    nbp = kc_ref.shape[1]
    col = lax.broadcasted_iota(i32, (1, LANES), 1)
    tpos = past + jnp.bitwise_and(col, lq - 1)

    c_new = jnp.sum(cn_ref[0] * a_ref[...], axis=0, keepdims=True).astype(bf16)
    is_new = lax.broadcasted_iota(i32, (nbp, KV_WIDTH), 0) == nb_real - 1
    kc = jnp.where(is_new, c_new[:, :KV_WIDTH], kc_ref[0])
    vc = jnp.where(is_new, c_new[:, KV_WIDTH:], vc_ref[0])
    sc = jnp.dot(kc, q, preferred_element_type=f32)
    blk = lax.broadcasted_iota(i32, (nbp, LANES), 0)
    pc = _masked_softmax_cols(sc, blk * CMP_BLOCK + (CMP_BLOCK - 1) <= tpos, jnp.exp)
    oc_ref[0] = _dot_t0(pc.astype(bf16), vc)

    r = lax.broadcasted_iota(i32, (LANES, LANES), 0)
    cc = lax.broadcasted_iota(i32, (LANES, LANES), 1)
    per_kv = GROUP * lq
    same = jnp.where(r // per_kv == cc // per_kv, jnp.where(r % lq == cc % lq, 1.0, 0.0), 0.0).astype(bf16)
    hi, mid, lo = _split3(pc)
    imp = (jnp.dot(hi, same, preferred_element_type=f32) + jnp.dot(mid, same, preferred_element_type=f32)
           + jnp.dot(lo, same, preferred_element_type=f32))
    cur = jnp.right_shift(tpos, CMP_SHIFT)
    forced = (blk == 0) | (blk == cur) | (blk == cur - 1)
    base = jnp.where(forced, FORCED_SCORE, jnp.where(blk * CMP_BLOCK <= tpos, 0.0, -1.0))
    score = jnp.where(blk < nb_real, base + imp, -3.0)
    sel_ref[0] = _select_bias(score, min(TOP_K, nb_real))

    nwt = kw_ref.shape[1]
    nw = nwt - lq
    s = jnp.dot(kw_ref[0], q, preferred_element_type=f32)
    wpos = past - nw + lax.broadcasted_iota(i32, (nwt, LANES), 0)
    dist = tpos - wpos
    ok = (dist >= 0) & (dist < WINDOW) & (wpos >= 0)
    pw = _masked_softmax_cols(s, ok, jnp.exp)
    ow_ref[0] = _dot_t0(pw.astype(bf16), vw_ref[0])


def _smp_sel_kernel(pt_ref, *refs, n_pg):
    del pt_ref
    pages = refs[:n_pg]
    q_ref, sel_ref, e_ref, kn_ref, vn_ref, bn_ref, g_ref, oc_ref, ow_ref, o_ref, m_ref, l_ref, acc_ref = refs[n_pg:]
    i = pl.program_id(1)

    @pl.when(i == 0)
    def _():
        m_ref[...] = jnp.full(m_ref.shape, NEG, f32)
        l_ref[...] = jnp.zeros(l_ref.shape, f32)
        acc_ref[...] = jnp.zeros(acc_ref.shape, f32)

    def update(s, vt):
        m_new = jnp.maximum(m_ref[...], jnp.max(s, axis=1, keepdims=True))
        a = jnp.exp(m_ref[...] - m_new)
        p = jnp.exp(s - m_new)
        l_ref[...] = a * l_ref[...] + jnp.sum(p, axis=1, keepdims=True)
        acc_ref[...] = a * acc_ref[...] + _dot_nt(p.astype(bf16), vt)
        m_ref[...] = m_new

    qa = jnp.concatenate([q_ref[0], sel_ref[0, 0].astype(bf16)], axis=1)
    kt = jnp.concatenate([r[0, 0, 0].reshape(KV_WIDTH, PAGE_SIZE) for r in pages], axis=1).astype(bf16)
    vt = jnp.concatenate([r[0, 0, 1].reshape(KV_WIDTH, PAGE_SIZE) for r in pages], axis=1).astype(bf16)
    update(jnp.dot(qa, jnp.concatenate([kt, e_ref[...]], axis=0), preferred_element_type=f32), vt)

    @pl.when(i == pl.num_programs(1) - 1)
    def _():
        update(jnp.dot(q_ref[0], kn_ref[0].astype(bf16), preferred_element_type=f32) + bn_ref[0], vn_ref[0].astype(bf16))
        gt = _sigmoid(g_ref[0])
        osel = acc_ref[...] / jnp.maximum(l_ref[...], 1e-30)
        o_ref[0] = gt[:, 0:1] * oc_ref[0] + gt[:, 1:2] * osel + gt[:, 2:3] * ow_ref[0]


def _nsa_sample(x, cache_cmp_t, cache_sel_t, cache_win, page_table, layer, w_in, alpha_rows, n_pg=16):
    b, lq, d = x.shape
    n_pages = page_table.shape[1]
    past = n_pages * PAGE_SIZE
    assert KV_HEADS * GROUP * lq == LANES and past % CMP_BLOCK == 0 and lq <= CMP_BLOCK
    assert n_pages % n_pg == 0 and n_pg * PAGE_SIZE // CMP_BLOCK <= LANES
    p = _matmul(x.reshape(b * lq, d), w_in, b * lq, "nsa_in_proj_s")
    nq = N_HEADS * HEAD_DIM
    w2 = 2 * KV_WIDTH
    q = (p[:, :nq] * HEAD_DIM ** -0.5).reshape(b, lq, KV_HEADS, GROUP, HEAD_DIM)
    eye = jnp.eye(KV_HEADS, dtype=f32)
    qbd = (q.transpose(0, 2, 3, 1, 4)[:, :, :, :, None, :] * eye[None, :, None, None, :, None])
    qbd = qbd.reshape(b, LANES, KV_WIDTH).astype(bf16)
    qbd_t = qbd.transpose(0, 2, 1)
    gt = p[:, nq + 6 * KV_WIDTH:nq + 6 * KV_WIDTH + 3 * N_HEADS].reshape(b, lq, KV_HEADS, GROUP, 3)
    gt = gt.transpose(0, 2, 3, 1, 4).reshape(b, LANES, 3)
    kv = p[:, nq:nq + 6 * KV_WIDTH].reshape(b, lq, 3, w2)
    kv_c, kv_s, kv_w = kv[:, :, 0], kv[:, :, 1], kv[:, :, 2]

    bpp = PAGE_SIZE // CMP_BLOCK
    c_past = _compress_paged(cache_cmp_t, layer, page_table, alpha_rows, n_pg)
    c_past = c_past.transpose(0, 1, 3, 2).reshape(b, past // CMP_BLOCK, w2)
    nb_real = past // CMP_BLOCK + 1
    nbp = -(-nb_real // 8) * 8
    cblk = jnp.pad(c_past, ((0, 0), (0, nbp - nb_real + 1), (0, 0)))
    wkv = jnp.concatenate([cache_win[layer].reshape(b, -1, w2), kv_w], axis=1)
    nwt = wkv.shape[1]

    seq3 = lambda i: (i, 0, 0)
    oc, ow, sel = pl.pallas_call(
        functools.partial(_smp_cmp_win_kernel, past=past, lq=lq, nb_real=nb_real), grid=(b,),
        in_specs=[pl.BlockSpec((1, KV_WIDTH, LANES), seq3), pl.BlockSpec((1, nbp, KV_WIDTH), seq3),
                  pl.BlockSpec((1, nbp, KV_WIDTH), seq3), pl.BlockSpec((1, lq, w2), seq3),
                  pl.BlockSpec((lq, w2), lambda i: (0, 0)), pl.BlockSpec((1, nwt, KV_WIDTH), seq3),
                  pl.BlockSpec((1, nwt, KV_WIDTH), seq3)],
        out_specs=[pl.BlockSpec((1, LANES, KV_WIDTH), seq3), pl.BlockSpec((1, LANES, KV_WIDTH), seq3),
                   pl.BlockSpec((1, nbp, LANES), seq3)],
        out_shape=[jax.ShapeDtypeStruct((b, LANES, KV_WIDTH), f32), jax.ShapeDtypeStruct((b, LANES, KV_WIDTH), f32),
                   jax.ShapeDtypeStruct((b, nbp, LANES), f32)],
        compiler_params=_cp("arbitrary"), name="nsa_sample_cmp_win")(
            qbd_t, cblk[:, :, :KV_WIDTH].astype(bf16), cblk[:, :, KV_WIDTH:].astype(bf16), kv_c, alpha_rows.T[:lq],
            wkv[:, :, :KV_WIDTH].astype(bf16), wkv[:, :, KV_WIDTH:].astype(bf16))

    n_steps, bps = n_pages // n_pg, n_pg * bpp
    sel_r = sel.transpose(0, 2, 1)
    sel_steps = sel_r[:, :, :past // CMP_BLOCK].reshape(b, LANES, n_steps, bps).transpose(0, 2, 1, 3)
    sel_steps = jnp.pad(sel_steps, ((0, 0), (0, 0), (0, 0), (0, LANES - bps)))
    onehot = (jnp.arange(LANES)[:, None] == jnp.arange(n_pg * PAGE_SIZE)[None, :] // CMP_BLOCK).astype(bf16)
    pad_t = lambda a: jnp.pad(a.transpose(0, 2, 1), ((0, 0), (0, 0), (0, LANES - lq)))
    qoff = jnp.arange(LANES) % lq
    readable = jnp.arange(LANES)[None, :] <= qoff[:, None]
    bias_new = jnp.where(readable[None], sel_r[:, :, past // CMP_BLOCK][:, :, None], NEG)

    def page_map(bb, i, pt, k):
        return (layer, pt[bb, i * n_pg + k], 0, 0, 0, 0)

    seq3p = lambda bb, i, pt: (bb, 0, 0)
    grid_spec = pltpu.PrefetchScalarGridSpec(
        num_scalar_prefetch=1, grid=(b, n_steps),
        in_specs=[pl.BlockSpec((1, 1, 2, KV_HEADS, HEAD_DIM, PAGE_SIZE), functools.partial(page_map, k=k))
                  for k in range(n_pg)]
        + [pl.BlockSpec((1, LANES, KV_WIDTH), seq3p), pl.BlockSpec((1, 1, LANES, LANES), lambda bb, i, pt: (bb, i, 0, 0)),
           pl.BlockSpec(onehot.shape, lambda bb, i, pt: (0, 0)),
           pl.BlockSpec((1, KV_WIDTH, LANES), seq3p), pl.BlockSpec((1, KV_WIDTH, LANES), seq3p),
           pl.BlockSpec((1, LANES, LANES), seq3p), pl.BlockSpec((1, LANES, 3), seq3p),
           pl.BlockSpec((1, LANES, KV_WIDTH), seq3p), pl.BlockSpec((1, LANES, KV_WIDTH), seq3p)],
        out_specs=pl.BlockSpec((1, LANES, KV_WIDTH), seq3p),
        scratch_shapes=[pltpu.VMEM((LANES, 1), f32), pltpu.VMEM((LANES, 1), f32), pltpu.VMEM((LANES, KV_WIDTH), f32)])
    o = pl.pallas_call(
        functools.partial(_smp_sel_kernel, n_pg=n_pg), grid_spec=grid_spec,
        out_shape=jax.ShapeDtypeStruct((b, LANES, KV_WIDTH), f32),
        compiler_params=_cp("arbitrary", "arbitrary"), name="nsa_sample_sel")(
            page_table, *([cache_sel_t] * n_pg), qbd, sel_steps, onehot, pad_t(kv_s[:, :, :KV_WIDTH]),
            pad_t(kv_s[:, :, KV_WIDTH:]), bias_new, gt, oc, ow)
    o = o.reshape(b, KV_HEADS, GROUP, lq, KV_HEADS, HEAD_DIM)
    o = jnp.stack([o[:, h, :, :, h] for h in range(KV_HEADS)], axis=1)
    o = o.transpose(0, 3, 1, 2, 4).reshape(b * lq, d)
    return o, kv_c, kv_s, wkv[:, lq:]


def _hgrn_kernel(zq_ref, zf_ref, zi_ref, zg_ref, lb_ref, ng_ref, s0_ref, o_ref, sout_ref, st_ref, *, n_valid, hb):
    t = pl.program_id(2)
    tb = zq_ref.shape[1]
    n_ch = tb // HG_CHUNK

    @pl.when(t == 0)
    def _():
        for hh in range(hb):
            st_ref[hh] = s0_ref[0, hh].T

    r = lax.broadcasted_iota(i32, (tb, tb), 0)
    cc = lax.broadcasted_iota(i32, (tb, tb), 1)
    tri = jnp.where(r // HG_CHUNK == cc // HG_CHUNK, jnp.where(cc <= r, 1.0, 0.0), 0.0).astype(bf16)
    causal = lax.broadcasted_iota(i32, (HG_CHUNK, HG_CHUNK), 1) <= lax.broadcasted_iota(i32, (HG_CHUNK, HG_CHUNK), 0)
    mid_row = HG_CHUNK // 2

    zf = zf_ref[0]
    lb = lb_ref[0]
    la = jnp.log(lb)
    l1 = jnp.log1p(-lb)
    bb = l1 + _log_sigmoid(zf)
    log_f = jnp.maximum(la, bb) + jnp.log1p(jnp.exp(-jnp.abs(la - bb)))
    kk = jnp.exp(l1 + _log_sigmoid(-zf))
    if n_valid < tb:
        live = lax.broadcasted_iota(i32, zf.shape, 0) < n_valid
        log_f = jnp.where(live, log_f, 0.0)
        kk = jnp.where(live, kk, 0.0)
    hi, mid, lo = _split3(log_f)
    cum = (jnp.dot(tri, hi, preferred_element_type=f32) + jnp.dot(tri, mid, preferred_element_type=f32)
           + jnp.dot(tri, lo, preferred_element_type=f32))
    zq, zi, zg = zq_ref[0], zi_ref[0], zg_ref[0]
    gate = ng_ref[0] * (zg * _sigmoid(zg))

    qs, o_in, upd, dec = {}, {}, {}, {}
    for hh in range(hb):
        hs = slice(hh * HG_DK, (hh + 1) * HG_DK)
        for ci in range(n_ch):
            sl = slice(ci * HG_CHUNK, (ci + 1) * HG_CHUNK)
            cm, qi, ki, vi = cum[sl, hs], zq[sl, hs], kk[sl, hs], zi[sl, hs]
            ref = cm[mid_row:mid_row + 1]
            last = cm[HG_CHUNK - 1:HG_CHUNK]
            a = _dot_nt((qi * jnp.exp(cm - ref)).astype(bf16), (ki * jnp.exp(ref - cm)).astype(bf16))
            a = jnp.where(causal, a, 0.0)
            o_in[hh, ci] = jnp.dot(a.astype(bf16), vi.astype(bf16), preferred_element_type=f32)
            qs[hh, ci] = (qi * jnp.exp(cm)).astype(bf16)
            upd[hh, ci] = _dot_t0(vi.astype(bf16), (ki * jnp.exp(last - cm)).astype(bf16))
            dec[hh, ci] = jnp.exp(last)
    for hh in range(hb):
        hs = slice(hh * HG_DK, (hh + 1) * HG_DK)
        st = st_ref[hh]
        outs = []
        for ci in range(n_ch):
            outs.append(o_in[hh, ci] + _dot_nt(qs[hh, ci], st.astype(bf16)))
            st = st * dec[hh, ci] + upd[hh, ci]
        st_ref[hh] = st
        o = jnp.concatenate(outs, axis=0) if n_ch > 1 else outs[0]
        o = o * lax.rsqrt(jnp.mean(o * o, axis=-1, keepdims=True) + RMS_EPS)
        o_ref[0, :, hs] = (o * gate[:, hs]).astype(o_ref.dtype)

        @pl.when(t == pl.num_programs(2) - 1)
        def _():
            sout_ref[0, hh] = st.T


def _hgrn(p, s0, lb, norm_g, tb, n_valid, hb):
    b, lp, _ = p.shape
    ng = HG_HEADS // hb
    w = hb * HG_DK
    col = lambda off: (lambda bb, hh, t: (bb, t, off * ng + hh))
    per_head = lambda bb, hh, t: (0, hh)
    state = lambda bb, hh, t: (bb, hh, 0, 0)
    return pl.pallas_call(
        functools.partial(_hgrn_kernel, n_valid=n_valid, hb=hb), grid=(b, ng, lp // tb),
        in_specs=[pl.BlockSpec((1, tb, w), col(0)), pl.BlockSpec((1, tb, w), col(1)),
                  pl.BlockSpec((1, tb, w), col(2)), pl.BlockSpec((1, tb, w), col(3)),
                  pl.BlockSpec((1, w), per_head), pl.BlockSpec((1, w), per_head),
                  pl.BlockSpec((1, hb, HG_DK, HG_DV), state)],
        out_specs=[pl.BlockSpec((1, tb, w), lambda bb, hh, t: (bb, t, hh)),
                   pl.BlockSpec((1, hb, HG_DK, HG_DV), state)],
        out_shape=[jax.ShapeDtypeStruct((b, lp, HG_HEADS * HG_DV), bf16),
                   jax.ShapeDtypeStruct((b, HG_HEADS, HG_DK, HG_DV), f32)],
        scratch_shapes=[pltpu.VMEM((hb, HG_DV, HG_DK), f32)],
        compiler_params=_cp("arbitrary", "arbitrary", "arbitrary"), name="hgrn_scan")(
            p, p, p, p, lb.reshape(1, -1), norm_g.reshape(1, -1), s0)


def _hgrn_mixer(x, s0, w_in, lb, norm_g, tb):
    b, seq, d = x.shape
    lp = -(-seq // HG_CHUNK) * HG_CHUNK
    tb = min(tb, lp)
    p = _matmul(x.reshape(b * seq, d), w_in, min(256, b * seq), "hgrn_in_proj").reshape(b, seq, 4 * d)
    if lp != seq:
        p = jnp.pad(p, ((0, 0), (0, lp - seq), (0, 0)))
    o, s_out = _hgrn(p, s0, lb, norm_g, tb, seq if lp != seq else lp, HG_HB)
    return o[:, :seq].reshape(b * seq, d), s_out


def _router_logits(x, whi_ref, wlo_ref, b_ref):
    xh = x.astype(bf16)
    xl = (x - xh.astype(f32)).astype(bf16)
    return (jnp.dot(xh, whi_ref[...], preferred_element_type=f32) + jnp.dot(xl, whi_ref[...], preferred_element_type=f32)
            + jnp.dot(xh, wlo_ref[...], preferred_element_type=f32)) + b_ref[...]


def _route(lg):
    ne = N_GROUPS * EXPERTS_PER_GROUP
    lane = lax.broadcasted_iota(i32, lg.shape, 1)
    lanef = lane.astype(f32)
    in_c = (lane >= ne) & (lane < ne + N_GROUPS)
    lc = jnp.where(in_c, lg, -jnp.inf)
    mc = jnp.max(lc, axis=-1, keepdims=True)
    grp = jnp.min(jnp.where(lc == mc, lanef, 1e9), axis=-1, keepdims=True) - float(ne)
    pg = 1.0 / jnp.sum(jnp.where(in_c, jnp.exp(lc - mc), 0.0), axis=-1, keepdims=True)
    e_lo = grp * float(EXPERTS_PER_GROUP)
    in_e = (lanef >= e_lo) & (lanef < e_lo + float(EXPERTS_PER_GROUP))
    le = jnp.where(in_e, lg, -jnp.inf)
    v1 = jnp.max(le, axis=-1, keepdims=True)
    i1 = jnp.min(jnp.where(le == v1, lanef, 1e9), axis=-1, keepdims=True)
    le2 = jnp.where(lanef == i1, -jnp.inf, le)
    v2 = jnp.max(le2, axis=-1, keepdims=True)
    i2 = jnp.min(jnp.where(le2 == v2, lanef, 1e9), axis=-1, keepdims=True)
    e2 = jnp.exp(v2 - v1)
    w1 = pg / (1.0 + e2)
    return jnp.where(lanef == i1, w1, 0.0) + jnp.where(lanef == i2, w1 * e2, 0.0), grp


def _router_kernel(x_ref, whi_ref, wlo_ref, b_ref, cw_ref):
    cw, _ = _route(_router_logits(x_ref[...], whi_ref, wlo_ref, b_ref))
    for g in range(N_GROUPS):
        cw_ref[g] = cw if g == 0 else pltpu.roll(cw, LANES - g * EXPERTS_PER_GROUP, axis=1)


def _router(x, whi, wlo, bias, tm):
    t, d = x.shape
    return pl.pallas_call(
        _router_kernel, grid=(t // tm,),
        in_specs=[pl.BlockSpec((tm, d), lambda i: (i, 0)), pl.BlockSpec((d, LANES), lambda i: (0, 0)),
                  pl.BlockSpec((d, LANES), lambda i: (0, 0)), pl.BlockSpec((1, LANES), lambda i: (0, 0))],
        out_specs=pl.BlockSpec((N_GROUPS, tm, LANES), lambda i: (0, i, 0)),
        out_shape=jax.ShapeDtypeStruct((N_GROUPS, t, LANES), f32), compiler_params=_cp("arbitrary"),
        name="moe_router")(x, whi, wlo, bias)


def _moe_kernel(x_ref, cw_ref, wg_ref, wu_ref, wd_ref, g_ref, b_ref, o_ref, acc_ref):
    gi = pl.program_id(1)

    @pl.when(gi == 0)
    def _():
        acc_ref[...] = jnp.zeros(acc_ref.shape, f32)

    xb = x_ref[...].astype(bf16)
    cw = cw_ref[0]
    acc = acc_ref[...]
    for e in range(EXPERTS_PER_GROUP):
        hg = jnp.dot(xb, wg_ref[0, e], preferred_element_type=f32)
        hu = jnp.dot(xb, wu_ref[0, e], preferred_element_type=f32)
        h = hg * _sigmoid(hg) * hu * cw[:, e:e + 1]
        acc = acc + jnp.dot(h.astype(bf16), wd_ref[0, e], preferred_element_type=f32)
    acc_ref[...] = acc

    @pl.when(gi == pl.num_programs(1) - 1)
    def _():
        o_ref[...] = _layer_norm(ALPHA * x_ref[...] + acc_ref[...], g_ref[...], b_ref[...])


def _moe_ln(x, cw, wg, wu, wd, g, b, tm):
    t, d = x.shape
    e, hid = wg.shape[1], wg.shape[3]
    row = lambda i, gi: (i, 0)
    fix = lambda i, gi: (0, 0)
    return pl.pallas_call(
        _moe_kernel, grid=(t // tm, N_GROUPS),
        in_specs=[pl.BlockSpec((tm, d), row), pl.BlockSpec((1, tm, LANES), lambda i, gi: (gi, i, 0)),
                  pl.BlockSpec((1, e, d, hid), lambda i, gi: (gi, 0, 0, 0)),
                  pl.BlockSpec((1, e, d, hid), lambda i, gi: (gi, 0, 0, 0)),
                  pl.BlockSpec((1, e, hid, d), lambda i, gi: (gi, 0, 0, 0)),
                  pl.BlockSpec((1, d), fix), pl.BlockSpec((1, d), fix)],
        out_specs=pl.BlockSpec((tm, d), row),
        out_shape=jax.ShapeDtypeStruct((t, d), f32),
        scratch_shapes=[pltpu.VMEM((tm, d), f32)],
        compiler_params=_cp("arbitrary", "arbitrary"), name="moe_dense")(
            x, cw, wg, wu, wd, g.reshape(1, d), b.reshape(1, d))


META_GROUP, META_RANK = EXPERTS_PER_GROUP, EXPERTS_PER_GROUP + 1


def _router_sort_kernel(x_ref, whi_ref, wlo_ref, b_ref, xc_ref, cnt_ref, carry_ref):
    @pl.when(pl.program_id(0) == 0)
    def _():
        carry_ref[...] = jnp.zeros(carry_ref.shape, f32)

    x = x_ref[...]
    tm, d = x.shape
    cw, grp = _route(_router_logits(x, whi_ref, wlo_ref, b_ref))
    cw8 = cw
    for g in range(1, N_GROUPS):
        cw8 = jnp.where(grp == float(g), pltpu.roll(cw, LANES - g * EXPERTS_PER_GROUP, axis=1), cw8)
    lanef = lax.broadcasted_iota(i32, (tm, LANES), 1).astype(f32)
    onehot = jnp.where(lanef == grp, 1.0, 0.0)
    before = lax.broadcasted_iota(i32, (tm, tm), 1) < lax.broadcasted_iota(i32, (tm, tm), 0)
    prefix = jnp.dot(before.astype(bf16), onehot.astype(bf16), preferred_element_type=f32) + carry_ref[...]
    rank = jnp.sum(onehot * prefix, axis=-1, keepdims=True)
    carry_ref[...] = carry_ref[...] + jnp.sum(onehot, axis=0, keepdims=True)
    cnt_ref[...] = carry_ref[...]
    xc_ref[:, :d] = x
    xc_ref[:, d:] = cw8 + jnp.where(lanef == float(META_GROUP), grp, 0.0) + jnp.where(lanef == float(META_RANK), rank, 0.0)


def _router_sort(x, whi, wlo, bias, tm):
    t, d = x.shape
    fix = lambda i: (0, 0)
    return pl.pallas_call(
        _router_sort_kernel, grid=(t // tm,),
        in_specs=[pl.BlockSpec((tm, d), lambda i: (i, 0)), pl.BlockSpec((d, LANES), fix), pl.BlockSpec((d, LANES), fix),
                  pl.BlockSpec((1, LANES), fix)],
        out_specs=[pl.BlockSpec((tm, d + LANES), lambda i: (i, 0)), pl.BlockSpec((1, LANES), fix)],
        out_shape=[jax.ShapeDtypeStruct((t, d + LANES), f32), jax.ShapeDtypeStruct((1, LANES), f32)],
        scratch_shapes=[pltpu.VMEM((1, LANES), f32)],
        compiler_params=_cp("arbitrary"), name="moe_router_sort")(x, whi, wlo, bias)


def _row_copy(src, i, dst, j, sem):
    return pltpu.make_async_copy(src.at[pl.ds(i, 1)], dst.at[pl.ds(j, 1)], sem)


def _dispatch_kernel(pos_ref, xc_ref, zero_ref, xs_ref, sem):
    del zero_ref
    tm = xc_ref.shape[0]
    base = pl.program_id(0) * tm

    def start(r, c):
        _row_copy(xc_ref, r, xs_ref, pos_ref[base + r], sem).start()
        return c

    def wait(r, c):
        _row_copy(xc_ref, 0, xs_ref, 0, sem).wait()
        return c

    lax.fori_loop(0, tm, start, 0, unroll=8)
    lax.fori_loop(0, tm, wait, 0, unroll=8)


def _dispatch(xc, pos, n_rows, tm):
    t, w = xc.shape
    grid_spec = pltpu.PrefetchScalarGridSpec(
        num_scalar_prefetch=1, grid=(t // tm,),
        in_specs=[pl.BlockSpec((tm, w), lambda i, pos: (i, 0)), pl.BlockSpec(memory_space=pl.ANY)],
        out_specs=pl.BlockSpec(memory_space=pl.ANY),
        scratch_shapes=[pltpu.SemaphoreType.DMA(())])
    return pl.pallas_call(
        _dispatch_kernel, grid_spec=grid_spec, out_shape=jax.ShapeDtypeStruct((n_rows, w), f32),
        input_output_aliases={2: 0}, compiler_params=_cp("arbitrary"), name="moe_dispatch")(
            pos, xc, jnp.zeros((n_rows, w), f32))


def _moe_sorted_kernel(tg_ref, xs_ref, wg_ref, wu_ref, wd_ref, ys_ref):
    del tg_ref
    d = ys_ref.shape[1]
    xb = xs_ref[:, :d].astype(bf16)
    cw = xs_ref[:, d:]
    acc = jnp.zeros(ys_ref.shape, f32)
    for e in range(EXPERTS_PER_GROUP):
        hg = jnp.dot(xb, wg_ref[0, e], preferred_element_type=f32)
        hu = jnp.dot(xb, wu_ref[0, e], preferred_element_type=f32)
        h = hg * _sigmoid(hg) * hu * cw[:, e:e + 1]
        acc = acc + jnp.dot(h.astype(bf16), wd_ref[0, e], preferred_element_type=f32)
    ys_ref[...] = acc


def _moe_sorted(xs, tile_group, wg, wu, wd, tm):
    n_rows, w = xs.shape
    e, d, hid = wg.shape[1], wg.shape[2], wg.shape[3]
    grp4 = lambda i, tg: (tg[i], 0, 0, 0)
    grid_spec = pltpu.PrefetchScalarGridSpec(
        num_scalar_prefetch=1, grid=(n_rows // tm,),
        in_specs=[pl.BlockSpec((tm, w), lambda i, tg: (i, 0)), pl.BlockSpec((1, e, d, hid), grp4),
                  pl.BlockSpec((1, e, d, hid), grp4), pl.BlockSpec((1, e, hid, d), grp4)],
        out_specs=pl.BlockSpec((tm, d), lambda i, tg: (i, 0)))
    return pl.pallas_call(
        _moe_sorted_kernel, grid_spec=grid_spec, out_shape=jax.ShapeDtypeStruct((n_rows, d), f32),
        compiler_params=_cp("arbitrary"), name="moe_sorted")(tile_group, xs, wg, wu, wd)


def _combine_kernel(pos_ref, x_ref, ys_ref, g_ref, b_ref, o_ref, ybuf, sem):
    tm = x_ref.shape[0]
    i = pl.program_id(0)
    n = pl.num_programs(0)

    def fetch(step, slot):
        def start(r, c):
            _row_copy(ys_ref, pos_ref[step * tm + r], ybuf.at[slot], r, sem.at[slot]).start()
            return c
        lax.fori_loop(0, tm, start, 0, unroll=8)

    @pl.when(i == 0)
    def _():
        fetch(0, 0)

    @pl.when(i + 1 < n)
    def _():
        fetch(i + 1, (i + 1) % 2)

    slot = i % 2

    def wait(r, c):
        _row_copy(ys_ref, 0, ybuf.at[slot], 0, sem.at[slot]).wait()
        return c

    lax.fori_loop(0, tm, wait, 0, unroll=8)
    o_ref[...] = _layer_norm(ALPHA * x_ref[...] + ybuf[slot], g_ref[...], b_ref[...])


def _combine_ln(x, ys, pos, g, b, tm):
    t, d = x.shape
    fix = lambda i, pos: (0, 0)
    grid_spec = pltpu.PrefetchScalarGridSpec(
        num_scalar_prefetch=1, grid=(t // tm,),
        in_specs=[pl.BlockSpec((tm, d), lambda i, pos: (i, 0)), pl.BlockSpec(memory_space=pl.ANY),
                  pl.BlockSpec((1, d), fix), pl.BlockSpec((1, d), fix)],
        out_specs=pl.BlockSpec((tm, d), lambda i, pos: (i, 0)),
        scratch_shapes=[pltpu.VMEM((2, tm, d), f32), pltpu.SemaphoreType.DMA((2,))])
    return pl.pallas_call(
        _combine_kernel, grid_spec=grid_spec, out_shape=jax.ShapeDtypeStruct((t, d), f32),
        compiler_params=_cp("arbitrary"), name="moe_combine_ln")(pos, x, ys, g.reshape(1, d), b.reshape(1, d))


def _moe_ln_sorted(x, whi, wlo, rbias, wg, wu, wd, g, b, tm=512):
    t, d = x.shape
    xc, cnt = _router_sort(x, whi, wlo, rbias, tm)
    grp = xc[:, d + META_GROUP].astype(i32)
    rank = xc[:, d + META_RANK].astype(i32)
    counts = cnt[0, :N_GROUPS].astype(i32)
    padded = (counts + tm - 1) // tm * tm
    ends = jnp.cumsum(padded)
    start = ends - padded
    pos = rank + sum(jnp.where(grp == gi, start[gi], 0) for gi in range(N_GROUPS))
    n_rows = t + N_GROUPS * tm
    tile_start = jnp.arange(n_rows // tm, dtype=i32) * tm
    tile_group = jnp.minimum(sum((tile_start >= ends[gi]).astype(i32) for gi in range(N_GROUPS)), N_GROUPS - 1)
    xs = _dispatch(xc, pos, n_rows, tm)
    ys = _moe_sorted(xs, tile_group, wg, wu, wd, tm)
    return _combine_ln(x, ys, pos, g, b, tm)


def _router_weights(w_rc, b_rc, w_re, b_re):
    d = w_rc.shape[0]
    ne = N_GROUPS * EXPERTS_PER_GROUP
    w = jnp.concatenate([w_re.reshape(d, ne), w_rc, jnp.zeros((d, LANES - ne - N_GROUPS), f32)], axis=1)
    bias = jnp.concatenate([b_re.reshape(ne), b_rc, jnp.zeros((LANES - ne - N_GROUPS,), f32)]).reshape(1, LANES)
    whi = w.astype(bf16)
    wlo = (w - whi.astype(f32)).astype(bf16)
    return whi, wlo, bias


def _token_major(kv_t):
    b, _, n = kv_t.shape
    return kv_t.reshape(b, 2, KV_HEADS, HEAD_DIM, n).transpose(0, 4, 1, 2, 3)


def kernel(x_prompt, x_sample, cache_cmp_kv, cache_sel_kv, cache_win_kv, state_hgrn, page_table, attn_w_in, attn_cmp_alpha, attn_w_out, rec_w_in, rec_lb_logits, rec_norm_g, rec_w_out, ln_g, ln_b, moe_w_router_c, moe_b_router_c, moe_w_router_e, moe_b_router_e, moe_w_gate, moe_w_up, moe_w_down):
    b, seq, d = x_prompt.shape
    bs, lq, _ = x_sample.shape
    tp, ts = b * seq, bs * lq
    lb_all = jnp.cumsum(jax.nn.softmax(rec_lb_logits.astype(f32), axis=0), axis=0)
    hp, hs = x_prompt.reshape(tp, d), x_sample.reshape(ts, d)
    cache_cmp_t = cache_cmp_kv.transpose(0, 1, 3, 4, 5, 2)
    cache_sel_t = cache_sel_kv.transpose(0, 1, 3, 4, 5, 2)
    kv_out = [[] for _ in range(6)]
    rec_out = [[], []]
    for layer in range(DEPTH):
        j = layer // 2
        if layer % 2 == 0:
            nq, n_in = N_HEADS * HEAD_DIM, attn_w_in.shape[2]
            w_all = attn_w_in[j]
            w_in = jnp.pad(w_all, ((0, 0), (0, -(-n_in // LANES) * LANES - n_in))).astype(bf16)
            w_qg = jnp.concatenate([w_all[:, :nq], w_all[:, nq + 6 * KV_WIDTH:]], axis=1)
            w_qg = jnp.pad(w_qg, ((0, 0), (0, -(-w_qg.shape[1] // LANES) * LANES - w_qg.shape[1]))).astype(bf16)
            w_kv_t = w_all[:, nq:nq + 6 * KV_WIDTH].T.astype(bf16)
            alpha_rows = jnp.repeat(attn_cmp_alpha[j], HEAD_DIM, axis=-1).reshape(CMP_BLOCK, 2 * KV_WIDTH).T
            w_out = attn_w_out[j].astype(bf16)
            op, c_p, s_p, w_p = _nsa_prompt(hp.reshape(b, seq, d), w_qg, w_kv_t, alpha_rows)
            os_, c_s, s_s, w_s = _nsa_sample(hs.reshape(bs, lq, d), cache_cmp_t, cache_sel_t, cache_win_kv,
                                             page_table, j, w_in, alpha_rows)
            kvs = (bs, lq, 2, KV_HEADS, HEAD_DIM)
            n_win = min(WINDOW, seq)
            for lst, val in zip(kv_out, (_token_major(c_p), c_s.reshape(kvs), _token_major(s_p), s_s.reshape(kvs),
                                         _token_major(w_p[:, :, seq - n_win:]),
                                         w_s.reshape(bs, -1, 2, KV_HEADS, HEAD_DIM))):
                lst.append(val)
        else:
            lb = lb_all[layer] - lb_all[0]
            w_in = rec_w_in[j].astype(bf16)
            w_out = rec_w_out[j].astype(bf16)
            op, st_p = _hgrn_mixer(hp.reshape(b, seq, d), jnp.zeros((b, HG_HEADS, HG_DK, HG_DV), f32), w_in, lb,
                                   rec_norm_g[j], 256)
            os_, st_s = _hgrn_mixer(hs.reshape(bs, lq, d), state_hgrn[j], w_in, lb, rec_norm_g[j], 256)
            rec_out[0].append(st_p)
            rec_out[1].append(st_s)
        hp = _proj_ln(op, w_out, hp, ln_g[layer, 0], ln_b[layer, 0], 512, "out_proj_ln")
        hs = _proj_ln(os_, w_out, hs, ln_g[layer, 0], ln_b[layer, 0], ts, "out_proj_ln_s")
        whi, wlo, rbias = _router_weights(moe_w_router_c[layer], moe_b_router_c[layer], moe_w_router_e[layer],
                                          moe_b_router_e[layer])
        wg, wu, wd = moe_w_gate[layer].astype(bf16), moe_w_up[layer].astype(bf16), moe_w_down[layer].astype(bf16)
        hp = _moe_ln_sorted(hp, whi, wlo, rbias, wg, wu, wd, ln_g[layer, 1], ln_b[layer, 1])
        hs = _moe_ln(hs, _router(hs, whi, wlo, rbias, ts), wg, wu, wd, ln_g[layer, 1], ln_b[layer, 1], ts)
    return (hp.reshape(b, seq, d), hs.reshape(bs, lq, d), *[jnp.stack(v) for v in kv_out],
            jnp.stack(rec_out[0]), jnp.stack(rec_out[1]))
```

```python
import functools

import jax
import jax.numpy as jnp
from jax import lax
from jax.experimental import pallas as pl
from jax.experimental.pallas import tpu as pltpu

f32, bf16, i32 = jnp.float32, jnp.bfloat16, jnp.int32

DEPTH = 2
N_HEADS, KV_HEADS, HEAD_DIM = 16, 4, 64
GROUP = N_HEADS // KV_HEADS
KV_WIDTH = KV_HEADS * HEAD_DIM
CMP_BLOCK, TOP_K, WINDOW = 64, 16, 512
FORCED_SCORE = float(GROUP + 1)
PAGE_SIZE = 128
HG_HEADS, HG_DK, HG_DV, HG_CHUNK = 8, 128, 128, 32
N_GROUPS, EXPERTS_PER_GROUP = 4, 8
ALPHA = (2.0 * DEPTH) ** 0.25
LN_EPS, RMS_EPS = 1e-5, 1e-6

NEG = -1e30
LOG2E = 1.4426950408889634
CMP_SHIFT = CMP_BLOCK.bit_length() - 1
assert 1 << CMP_SHIFT == CMP_BLOCK
LANES = 128
ONES_ROWS = 16
HG_HB = 8
VMEM_LIMIT = 56 * 1024 * 1024


def _cp(*sem):
    return pltpu.CompilerParams(dimension_semantics=sem, vmem_limit_bytes=VMEM_LIMIT)


def _sigmoid(x):
    return 1.0 / (1.0 + jnp.exp(-x))


def _log_sigmoid(x):
    return jnp.minimum(x, 0.0) - jnp.log1p(jnp.exp(-jnp.abs(x)))


def _layer_norm(z, g, b):
    mu = jnp.mean(z, axis=-1, keepdims=True)
    d = z - mu
    var = jnp.mean(d * d, axis=-1, keepdims=True)
    return d * lax.rsqrt(var + LN_EPS) * g + b


def _mm_kernel(x_ref, w_ref, o_ref):
    o_ref[...] = jnp.dot(x_ref[...].astype(bf16), w_ref[...], preferred_element_type=f32).astype(o_ref.dtype)


def _matmul(x, w, tm, name):
    m, k = x.shape
    n = w.shape[1]
    return pl.pallas_call(
        _mm_kernel, grid=(m // tm,),
        in_specs=[pl.BlockSpec((tm, k), lambda i: (i, 0)), pl.BlockSpec((k, n), lambda i: (0, 0))],
        out_specs=pl.BlockSpec((tm, n), lambda i: (i, 0)),
        out_shape=jax.ShapeDtypeStruct((m, n), f32), compiler_params=_cp("arbitrary"), name=name)(x, w)


def _proj_ln_kernel(o_ref, w_ref, x_ref, g_ref, b_ref, h_ref):
    y = jnp.dot(o_ref[...].astype(bf16), w_ref[...], preferred_element_type=f32)
    h_ref[...] = _layer_norm(ALPHA * x_ref[...] + y, g_ref[...], b_ref[...])


def _proj_ln(o, w, x, g, b, tm, name):
    m, k = o.shape
    n = w.shape[1]
    row = lambda i: (i, 0)
    fix = lambda i: (0, 0)
    return pl.pallas_call(
        _proj_ln_kernel, grid=(m // tm,),
        in_specs=[pl.BlockSpec((tm, k), row), pl.BlockSpec((k, n), fix), pl.BlockSpec((tm, n), row),
                  pl.BlockSpec((1, n), fix), pl.BlockSpec((1, n), fix)],
        out_specs=pl.BlockSpec((tm, n), row),
        out_shape=jax.ShapeDtypeStruct((m, n), f32), compiler_params=_cp("arbitrary"), name=name)(
            o, w, x, g.reshape(1, n), b.reshape(1, n))


def _split3(x):
    hi = x.astype(bf16)
    r1 = x - hi.astype(f32)
    mid = r1.astype(bf16)
    return hi, mid, (r1 - mid.astype(f32)).astype(bf16)


def _dot_nt(a, b):
    return lax.dot_general(a, b, (((1,), (1,)), ((), ())), preferred_element_type=f32)


def _block_sum_lanes(y, onehot):
    hi = y.astype(bf16)
    return _dot_nt(hi, onehot) + _dot_nt((y - hi.astype(f32)).astype(bf16), onehot)


def _compress_kernel(x_ref, a_ref, e_ref, o_ref):
    o_ref[0, 0] = _block_sum_lanes(x_ref[0, 0] * a_ref[0], e_ref[...])


def _compress_t(kvt, alpha_l, onehot):
    b, ch, hd, seq = kvt.shape
    nbp = onehot.shape[0]
    return pl.pallas_call(
        _compress_kernel, grid=(b, ch),
        in_specs=[pl.BlockSpec((1, 1, hd, seq), lambda i, j: (i, j, 0, 0)),
                  pl.BlockSpec((1, 1, seq), lambda i, j: (j, 0, 0)),
                  pl.BlockSpec((nbp, seq), lambda i, j: (0, 0))],
        out_specs=pl.BlockSpec((1, 1, hd, nbp), lambda i, j: (i, j, 0, 0)),
        out_shape=jax.ShapeDtypeStruct((b, ch, hd, nbp), f32),
        compiler_params=_cp("arbitrary", "arbitrary"), name="nsa_compress")(kvt, alpha_l, onehot)


def _compress_paged_kernel(pt_ref, *refs, n_pg):
    del pt_ref
    pages, a_ref, e_ref, o_ref = refs[:n_pg], refs[n_pg], refs[n_pg + 1], refs[n_pg + 2]
    w = 2 * KV_WIDTH
    x = jnp.concatenate([r[0, 0].reshape(w, PAGE_SIZE) for r in pages], axis=1)
    o_ref[0, 0] = _block_sum_lanes(x * a_ref[...], e_ref[...])


def _compress_paged(cache_t, layer, page_table, alpha_rows, n_pg):
    nb_, n_pages = page_table.shape
    bpp = PAGE_SIZE // CMP_BLOCK
    n_tok = n_pg * PAGE_SIZE
    a_t = jnp.tile(alpha_rows, (1, n_pg * bpp))
    onehot = (jnp.arange(n_pg * bpp)[:, None] == jnp.arange(n_tok)[None, :] // CMP_BLOCK).astype(bf16)

    def page_map(b, i, pt, k):
        return (layer, pt[b, i * n_pg + k], 0, 0, 0, 0)

    fix = lambda b, i, pt: (0, 0)
    grid_spec = pltpu.PrefetchScalarGridSpec(
        num_scalar_prefetch=1, grid=(nb_, n_pages // n_pg),
        in_specs=[pl.BlockSpec((1, 1, 2, KV_HEADS, HEAD_DIM, PAGE_SIZE), functools.partial(page_map, k=k))
                  for k in range(n_pg)]
        + [pl.BlockSpec(a_t.shape, fix), pl.BlockSpec(onehot.shape, fix)],
        out_specs=pl.BlockSpec((1, 1, 2 * KV_WIDTH, n_pg * bpp), lambda b, i, pt: (b, i, 0, 0)))
    return pl.pallas_call(
        functools.partial(_compress_paged_kernel, n_pg=n_pg), grid_spec=grid_spec,
        out_shape=jax.ShapeDtypeStruct((nb_, n_pages // n_pg, 2 * KV_WIDTH, n_pg * bpp), f32),
        compiler_params=_cp("arbitrary", "arbitrary"), name="nsa_compress_paged")(
            page_table, *([cache_t] * n_pg), a_t, onehot)


def _select_bias(score, k_sel):
    rowf = lax.broadcasted_iota(i32, score.shape, 0).astype(f32)
    picked = jnp.zeros(score.shape, f32)
    work = score
    for _ in range(k_sel):
        mx = jnp.max(work, axis=0, keepdims=True)
        first = jnp.min(jnp.where(work == mx, rowf, 1e9), axis=0, keepdims=True)
        pick = rowf == first
        picked = jnp.where(pick, 1.0, picked)
        work = jnp.where(pick, -jnp.inf, work)
    return jnp.where(picked > 0.0, 0.0, NEG)


def _masked_softmax_cols(s, mask, exp_fn):
    sm = jnp.where(mask, s, NEG)
    m = jnp.max(sm, axis=0, keepdims=True)
    e = jnp.where(mask, exp_fn(sm - m), 0.0)
    return e / jnp.maximum(jnp.sum(e, axis=0, keepdims=True), 1e-30)


def _nsa_in_proj_kernel(x_ref, w_ref, q_ref, g_ref, c_ref, s_ref, w_out_ref, *, q_scale):
    pt = _dot_nt(w_ref[...], x_ref[0].astype(bf16))
    nq, w = N_HEADS * HEAD_DIM, 2 * KV_WIDTH
    q_ref[0] = (pt[:nq] * q_scale).astype(q_ref.dtype)
    for n, ref in enumerate((c_ref, s_ref, w_out_ref)):
        ref[0] = pt[nq + n * w:nq + (n + 1) * w]
    g_ref[0] = pt[nq + 3 * w:nq + 3 * w + 3 * N_HEADS]


def _nsa_in_proj(x, w_t, tm, q_scale):
    b, seq, d = x.shape
    nq, w, ng = N_HEADS * HEAD_DIM, 2 * KV_WIDTH, 3 * N_HEADS
    rows = lambda n: pl.BlockSpec((1, n, tm), lambda i, j: (i, 0, j))
    kv_shape = jax.ShapeDtypeStruct((b, w, seq), f32)
    return pl.pallas_call(
        functools.partial(_nsa_in_proj_kernel, q_scale=q_scale), grid=(b, seq // tm),
        in_specs=[pl.BlockSpec((1, tm, d), lambda i, j: (i, j, 0)), pl.BlockSpec(w_t.shape, lambda i, j: (0, 0))],
        out_specs=[rows(nq), rows(ng), rows(w), rows(w), rows(w)],
        out_shape=[jax.ShapeDtypeStruct((b, nq, seq), bf16), jax.ShapeDtypeStruct((b, ng, seq), f32),
                   kv_shape, kv_shape, kv_shape],
        compiler_params=_cp("arbitrary", "arbitrary"), name="nsa_in_proj")(x, w_t)


def _nsa_prompt_kernel(qt_ref, g_ref, kct_ref, vct_ref, kst_ref, vst_ref, kwt_ref, vwt_ref, et_ref, o_ref,
                       m_ref, acc_ref, sa_ref, sb_ref, ka_ref, va_ref, kw_ref, vwa_ref, *, tq, tk, nb):
    s0 = pl.program_id(1) * tq
    c = GROUP * tq
    nbp = et_ref.shape[1]
    qt = jnp.concatenate([qt_ref[0, g * HEAD_DIM:(g + 1) * HEAD_DIM, :] for g in range(GROUP)], axis=1)
    tq_pos = s0 + lax.broadcasted_iota(i32, (1, tq), 1)
    tpos = jnp.concatenate([tq_pos] * GROUP, axis=1)

    kc = kct_ref[0, 0, 0].T.astype(bf16)
    sc = jnp.dot(kc, qt, preferred_element_type=f32)
    blk_c = lax.broadcasted_iota(i32, (nbp, c), 0)
    pc = _masked_softmax_cols(sc, blk_c * CMP_BLOCK + (CMP_BLOCK - 1) <= tpos, jnp.exp2)
    oc = jnp.dot(vct_ref[0, 0, 0].astype(bf16), pc.astype(bf16), preferred_element_type=f32)

    imp = pc[:, 0:tq]
    for g in range(1, GROUP):
        imp = imp + pc[:, g * tq:(g + 1) * tq]
    blk = lax.broadcasted_iota(i32, (nbp, tq), 0)
    cur = jnp.right_shift(tq_pos, CMP_SHIFT)
    forced = (blk == 0) | (blk == cur) | (blk == cur - 1)
    base = jnp.where(forced, FORCED_SCORE, jnp.where(blk * CMP_BLOCK <= tq_pos, 0.0, -1.0))
    sel = _select_bias(jnp.where(blk < nb, base + imp, -3.0), min(TOP_K, nb)).astype(bf16)
    qa = jnp.concatenate([jnp.concatenate([sel] * GROUP, axis=1), qt], axis=0)

    m_ref[...] = jnp.full(m_ref.shape, NEG, f32)
    acc_ref[...] = jnp.zeros(acc_ref.shape, f32)

    @pl.when(pl.program_id(1) == 0)
    def _():
        ones_rows = jnp.ones((ONES_ROWS, tk), bf16)

        def fill(t, carry):
            k0 = pl.multiple_of(t * tk, tk)
            ka_ref[pl.ds(k0, tk), :] = jnp.concatenate(
                [et_ref[pl.ds(k0, tk), :], kst_ref[0, 0, 0, :, pl.ds(k0, tk)].T.astype(bf16)], axis=1)
            va_ref[:, pl.ds(k0, tk)] = jnp.concatenate([vst_ref[0, 0, 0, :, pl.ds(k0, tk)].astype(bf16), ones_rows], axis=0)
            kw_ref[pl.ds(k0, tk), :] = kwt_ref[0, 0, 0, :, pl.ds(k0, tk)].T.astype(bf16)
            vwa_ref[:, pl.ds(k0, tk)] = jnp.concatenate([vwt_ref[0, 0, 0, :, pl.ds(k0, tk)].astype(bf16), ones_rows], axis=0)
            return carry

        lax.fori_loop(0, ka_ref.shape[0] // tk, fill, 0)

    def produce(j, s_ref):
        k0 = pl.multiple_of(j * tk, tk)
        s_ref[...] = jnp.dot(ka_ref[pl.ds(k0, tk), :], qa, preferred_element_type=f32)

    def consume(j, s_ref, width, causal):
        k0 = pl.multiple_of(j * tk, tk)
        s = s_ref[0:width, :]
        if causal:
            tail = jnp.where(k0 + (width - tq) + lax.broadcasted_iota(i32, (tq, c), 0) <= tpos, s[width - tq:], NEG)
            s = tail if width == tq else jnp.concatenate([s[:width - tq], tail], axis=0)
        m_new = jnp.maximum(m_ref[...], jnp.max(s, axis=0, keepdims=True))
        a = jnp.exp2(m_ref[...] - m_new)
        p = jnp.exp2(s - m_new).astype(bf16)
        acc_ref[...] = a * acc_ref[...] + jnp.dot(va_ref[:, pl.ds(k0, width)], p, preferred_element_type=f32)
        m_ref[...] = m_new

    n_full = s0 // tk
    produce(0, sa_ref)

    def pair(i, carry):
        produce(2 * i + 1, sb_ref)
        consume(2 * i, sa_ref, tk, False)
        produce(2 * i + 2, sa_ref)
        consume(2 * i + 1, sb_ref, tk, False)
        return carry

    lax.fori_loop(0, n_full // 2, pair, 0)
    odd = n_full % 2 == 1

    @pl.when(odd)
    def _():
        produce(n_full, sb_ref)
        consume(n_full - 1, sa_ref, tk, False)

    for wi in range(1, tk // tq + 1):
        here = s0 - n_full * tk == (wi - 1) * tq

        @pl.when(here & odd)
        def _():
            consume(n_full, sb_ref, wi * tq, True)

        @pl.when(here & jnp.logical_not(odd))
        def _():
            consume(n_full, sa_ref, wi * tq, True)

    osel = acc_ref[0:HEAD_DIM] / jnp.maximum(acc_ref[HEAD_DIM:HEAD_DIM + 1], 1e-30)

    wt = WINDOW + tq
    k0w = pl.multiple_of(jnp.maximum(s0 - WINDOW, 0), LANES)
    s = jnp.dot(kw_ref[pl.ds(k0w, wt), :], qt, preferred_element_type=f32)
    dist = tpos - (k0w + lax.broadcasted_iota(i32, (wt, c), 0))
    s = jnp.where(dist >= 0, jnp.where(dist < WINDOW, s, NEG), NEG)
    p = jnp.exp2(s - jnp.max(s, axis=0, keepdims=True)).astype(bf16)
    ow = jnp.dot(vwa_ref[:, pl.ds(k0w, wt)], p, preferred_element_type=f32)
    ow = ow[0:HEAD_DIM] / jnp.maximum(ow[HEAD_DIM:HEAD_DIM + 1], 1e-30)

    gt = _sigmoid(g_ref[0, 0])
    outs = []
    for g in range(GROUP):
        cs = slice(g * tq, (g + 1) * tq)
        og = gt[3 * g:3 * g + 1] * oc[:, cs] + gt[3 * g + 1:3 * g + 2] * osel[:, cs] + gt[3 * g + 2:3 * g + 3] * ow[:, cs]
        outs.append(og.T)
    o_ref[0] = jnp.concatenate(outs, axis=1).astype(o_ref.dtype)


def _nsa_prompt_attention(qt, gt, cblk_t, kvs_t, kvw_t, onehot_t, tq, tk):
    b, nq, seq = qt.shape
    kvh, hd, c = KV_HEADS, HEAD_DIM, GROUP * tq
    nbp = onehot_t.shape[1]
    assert tk % tq == 0
    comp = lambda cc: (lambda bh, i: (bh // kvh, cc, bh % kvh, 0, 0))
    rows = lambda n: (pl.BlockSpec((1, 1, 1, hd, n), comp(0)), pl.BlockSpec((1, 1, 1, hd, n), comp(1)))
    return pl.pallas_call(
        functools.partial(_nsa_prompt_kernel, tq=tq, tk=tk, nb=seq // CMP_BLOCK), grid=(b * kvh, seq // tq),
        in_specs=[pl.BlockSpec((1, GROUP * hd, tq), lambda bh, i: (bh // kvh, bh % kvh, i)),
                  pl.BlockSpec((1, 1, GROUP * 3, tq), lambda bh, i: (bh // kvh, bh % kvh, 0, i)),
                  *rows(nbp), *rows(seq), *rows(seq), pl.BlockSpec((seq, nbp), lambda bh, i: (0, 0))],
        out_specs=pl.BlockSpec((1, tq, GROUP * hd), lambda bh, i: (bh // kvh, i, bh % kvh)),
        out_shape=jax.ShapeDtypeStruct((b, seq, nq), bf16),
        scratch_shapes=[pltpu.VMEM((1, c), f32), pltpu.VMEM((hd + ONES_ROWS, c), f32),
                        pltpu.VMEM((tk, c), f32), pltpu.VMEM((tk, c), f32),
                        pltpu.VMEM((seq, nbp + hd), bf16), pltpu.VMEM((hd + ONES_ROWS, seq), bf16),
                        pltpu.VMEM((seq, hd), bf16), pltpu.VMEM((hd + ONES_ROWS, seq), bf16)],
        compiler_params=_cp("arbitrary", "arbitrary"), name="nsa_prompt_attn")(
            qt, gt, cblk_t, cblk_t, kvs_t, kvs_t, kvw_t, kvw_t, onehot_t)


def _nsa_prompt(x, w_t, alpha_rows, tq=128, tk=512):
    b, seq, d = x.shape
    nb = seq // CMP_BLOCK
    qt, gt, kvc_t, kvs_t, kvw_t = _nsa_in_proj(x, w_t, 512, HEAD_DIM ** -0.5 * LOG2E)
    five = lambda a: a.reshape(b, 2, KV_HEADS, HEAD_DIM, -1)
    nbp = -(-nb // LANES) * LANES
    onehot = (jnp.arange(nbp)[:, None] == jnp.arange(seq)[None, :] // CMP_BLOCK).astype(bf16)
    alpha_l = jnp.tile(alpha_rows[::HEAD_DIM], (1, nb)).reshape(2 * KV_HEADS, 1, seq)
    cblk_t = _compress_t(kvc_t.reshape(b, 2 * KV_HEADS, HEAD_DIM, seq), alpha_l, onehot)
    o = _nsa_prompt_attention(qt, gt.reshape(b, KV_HEADS, GROUP * 3, seq), five(cblk_t), five(kvs_t), five(kvw_t),
                              onehot.T, tq, tk)
    return o.reshape(b * seq, d), kvc_t, kvs_t, kvw_t


def _dot_t0(a, b):
    return lax.dot_general(a, b, (((0,), (0,)), ((), ())), preferred_element_type=f32)


def _smp_cmp_win_kernel(q_ref, kc_ref, vc_ref, cn_ref, a_ref, kw_ref, vw_ref, oc_ref, ow_ref, sel_ref, *, past, lq,
                        nb_real):
    q = q_ref[0]
    nbp = kc_ref.shape[1]
    col = lax.broadcasted_iota(i32, (1, LANES), 1)
    tpos = past + jnp.bitwise_and(col, lq - 1)

    c_new = jnp.sum(cn_ref[0] * a_ref[...], axis=0, keepdims=True).astype(bf16)
    is_new = lax.broadcasted_iota(i32, (nbp, KV_WIDTH), 0) == nb_real - 1
    kc = jnp.where(is_new, c_new[:, :KV_WIDTH], kc_ref[0])
    vc = jnp.where(is_new, c_new[:, KV_WIDTH:], vc_ref[0])
    sc = jnp.dot(kc, q, preferred_element_type=f32)
    blk = lax.broadcasted_iota(i32, (nbp, LANES), 0)
    pc = _masked_softmax_cols(sc, blk * CMP_BLOCK + (CMP_BLOCK - 1) <= tpos, jnp.exp)
    oc_ref[0] = _dot_t0(pc.astype(bf16), vc)

    r = lax.broadcasted_iota(i32, (LANES, LANES), 0)
    cc = lax.broadcasted_iota(i32, (LANES, LANES), 1)
    per_kv = GROUP * lq
    same = jnp.where(r // per_kv == cc // per_kv, jnp.where(r % lq == cc % lq, 1.0, 0.0), 0.0).astype(bf16)
    hi, mid, lo = _split3(pc)
    imp = (jnp.dot(hi, same, preferred_element_type=f32) + jnp.dot(mid, same, preferred_element_type=f32)
           + jnp.dot(lo, same, preferred_element_type=f32))
    cur = jnp.right_shift(tpos, CMP_SHIFT)
    forced = (blk == 0) | (blk == cur) | (blk == cur - 1)
    base = jnp.where(forced, FORCED_SCORE, jnp.where(blk * CMP_BLOCK <= tpos, 0.0, -1.0))
    score = jnp.where(blk < nb_real, base + imp, -3.0)
    sel_ref[0] = _select_bias(score, min(TOP_K, nb_real))

    nwt = kw_ref.shape[1]
    nw = nwt - lq
    s = jnp.dot(kw_ref[0], q, preferred_element_type=f32)
    wpos = past - nw + lax.broadcasted_iota(i32, (nwt, LANES), 0)
    dist = tpos - wpos
    ok = (dist >= 0) & (dist < WINDOW) & (wpos >= 0)
    pw = _masked_softmax_cols(s, ok, jnp.exp)
    ow_ref[0] = _dot_t0(pw.astype(bf16), vw_ref[0])


def _smp_sel_kernel(pt_ref, *refs, n_pg):
    del pt_ref
    pages = refs[:n_pg]
    q_ref, sel_ref, e_ref, kn_ref, vn_ref, bn_ref, g_ref, oc_ref, ow_ref, o_ref, m_ref, l_ref, acc_ref = refs[n_pg:]
    i = pl.program_id(1)

    @pl.when(i == 0)
    def _():
        m_ref[...] = jnp.full(m_ref.shape, NEG, f32)
        l_ref[...] = jnp.zeros(l_ref.shape, f32)
        acc_ref[...] = jnp.zeros(acc_ref.shape, f32)

    def update(s, vt):
        m_new = jnp.maximum(m_ref[...], jnp.max(s, axis=1, keepdims=True))
        a = jnp.exp(m_ref[...] - m_new)
        p = jnp.exp(s - m_new)
        l_ref[...] = a * l_ref[...] + jnp.sum(p, axis=1, keepdims=True)
        acc_ref[...] = a * acc_ref[...] + _dot_nt(p.astype(bf16), vt)
        m_ref[...] = m_new

    qa = jnp.concatenate([q_ref[0], sel_ref[0, 0].astype(bf16)], axis=1)
    kt = jnp.concatenate([r[0, 0, 0].reshape(KV_WIDTH, PAGE_SIZE) for r in pages], axis=1).astype(bf16)
    vt = jnp.concatenate([r[0, 0, 1].reshape(KV_WIDTH, PAGE_SIZE) for r in pages], axis=1).astype(bf16)
    update(jnp.dot(qa, jnp.concatenate([kt, e_ref[...]], axis=0), preferred_element_type=f32), vt)

    @pl.when(i == pl.num_programs(1) - 1)
    def _():
        update(jnp.dot(q_ref[0], kn_ref[0].astype(bf16), preferred_element_type=f32) + bn_ref[0], vn_ref[0].astype(bf16))
        gt = _sigmoid(g_ref[0])
        osel = acc_ref[...] / jnp.maximum(l_ref[...], 1e-30)
        o_ref[0] = gt[:, 0:1] * oc_ref[0] + gt[:, 1:2] * osel + gt[:, 2:3] * ow_ref[0]


def _nsa_sample(x, cache_cmp_t, cache_sel_t, cache_win, page_table, layer, w_in, alpha_rows, n_pg=16):
    b, lq, d = x.shape
    n_pages = page_table.shape[1]
    past = n_pages * PAGE_SIZE
    assert KV_HEADS * GROUP * lq == LANES and past % CMP_BLOCK == 0 and lq <= CMP_BLOCK
    assert n_pages % n_pg == 0 and n_pg * PAGE_SIZE // CMP_BLOCK <= LANES
    p = _matmul(x.reshape(b * lq, d), w_in, b * lq, "nsa_in_proj_s")
    nq = N_HEADS * HEAD_DIM
    w2 = 2 * KV_WIDTH
    q = (p[:, :nq] * HEAD_DIM ** -0.5).reshape(b, lq, KV_HEADS, GROUP, HEAD_DIM)
    eye = jnp.eye(KV_HEADS, dtype=f32)
    qbd = (q.transpose(0, 2, 3, 1, 4)[:, :, :, :, None, :] * eye[None, :, None, None, :, None])
    qbd = qbd.reshape(b, LANES, KV_WIDTH).astype(bf16)
    qbd_t = qbd.transpose(0, 2, 1)
    gt = p[:, nq + 6 * KV_WIDTH:nq + 6 * KV_WIDTH + 3 * N_HEADS].reshape(b, lq, KV_HEADS, GROUP, 3)
    gt = gt.transpose(0, 2, 3, 1, 4).reshape(b, LANES, 3)
    kv = p[:, nq:nq + 6 * KV_WIDTH].reshape(b, lq, 3, w2)
    kv_c, kv_s, kv_w = kv[:, :, 0], kv[:, :, 1], kv[:, :, 2]

    bpp = PAGE_SIZE // CMP_BLOCK
    c_past = _compress_paged(cache_cmp_t, layer, page_table, alpha_rows, n_pg)
    c_past = c_past.transpose(0, 1, 3, 2).reshape(b, past // CMP_BLOCK, w2)
    nb_real = past // CMP_BLOCK + 1
    nbp = -(-nb_real // 8) * 8
    cblk = jnp.pad(c_past, ((0, 0), (0, nbp - nb_real + 1), (0, 0)))
    wkv = jnp.concatenate([cache_win[layer].reshape(b, -1, w2), kv_w], axis=1)
    nwt = wkv.shape[1]

    seq3 = lambda i: (i, 0, 0)
    oc, ow, sel = pl.pallas_call(
        functools.partial(_smp_cmp_win_kernel, past=past, lq=lq, nb_real=nb_real), grid=(b,),
        in_specs=[pl.BlockSpec((1, KV_WIDTH, LANES), seq3), pl.BlockSpec((1, nbp, KV_WIDTH), seq3),
                  pl.BlockSpec((1, nbp, KV_WIDTH), seq3), pl.BlockSpec((1, lq, w2), seq3),
                  pl.BlockSpec((lq, w2), lambda i: (0, 0)), pl.BlockSpec((1, nwt, KV_WIDTH), seq3),
                  pl.BlockSpec((1, nwt, KV_WIDTH), seq3)],
        out_specs=[pl.BlockSpec((1, LANES, KV_WIDTH), seq3), pl.BlockSpec((1, LANES, KV_WIDTH), seq3),
                   pl.BlockSpec((1, nbp, LANES), seq3)],
        out_shape=[jax.ShapeDtypeStruct((b, LANES, KV_WIDTH), f32), jax.ShapeDtypeStruct((b, LANES, KV_WIDTH), f32),
                   jax.ShapeDtypeStruct((b, nbp, LANES), f32)],
        compiler_params=_cp("arbitrary"), name="nsa_sample_cmp_win")(
            qbd_t, cblk[:, :, :KV_WIDTH].astype(bf16), cblk[:, :, KV_WIDTH:].astype(bf16), kv_c, alpha_rows.T[:lq],
            wkv[:, :, :KV_WIDTH].astype(bf16), wkv[:, :, KV_WIDTH:].astype(bf16))

    n_steps, bps = n_pages // n_pg, n_pg * bpp
    sel_r = sel.transpose(0, 2, 1)
    sel_steps = sel_r[:, :, :past // CMP_BLOCK].reshape(b, LANES, n_steps, bps).transpose(0, 2, 1, 3)
    sel_steps = jnp.pad(sel_steps, ((0, 0), (0, 0), (0, 0), (0, LANES - bps)))
    onehot = (jnp.arange(LANES)[:, None] == jnp.arange(n_pg * PAGE_SIZE)[None, :] // CMP_BLOCK).astype(bf16)
    pad_t = lambda a: jnp.pad(a.transpose(0, 2, 1), ((0, 0), (0, 0), (0, LANES - lq)))
    qoff = jnp.arange(LANES) % lq
    readable = jnp.arange(LANES)[None, :] <= qoff[:, None]
    bias_new = jnp.where(readable[None], sel_r[:, :, past // CMP_BLOCK][:, :, None], NEG)

    def page_map(bb, i, pt, k):
        return (layer, pt[bb, i * n_pg + k], 0, 0, 0, 0)

    seq3p = lambda bb, i, pt: (bb, 0, 0)
    grid_spec = pltpu.PrefetchScalarGridSpec(
        num_scalar_prefetch=1, grid=(b, n_steps),
        in_specs=[pl.BlockSpec((1, 1, 2, KV_HEADS, HEAD_DIM, PAGE_SIZE), functools.partial(page_map, k=k))
                  for k in range(n_pg)]
        + [pl.BlockSpec((1, LANES, KV_WIDTH), seq3p), pl.BlockSpec((1, 1, LANES, LANES), lambda bb, i, pt: (bb, i, 0, 0)),
           pl.BlockSpec(onehot.shape, lambda bb, i, pt: (0, 0)),
           pl.BlockSpec((1, KV_WIDTH, LANES), seq3p), pl.BlockSpec((1, KV_WIDTH, LANES), seq3p),
           pl.BlockSpec((1, LANES, LANES), seq3p), pl.BlockSpec((1, LANES, 3), seq3p),
           pl.BlockSpec((1, LANES, KV_WIDTH), seq3p), pl.BlockSpec((1, LANES, KV_WIDTH), seq3p)],
        out_specs=pl.BlockSpec((1, LANES, KV_WIDTH), seq3p),
        scratch_shapes=[pltpu.VMEM((LANES, 1), f32), pltpu.VMEM((LANES, 1), f32), pltpu.VMEM((LANES, KV_WIDTH), f32)])
    o = pl.pallas_call(
        functools.partial(_smp_sel_kernel, n_pg=n_pg), grid_spec=grid_spec,
        out_shape=jax.ShapeDtypeStruct((b, LANES, KV_WIDTH), f32),
        compiler_params=_cp("arbitrary", "arbitrary"), name="nsa_sample_sel")(
            page_table, *([cache_sel_t] * n_pg), qbd, sel_steps, onehot, pad_t(kv_s[:, :, :KV_WIDTH]),
            pad_t(kv_s[:, :, KV_WIDTH:]), bias_new, gt, oc, ow)
    o = o.reshape(b, KV_HEADS, GROUP, lq, KV_HEADS, HEAD_DIM)
    o = jnp.stack([o[:, h, :, :, h] for h in range(KV_HEADS)], axis=1)
    o = o.transpose(0, 3, 1, 2, 4).reshape(b * lq, d)
    return o, kv_c, kv_s, wkv[:, lq:]


def _hgrn_kernel(zq_ref, zf_ref, zi_ref, zg_ref, lb_ref, ng_ref, s0_ref, o_ref, sout_ref, st_ref, *, n_valid, hb):
    t = pl.program_id(2)
    tb = zq_ref.shape[1]
    n_ch = tb // HG_CHUNK

    @pl.when(t == 0)
    def _():
        for hh in range(hb):
            st_ref[hh] = s0_ref[0, hh].T

    r = lax.broadcasted_iota(i32, (tb, tb), 0)
    cc = lax.broadcasted_iota(i32, (tb, tb), 1)
    tri = jnp.where(r // HG_CHUNK == cc // HG_CHUNK, jnp.where(cc <= r, 1.0, 0.0), 0.0).astype(bf16)
    causal = lax.broadcasted_iota(i32, (HG_CHUNK, HG_CHUNK), 1) <= lax.broadcasted_iota(i32, (HG_CHUNK, HG_CHUNK), 0)
    mid_row = HG_CHUNK // 2

    zf = zf_ref[0]
    lb = lb_ref[0]
    la = jnp.log(lb)
    l1 = jnp.log1p(-lb)
    bb = l1 + _log_sigmoid(zf)
    log_f = jnp.maximum(la, bb) + jnp.log1p(jnp.exp(-jnp.abs(la - bb)))
    kk = jnp.exp(l1 + _log_sigmoid(-zf))
    if n_valid < tb:
        live = lax.broadcasted_iota(i32, zf.shape, 0) < n_valid
        log_f = jnp.where(live, log_f, 0.0)
        kk = jnp.where(live, kk, 0.0)
    hi, mid, lo = _split3(log_f)
    cum = (jnp.dot(tri, hi, preferred_element_type=f32) + jnp.dot(tri, mid, preferred_element_type=f32)
           + jnp.dot(tri, lo, preferred_element_type=f32))
    zq, zi, zg = zq_ref[0], zi_ref[0], zg_ref[0]
    gate = ng_ref[0] * (zg * _sigmoid(zg))

    qs, o_in, upd, dec = {}, {}, {}, {}
    for hh in range(hb):
        hs = slice(hh * HG_DK, (hh + 1) * HG_DK)
        for ci in range(n_ch):
            sl = slice(ci * HG_CHUNK, (ci + 1) * HG_CHUNK)
            cm, qi, ki, vi = cum[sl, hs], zq[sl, hs], kk[sl, hs], zi[sl, hs]
            ref = cm[mid_row:mid_row + 1]
            last = cm[HG_CHUNK - 1:HG_CHUNK]
            a = _dot_nt((qi * jnp.exp(cm - ref)).astype(bf16), (ki * jnp.exp(ref - cm)).astype(bf16))
            a = jnp.where(causal, a, 0.0)
            o_in[hh, ci] = jnp.dot(a.astype(bf16), vi.astype(bf16), preferred_element_type=f32)
            qs[hh, ci] = (qi * jnp.exp(cm)).astype(bf16)
            upd[hh, ci] = _dot_t0(vi.astype(bf16), (ki * jnp.exp(last - cm)).astype(bf16))
            dec[hh, ci] = jnp.exp(last)
    for hh in range(hb):
        hs = slice(hh * HG_DK, (hh + 1) * HG_DK)
        st = st_ref[hh]
        outs = []
        for ci in range(n_ch):
            outs.append(o_in[hh, ci] + _dot_nt(qs[hh, ci], st.astype(bf16)))
            st = st * dec[hh, ci] + upd[hh, ci]
        st_ref[hh] = st
        o = jnp.concatenate(outs, axis=0) if n_ch > 1 else outs[0]
        o = o * lax.rsqrt(jnp.mean(o * o, axis=-1, keepdims=True) + RMS_EPS)
        o_ref[0, :, hs] = (o * gate[:, hs]).astype(o_ref.dtype)

        @pl.when(t == pl.num_programs(2) - 1)
        def _():
            sout_ref[0, hh] = st.T


def _hgrn(p, s0, lb, norm_g, tb, n_valid, hb):
    b, lp, _ = p.shape
    ng = HG_HEADS // hb
    w = hb * HG_DK
    col = lambda off: (lambda bb, hh, t: (bb, t, off * ng + hh))
    per_head = lambda bb, hh, t: (0, hh)
    state = lambda bb, hh, t: (bb, hh, 0, 0)
    return pl.pallas_call(
        functools.partial(_hgrn_kernel, n_valid=n_valid, hb=hb), grid=(b, ng, lp // tb),
        in_specs=[pl.BlockSpec((1, tb, w), col(0)), pl.BlockSpec((1, tb, w), col(1)),
                  pl.BlockSpec((1, tb, w), col(2)), pl.BlockSpec((1, tb, w), col(3)),
                  pl.BlockSpec((1, w), per_head), pl.BlockSpec((1, w), per_head),
                  pl.BlockSpec((1, hb, HG_DK, HG_DV), state)],
        out_specs=[pl.BlockSpec((1, tb, w), lambda bb, hh, t: (bb, t, hh)),
                   pl.BlockSpec((1, hb, HG_DK, HG_DV), state)],
        out_shape=[jax.ShapeDtypeStruct((b, lp, HG_HEADS * HG_DV), bf16),
                   jax.ShapeDtypeStruct((b, HG_HEADS, HG_DK, HG_DV), f32)],
        scratch_shapes=[pltpu.VMEM((hb, HG_DV, HG_DK), f32)],
        compiler_params=_cp("arbitrary", "arbitrary", "arbitrary"), name="hgrn_scan")(
            p, p, p, p, lb.reshape(1, -1), norm_g.reshape(1, -1), s0)


def _hgrn_mixer(x, s0, w_in, lb, norm_g, tb):
    b, seq, d = x.shape
    lp = -(-seq // HG_CHUNK) * HG_CHUNK
    tb = min(tb, lp)
    p = _matmul(x.reshape(b * seq, d), w_in, min(256, b * seq), "hgrn_in_proj").reshape(b, seq, 4 * d)
    if lp != seq:
        p = jnp.pad(p, ((0, 0), (0, lp - seq), (0, 0)))
    o, s_out = _hgrn(p, s0, lb, norm_g, tb, seq if lp != seq else lp, HG_HB)
    return o[:, :seq].reshape(b * seq, d), s_out


def _router_logits(x, whi_ref, wlo_ref, b_ref):
    xh = x.astype(bf16)
    xl = (x - xh.astype(f32)).astype(bf16)
    return (jnp.dot(xh, whi_ref[...], preferred_element_type=f32) + jnp.dot(xl, whi_ref[...], preferred_element_type=f32)
            + jnp.dot(xh, wlo_ref[...], preferred_element_type=f32)) + b_ref[...]


def _route(lg):
    ne = N_GROUPS * EXPERTS_PER_GROUP
    lane = lax.broadcasted_iota(i32, lg.shape, 1)
    lanef = lane.astype(f32)
    in_c = (lane >= ne) & (lane < ne + N_GROUPS)
    lc = jnp.where(in_c, lg, -jnp.inf)
    mc = jnp.max(lc, axis=-1, keepdims=True)
    grp = jnp.min(jnp.where(lc == mc, lanef, 1e9), axis=-1, keepdims=True) - float(ne)
    pg = 1.0 / jnp.sum(jnp.where(in_c, jnp.exp(lc - mc), 0.0), axis=-1, keepdims=True)
    e_lo = grp * float(EXPERTS_PER_GROUP)
    in_e = (lanef >= e_lo) & (lanef < e_lo + float(EXPERTS_PER_GROUP))
    le = jnp.where(in_e, lg, -jnp.inf)
    v1 = jnp.max(le, axis=-1, keepdims=True)
    i1 = jnp.min(jnp.where(le == v1, lanef, 1e9), axis=-1, keepdims=True)
    le2 = jnp.where(lanef == i1, -jnp.inf, le)
    v2 = jnp.max(le2, axis=-1, keepdims=True)
    i2 = jnp.min(jnp.where(le2 == v2, lanef, 1e9), axis=-1, keepdims=True)
    e2 = jnp.exp(v2 - v1)
    w1 = pg / (1.0 + e2)
    return jnp.where(lanef == i1, w1, 0.0) + jnp.where(lanef == i2, w1 * e2, 0.0), grp


def _router_kernel(x_ref, whi_ref, wlo_ref, b_ref, cw_ref):
    cw, _ = _route(_router_logits(x_ref[...], whi_ref, wlo_ref, b_ref))
    for g in range(N_GROUPS):
        cw_ref[g] = cw if g == 0 else pltpu.roll(cw, LANES - g * EXPERTS_PER_GROUP, axis=1)


def _router(x, whi, wlo, bias, tm):
    t, d = x.shape
    return pl.pallas_call(
        _router_kernel, grid=(t // tm,),
        in_specs=[pl.BlockSpec((tm, d), lambda i: (i, 0)), pl.BlockSpec((d, LANES), lambda i: (0, 0)),
                  pl.BlockSpec((d, LANES), lambda i: (0, 0)), pl.BlockSpec((1, LANES), lambda i: (0, 0))],
        out_specs=pl.BlockSpec((N_GROUPS, tm, LANES), lambda i: (0, i, 0)),
        out_shape=jax.ShapeDtypeStruct((N_GROUPS, t, LANES), f32), compiler_params=_cp("arbitrary"),
        name="moe_router")(x, whi, wlo, bias)


def _moe_kernel(x_ref, cw_ref, wg_ref, wu_ref, wd_ref, g_ref, b_ref, o_ref, acc_ref):
    gi = pl.program_id(1)

    @pl.when(gi == 0)
    def _():
        acc_ref[...] = jnp.zeros(acc_ref.shape, f32)

    xb = x_ref[...].astype(bf16)
    cw = cw_ref[0]
    acc = acc_ref[...]
    for e in range(EXPERTS_PER_GROUP):
        hg = jnp.dot(xb, wg_ref[0, e], preferred_element_type=f32)
        hu = jnp.dot(xb, wu_ref[0, e], preferred_element_type=f32)
        h = hg * _sigmoid(hg) * hu * cw[:, e:e + 1]
        acc = acc + jnp.dot(h.astype(bf16), wd_ref[0, e], preferred_element_type=f32)
    acc_ref[...] = acc

    @pl.when(gi == pl.num_programs(1) - 1)
    def _():
        o_ref[...] = _layer_norm(ALPHA * x_ref[...] + acc_ref[...], g_ref[...], b_ref[...])


def _moe_ln(x, cw, wg, wu, wd, g, b, tm):
    t, d = x.shape
    e, hid = wg.shape[1], wg.shape[3]
    row = lambda i, gi: (i, 0)
    fix = lambda i, gi: (0, 0)
    return pl.pallas_call(
        _moe_kernel, grid=(t // tm, N_GROUPS),
        in_specs=[pl.BlockSpec((tm, d), row), pl.BlockSpec((1, tm, LANES), lambda i, gi: (gi, i, 0)),
                  pl.BlockSpec((1, e, d, hid), lambda i, gi: (gi, 0, 0, 0)),
                  pl.BlockSpec((1, e, d, hid), lambda i, gi: (gi, 0, 0, 0)),
                  pl.BlockSpec((1, e, hid, d), lambda i, gi: (gi, 0, 0, 0)),
                  pl.BlockSpec((1, d), fix), pl.BlockSpec((1, d), fix)],
        out_specs=pl.BlockSpec((tm, d), row),
        out_shape=jax.ShapeDtypeStruct((t, d), f32),
        scratch_shapes=[pltpu.VMEM((tm, d), f32)],
        compiler_params=_cp("arbitrary", "arbitrary"), name="moe_dense")(
            x, cw, wg, wu, wd, g.reshape(1, d), b.reshape(1, d))


META_GROUP, META_RANK = EXPERTS_PER_GROUP, EXPERTS_PER_GROUP + 1


def _router_sort_kernel(x_ref, whi_ref, wlo_ref, b_ref, xc_ref, cnt_ref, carry_ref):
    @pl.when(pl.program_id(0) == 0)
    def _():
        carry_ref[...] = jnp.zeros(carry_ref.shape, f32)

    x = x_ref[...]
    tm, d = x.shape
    cw, grp = _route(_router_logits(x, whi_ref, wlo_ref, b_ref))
    cw8 = cw
    for g in range(1, N_GROUPS):
        cw8 = jnp.where(grp == float(g), pltpu.roll(cw, LANES - g * EXPERTS_PER_GROUP, axis=1), cw8)
    lanef = lax.broadcasted_iota(i32, (tm, LANES), 1).astype(f32)
    onehot = jnp.where(lanef == grp, 1.0, 0.0)
    before = lax.broadcasted_iota(i32, (tm, tm), 1) < lax.broadcasted_iota(i32, (tm, tm), 0)
    prefix = jnp.dot(before.astype(bf16), onehot.astype(bf16), preferred_element_type=f32) + carry_ref[...]
    rank = jnp.sum(onehot * prefix, axis=-1, keepdims=True)
    carry_ref[...] = carry_ref[...] + jnp.sum(onehot, axis=0, keepdims=True)
    cnt_ref[...] = carry_ref[...]
    xc_ref[:, :d] = x
    xc_ref[:, d:] = cw8 + jnp.where(lanef == float(META_GROUP), grp, 0.0) + jnp.where(lanef == float(META_RANK), rank, 0.0)


def _router_sort(x, whi, wlo, bias, tm):
    t, d = x.shape
    fix = lambda i: (0, 0)
    return pl.pallas_call(
        _router_sort_kernel, grid=(t // tm,),
        in_specs=[pl.BlockSpec((tm, d), lambda i: (i, 0)), pl.BlockSpec((d, LANES), fix), pl.BlockSpec((d, LANES), fix),
                  pl.BlockSpec((1, LANES), fix)],
        out_specs=[pl.BlockSpec((tm, d + LANES), lambda i: (i, 0)), pl.BlockSpec((1, LANES), fix)],
        out_shape=[jax.ShapeDtypeStruct((t, d + LANES), f32), jax.ShapeDtypeStruct((1, LANES), f32)],
        scratch_shapes=[pltpu.VMEM((1, LANES), f32)],
        compiler_params=_cp("arbitrary"), name="moe_router_sort")(x, whi, wlo, bias)


def _row_copy(src, i, dst, j, sem):
    return pltpu.make_async_copy(src.at[pl.ds(i, 1)], dst.at[pl.ds(j, 1)], sem)


def _dispatch_kernel(pos_ref, xc_ref, zero_ref, xs_ref, sem):
    del zero_ref
    tm = xc_ref.shape[0]
    base = pl.program_id(0) * tm

    def start(r, c):
        _row_copy(xc_ref, r, xs_ref, pos_ref[base + r], sem).start()
        return c

    def wait(r, c):
        _row_copy(xc_ref, 0, xs_ref, 0, sem).wait()
        return c

    lax.fori_loop(0, tm, start, 0, unroll=8)
    lax.fori_loop(0, tm, wait, 0, unroll=8)


def _dispatch(xc, pos, n_rows, tm):
    t, w = xc.shape
    grid_spec = pltpu.PrefetchScalarGridSpec(
        num_scalar_prefetch=1, grid=(t // tm,),
        in_specs=[pl.BlockSpec((tm, w), lambda i, pos: (i, 0)), pl.BlockSpec(memory_space=pl.ANY)],
        out_specs=pl.BlockSpec(memory_space=pl.ANY),
        scratch_shapes=[pltpu.SemaphoreType.DMA(())])
    return pl.pallas_call(
        _dispatch_kernel, grid_spec=grid_spec, out_shape=jax.ShapeDtypeStruct((n_rows, w), f32),
        input_output_aliases={2: 0}, compiler_params=_cp("arbitrary"), name="moe_dispatch")(
            pos, xc, jnp.zeros((n_rows, w), f32))


def _moe_sorted_kernel(tg_ref, xs_ref, wg_ref, wu_ref, wd_ref, ys_ref):
    del tg_ref
    d = ys_ref.shape[1]
    xb = xs_ref[:, :d].astype(bf16)
    cw = xs_ref[:, d:]
    acc = jnp.zeros(ys_ref.shape, f32)
    for e in range(EXPERTS_PER_GROUP):
        hg = jnp.dot(xb, wg_ref[0, e], preferred_element_type=f32)
        hu = jnp.dot(xb, wu_ref[0, e], preferred_element_type=f32)
        h = hg * _sigmoid(hg) * hu * cw[:, e:e + 1]
        acc = acc + jnp.dot(h.astype(bf16), wd_ref[0, e], preferred_element_type=f32)
    ys_ref[...] = acc


def _moe_sorted(xs, tile_group, wg, wu, wd, tm):
    n_rows, w = xs.shape
    e, d, hid = wg.shape[1], wg.shape[2], wg.shape[3]
    grp4 = lambda i, tg: (tg[i], 0, 0, 0)
    grid_spec = pltpu.PrefetchScalarGridSpec(
        num_scalar_prefetch=1, grid=(n_rows // tm,),
        in_specs=[pl.BlockSpec((tm, w), lambda i, tg: (i, 0)), pl.BlockSpec((1, e, d, hid), grp4),
                  pl.BlockSpec((1, e, d, hid), grp4), pl.BlockSpec((1, e, hid, d), grp4)],
        out_specs=pl.BlockSpec((tm, d), lambda i, tg: (i, 0)))
    return pl.pallas_call(
        _moe_sorted_kernel, grid_spec=grid_spec, out_shape=jax.ShapeDtypeStruct((n_rows, d), f32),
        compiler_params=_cp("arbitrary"), name="moe_sorted")(tile_group, xs, wg, wu, wd)


def _combine_kernel(pos_ref, x_ref, ys_ref, g_ref, b_ref, o_ref, ybuf, sem):
    tm = x_ref.shape[0]
    i = pl.program_id(0)
    n = pl.num_programs(0)

    def fetch(step, slot):
        def start(r, c):
            _row_copy(ys_ref, pos_ref[step * tm + r], ybuf.at[slot], r, sem.at[slot]).start()
            return c
        lax.fori_loop(0, tm, start, 0, unroll=8)

    @pl.when(i == 0)
    def _():
        fetch(0, 0)

    @pl.when(i + 1 < n)
    def _():
        fetch(i + 1, (i + 1) % 2)

    slot = i % 2

    def wait(r, c):
        _row_copy(ys_ref, 0, ybuf.at[slot], 0, sem.at[slot]).wait()
        return c

    lax.fori_loop(0, tm, wait, 0, unroll=8)
    o_ref[...] = _layer_norm(ALPHA * x_ref[...] + ybuf[slot], g_ref[...], b_ref[...])


def _combine_ln(x, ys, pos, g, b, tm):
    t, d = x.shape
    fix = lambda i, pos: (0, 0)
    grid_spec = pltpu.PrefetchScalarGridSpec(
        num_scalar_prefetch=1, grid=(t // tm,),
        in_specs=[pl.BlockSpec((tm, d), lambda i, pos: (i, 0)), pl.BlockSpec(memory_space=pl.ANY),
                  pl.BlockSpec((1, d), fix), pl.BlockSpec((1, d), fix)],
        out_specs=pl.BlockSpec((tm, d), lambda i, pos: (i, 0)),
        scratch_shapes=[pltpu.VMEM((2, tm, d), f32), pltpu.SemaphoreType.DMA((2,))])
    return pl.pallas_call(
        _combine_kernel, grid_spec=grid_spec, out_shape=jax.ShapeDtypeStruct((t, d), f32),
        compiler_params=_cp("arbitrary"), name="moe_combine_ln")(pos, x, ys, g.reshape(1, d), b.reshape(1, d))


def _moe_ln_sorted(x, whi, wlo, rbias, wg, wu, wd, g, b, tm=512):
    t, d = x.shape
    xc, cnt = _router_sort(x, whi, wlo, rbias, tm)
    grp = xc[:, d + META_GROUP].astype(i32)
    rank = xc[:, d + META_RANK].astype(i32)
    counts = cnt[0, :N_GROUPS].astype(i32)
    padded = (counts + tm - 1) // tm * tm
    ends = jnp.cumsum(padded)
    start = ends - padded
    pos = rank + sum(jnp.where(grp == gi, start[gi], 0) for gi in range(N_GROUPS))
    n_rows = t + N_GROUPS * tm
    tile_start = jnp.arange(n_rows // tm, dtype=i32) * tm
    tile_group = jnp.minimum(sum((tile_start >= ends[gi]).astype(i32) for gi in range(N_GROUPS)), N_GROUPS - 1)
    xs = _dispatch(xc, pos, n_rows, tm)
    ys = _moe_sorted(xs, tile_group, wg, wu, wd, tm)
    return _combine_ln(x, ys, pos, g, b, tm)


def _router_weights(w_rc, b_rc, w_re, b_re):
    d = w_rc.shape[0]
    ne = N_GROUPS * EXPERTS_PER_GROUP
    w = jnp.concatenate([w_re.reshape(d, ne), w_rc, jnp.zeros((d, LANES - ne - N_GROUPS), f32)], axis=1)
    bias = jnp.concatenate([b_re.reshape(ne), b_rc, jnp.zeros((LANES - ne - N_GROUPS,), f32)]).reshape(1, LANES)
    whi = w.astype(bf16)
    wlo = (w - whi.astype(f32)).astype(bf16)
    return whi, wlo, bias


def _token_major(kv_t):
    b, _, n = kv_t.shape
    return kv_t.reshape(b, 2, KV_HEADS, HEAD_DIM, n).transpose(0, 4, 1, 2, 3)


def kernel(x_prompt, x_sample, cache_cmp_kv, cache_sel_kv, cache_win_kv, state_hgrn, page_table, attn_w_in, attn_cmp_alpha, attn_w_out, rec_w_in, rec_lb_logits, rec_norm_g, rec_w_out, ln_g, ln_b, moe_w_router_c, moe_b_router_c, moe_w_router_e, moe_b_router_e, moe_w_gate, moe_w_up, moe_w_down):
    b, seq, d = x_prompt.shape
    bs, lq, _ = x_sample.shape
    tp, ts = b * seq, bs * lq
    lb_all = jnp.cumsum(jax.nn.softmax(rec_lb_logits.astype(f32), axis=0), axis=0)
    hp, hs = x_prompt.reshape(tp, d), x_sample.reshape(ts, d)
    cache_cmp_t = cache_cmp_kv.transpose(0, 1, 3, 4, 5, 2)
    cache_sel_t = cache_sel_kv.transpose(0, 1, 3, 4, 5, 2)
    kv_out = [[] for _ in range(6)]
    rec_out = [[], []]
    for layer in range(DEPTH):
        j = layer // 2
        if layer % 2 == 0:
            nq, n_in = N_HEADS * HEAD_DIM, attn_w_in.shape[2]
            w_all = attn_w_in[j]
            w_in = jnp.pad(w_all, ((0, 0), (0, -(-n_in // LANES) * LANES - n_in))).astype(bf16)
            w_t = w_all.T.astype(bf16)
            alpha_rows = jnp.repeat(attn_cmp_alpha[j], HEAD_DIM, axis=-1).reshape(CMP_BLOCK, 2 * KV_WIDTH).T
            w_out = attn_w_out[j].astype(bf16)
            op, c_p, s_p, w_p = _nsa_prompt(hp.reshape(b, seq, d), w_t, alpha_rows)
            os_, c_s, s_s, w_s = _nsa_sample(hs.reshape(bs, lq, d), cache_cmp_t, cache_sel_t, cache_win_kv,
                                             page_table, j, w_in, alpha_rows)
            kvs = (bs, lq, 2, KV_HEADS, HEAD_DIM)
            n_win = min(WINDOW, seq)
            for lst, val in zip(kv_out, (_token_major(c_p), c_s.reshape(kvs), _token_major(s_p), s_s.reshape(kvs),
                                         _token_major(w_p[:, :, seq - n_win:]),
                                         w_s.reshape(bs, -1, 2, KV_HEADS, HEAD_DIM))):
                lst.append(val)
        else:
            lb = lb_all[layer] - lb_all[0]
            w_in = rec_w_in[j].astype(bf16)
            w_out = rec_w_out[j].astype(bf16)
            op, st_p = _hgrn_mixer(hp.reshape(b, seq, d), jnp.zeros((b, HG_HEADS, HG_DK, HG_DV), f32), w_in, lb,
                                   rec_norm_g[j], 256)
            os_, st_s = _hgrn_mixer(hs.reshape(bs, lq, d), state_hgrn[j], w_in, lb, rec_norm_g[j], 256)
            rec_out[0].append(st_p)
            rec_out[1].append(st_s)
        hp = _proj_ln(op, w_out, hp, ln_g[layer, 0], ln_b[layer, 0], 512, "out_proj_ln")
        hs = _proj_ln(os_, w_out, hs, ln_g[layer, 0], ln_b[layer, 0], ts, "out_proj_ln_s")
        whi, wlo, rbias = _router_weights(moe_w_router_c[layer], moe_b_router_c[layer], moe_w_router_e[layer],
                                          moe_b_router_e[layer])
        wg, wu, wd = moe_w_gate[layer].astype(bf16), moe_w_up[layer].astype(bf16), moe_w_down[layer].astype(bf16)
        hp = _moe_ln_sorted(hp, whi, wlo, rbias, wg, wu, wd, ln_g[layer, 1], ln_b[layer, 1])
        hs = _moe_ln(hs, _router(hs, whi, wlo, rbias, ts), wg, wu, wd, ln_g[layer, 1], ln_b[layer, 1], ts)
    return (hp.reshape(b, seq, d), hs.reshape(bs, lq, d), *[jnp.stack(v) for v in kv_out],
            jnp.stack(rec_out[0]), jnp.stack(rec_out[1]))
```

```python
import functools

import jax
import jax.numpy as jnp
from jax import lax
from jax.experimental import pallas as pl
from jax.experimental.pallas import tpu as pltpu

f32, bf16, i32 = jnp.float32, jnp.bfloat16, jnp.int32

DEPTH = 2
N_HEADS, KV_HEADS, HEAD_DIM = 16, 4, 64
GROUP = N_HEADS // KV_HEADS
KV_WIDTH = KV_HEADS * HEAD_DIM
CMP_BLOCK, TOP_K, WINDOW = 64, 16, 512
FORCED_SCORE = float(GROUP + 1)
PAGE_SIZE = 128
HG_HEADS, HG_DK, HG_DV, HG_CHUNK = 8, 128, 128, 32
N_GROUPS, EXPERTS_PER_GROUP = 4, 8
ALPHA = (2.0 * DEPTH) ** 0.25
LN_EPS, RMS_EPS = 1e-5, 1e-6

NEG = -1e30
LOG2E = 1.4426950408889634
CMP_SHIFT = CMP_BLOCK.bit_length() - 1
assert 1 << CMP_SHIFT == CMP_BLOCK
LANES = 128
ONES_ROWS = 16
HG_HB = 8
VMEM_LIMIT = 56 * 1024 * 1024


def _cp(*sem):
    return pltpu.CompilerParams(dimension_semantics=sem, vmem_limit_bytes=VMEM_LIMIT)


def _sigmoid(x):
    return 1.0 / (1.0 + jnp.exp(-x))


def _log_sigmoid(x):
    return jnp.minimum(x, 0.0) - jnp.log1p(jnp.exp(-jnp.abs(x)))


def _layer_norm(z, g, b):
    mu = jnp.mean(z, axis=-1, keepdims=True)
    d = z - mu
    var = jnp.mean(d * d, axis=-1, keepdims=True)
    return d * lax.rsqrt(var + LN_EPS) * g + b


def _mm_kernel(x_ref, w_ref, o_ref):
    o_ref[...] = jnp.dot(x_ref[...].astype(bf16), w_ref[...], preferred_element_type=f32).astype(o_ref.dtype)


def _matmul(x, w, tm, name):
    m, k = x.shape
    n = w.shape[1]
    return pl.pallas_call(
        _mm_kernel, grid=(m // tm,),
        in_specs=[pl.BlockSpec((tm, k), lambda i: (i, 0)), pl.BlockSpec((k, n), lambda i: (0, 0))],
        out_specs=pl.BlockSpec((tm, n), lambda i: (i, 0)),
        out_shape=jax.ShapeDtypeStruct((m, n), f32), compiler_params=_cp("arbitrary"), name=name)(x, w)


def _proj_ln_kernel(o_ref, w_ref, x_ref, g_ref, b_ref, h_ref):
    y = jnp.dot(o_ref[...].astype(bf16), w_ref[...], preferred_element_type=f32)
    h_ref[...] = _layer_norm(ALPHA * x_ref[...] + y, g_ref[...], b_ref[...])


def _proj_ln(o, w, x, g, b, tm, name):
    m, k = o.shape
    n = w.shape[1]
    row = lambda i: (i, 0)
    fix = lambda i: (0, 0)
    return pl.pallas_call(
        _proj_ln_kernel, grid=(m // tm,),
        in_specs=[pl.BlockSpec((tm, k), row), pl.BlockSpec((k, n), fix), pl.BlockSpec((tm, n), row),
                  pl.BlockSpec((1, n), fix), pl.BlockSpec((1, n), fix)],
        out_specs=pl.BlockSpec((tm, n), row),
        out_shape=jax.ShapeDtypeStruct((m, n), f32), compiler_params=_cp("arbitrary"), name=name)(
            o, w, x, g.reshape(1, n), b.reshape(1, n))


def _split3(x):
    hi = x.astype(bf16)
    r1 = x - hi.astype(f32)
    mid = r1.astype(bf16)
    return hi, mid, (r1 - mid.astype(f32)).astype(bf16)


def _dot_nt(a, b):
    return lax.dot_general(a, b, (((1,), (1,)), ((), ())), preferred_element_type=f32)


def _block_sum_lanes(y, onehot):
    hi = y.astype(bf16)
    return _dot_nt(hi, onehot) + _dot_nt((y - hi.astype(f32)).astype(bf16), onehot)


def _compress_kernel(x_ref, a_ref, e_ref, o_ref):
    o_ref[0, 0] = _block_sum_lanes(x_ref[0, 0] * a_ref[0], e_ref[...])


def _compress_t(kvt, alpha_l, onehot):
    b, ch, hd, seq = kvt.shape
    nbp = onehot.shape[0]
    return pl.pallas_call(
        _compress_kernel, grid=(b, ch),
        in_specs=[pl.BlockSpec((1, 1, hd, seq), lambda i, j: (i, j, 0, 0)),
                  pl.BlockSpec((1, 1, seq), lambda i, j: (j, 0, 0)),
                  pl.BlockSpec((nbp, seq), lambda i, j: (0, 0))],
        out_specs=pl.BlockSpec((1, 1, hd, nbp), lambda i, j: (i, j, 0, 0)),
        out_shape=jax.ShapeDtypeStruct((b, ch, hd, nbp), f32),
        compiler_params=_cp("arbitrary", "arbitrary"), name="nsa_compress")(kvt, alpha_l, onehot)


def _compress_paged_kernel(pt_ref, *refs, n_pg):
    del pt_ref
    pages, a_ref, e_ref, o_ref = refs[:n_pg], refs[n_pg], refs[n_pg + 1], refs[n_pg + 2]
    w = 2 * KV_WIDTH
    x = jnp.concatenate([r[0, 0].reshape(w, PAGE_SIZE) for r in pages], axis=1)
    o_ref[0, 0] = _block_sum_lanes(x * a_ref[...], e_ref[...])


def _compress_paged(cache_t, layer, page_table, alpha_rows, n_pg):
    nb_, n_pages = page_table.shape
    bpp = PAGE_SIZE // CMP_BLOCK
    n_tok = n_pg * PAGE_SIZE
    a_t = jnp.tile(alpha_rows, (1, n_pg * bpp))
    onehot = (jnp.arange(n_pg * bpp)[:, None] == jnp.arange(n_tok)[None, :] // CMP_BLOCK).astype(bf16)

    def page_map(b, i, pt, k):
        return (layer, pt[b, i * n_pg + k], 0, 0, 0, 0)

    fix = lambda b, i, pt: (0, 0)
    grid_spec = pltpu.PrefetchScalarGridSpec(
        num_scalar_prefetch=1, grid=(nb_, n_pages // n_pg),
        in_specs=[pl.BlockSpec((1, 1, 2, KV_HEADS, HEAD_DIM, PAGE_SIZE), functools.partial(page_map, k=k))
                  for k in range(n_pg)]
        + [pl.BlockSpec(a_t.shape, fix), pl.BlockSpec(onehot.shape, fix)],
        out_specs=pl.BlockSpec((1, 1, 2 * KV_WIDTH, n_pg * bpp), lambda b, i, pt: (b, i, 0, 0)))
    return pl.pallas_call(
        functools.partial(_compress_paged_kernel, n_pg=n_pg), grid_spec=grid_spec,
        out_shape=jax.ShapeDtypeStruct((nb_, n_pages // n_pg, 2 * KV_WIDTH, n_pg * bpp), f32),
        compiler_params=_cp("arbitrary", "arbitrary"), name="nsa_compress_paged")(
            page_table, *([cache_t] * n_pg), a_t, onehot)


def _select_bias(score, k_sel):
    rowf = lax.broadcasted_iota(i32, score.shape, 0).astype(f32)
    picked = jnp.zeros(score.shape, f32)
    work = score
    for _ in range(k_sel):
        mx = jnp.max(work, axis=0, keepdims=True)
        first = jnp.min(jnp.where(work == mx, rowf, 1e9), axis=0, keepdims=True)
        pick = rowf == first
        picked = jnp.where(pick, 1.0, picked)
        work = jnp.where(pick, -jnp.inf, work)
    return jnp.where(picked > 0.0, 0.0, NEG)


def _masked_softmax_cols(s, mask, exp_fn):
    sm = jnp.where(mask, s, NEG)
    m = jnp.max(sm, axis=0, keepdims=True)
    e = jnp.where(mask, exp_fn(sm - m), 0.0)
    return e / jnp.maximum(jnp.sum(e, axis=0, keepdims=True), 1e-30)


def _nsa_in_proj_kernel(x_ref, w_ref, q_ref, g_ref, c_ref, s_ref, w_out_ref, *, q_scale):
    pt = _dot_nt(w_ref[...], x_ref[0].astype(bf16))
    nq, w = N_HEADS * HEAD_DIM, 2 * KV_WIDTH
    q_ref[0] = (pt[:nq] * q_scale).astype(q_ref.dtype)
    for n, ref in enumerate((c_ref, s_ref, w_out_ref)):
        ref[0] = pt[nq + n * w:nq + (n + 1) * w]
    g_ref[0] = pt[nq + 3 * w:nq + 3 * w + 3 * N_HEADS]


def _nsa_in_proj(x, w_t, tm, q_scale):
    b, seq, d = x.shape
    nq, w, ng = N_HEADS * HEAD_DIM, 2 * KV_WIDTH, 3 * N_HEADS
    rows = lambda n: pl.BlockSpec((1, n, tm), lambda i, j: (i, 0, j))
    kv_shape = jax.ShapeDtypeStruct((b, w, seq), f32)
    return pl.pallas_call(
        functools.partial(_nsa_in_proj_kernel, q_scale=q_scale), grid=(b, seq // tm),
        in_specs=[pl.BlockSpec((1, tm, d), lambda i, j: (i, j, 0)), pl.BlockSpec(w_t.shape, lambda i, j: (0, 0))],
        out_specs=[rows(nq), rows(ng), rows(w), rows(w), rows(w)],
        out_shape=[jax.ShapeDtypeStruct((b, nq, seq), bf16), jax.ShapeDtypeStruct((b, ng, seq), f32),
                   kv_shape, kv_shape, kv_shape],
        compiler_params=_cp("arbitrary", "arbitrary"), name="nsa_in_proj")(x, w_t)


def _nsa_prompt_kernel(qt_ref, g_ref, kct_ref, vct_ref, kst_ref, vst_ref, kwt_ref, vwt_ref, et_ref, o_ref,
                       m_ref, acc_ref, sa_ref, sb_ref, ka_ref, va_ref, kw_ref, vwa_ref, *, tq, tk, nb):
    s0 = pl.program_id(1) * tq
    c = GROUP * tq
    nbp = et_ref.shape[1]
    qt = jnp.concatenate([qt_ref[0, g * HEAD_DIM:(g + 1) * HEAD_DIM, :] for g in range(GROUP)], axis=1)
    tq_pos = s0 + lax.broadcasted_iota(i32, (1, tq), 1)
    tpos = jnp.concatenate([tq_pos] * GROUP, axis=1)

    kc = kct_ref[0, 0, 0].T.astype(bf16)
    sc = jnp.dot(kc, qt, preferred_element_type=f32)
    blk_c = lax.broadcasted_iota(i32, (nbp, c), 0)
    pc = _masked_softmax_cols(sc, blk_c * CMP_BLOCK + (CMP_BLOCK - 1) <= tpos, jnp.exp2)
    oc = jnp.dot(vct_ref[0, 0, 0].astype(bf16), pc.astype(bf16), preferred_element_type=f32)

    imp = pc[:, 0:tq]
    for g in range(1, GROUP):
        imp = imp + pc[:, g * tq:(g + 1) * tq]
    blk = lax.broadcasted_iota(i32, (nbp, tq), 0)
    cur = jnp.right_shift(tq_pos, CMP_SHIFT)
    forced = (blk == 0) | (blk == cur) | (blk == cur - 1)
    base = jnp.where(forced, FORCED_SCORE, jnp.where(blk * CMP_BLOCK <= tq_pos, 0.0, -1.0))
    sel = _select_bias(jnp.where(blk < nb, base + imp, -3.0), min(TOP_K, nb)).astype(bf16)
    qa = jnp.concatenate([jnp.concatenate([sel] * GROUP, axis=1), qt], axis=0)

    m_ref[...] = jnp.full(m_ref.shape, NEG, f32)
    acc_ref[...] = jnp.zeros(acc_ref.shape, f32)

    @pl.when(pl.program_id(1) == 0)
    def _():
        ones_rows = jnp.ones((ONES_ROWS, tk), bf16)

        def fill(t, carry):
            k0 = pl.multiple_of(t * tk, tk)
            ka_ref[pl.ds(k0, tk), :] = jnp.concatenate(
                [et_ref[pl.ds(k0, tk), :], kst_ref[0, 0, 0, :, pl.ds(k0, tk)].T.astype(bf16)], axis=1)
            va_ref[:, pl.ds(k0, tk)] = jnp.concatenate([vst_ref[0, 0, 0, :, pl.ds(k0, tk)].astype(bf16), ones_rows], axis=0)
            kw_ref[pl.ds(k0, tk), :] = kwt_ref[0, 0, 0, :, pl.ds(k0, tk)].T.astype(bf16)
            vwa_ref[:, pl.ds(k0, tk)] = jnp.concatenate([vwt_ref[0, 0, 0, :, pl.ds(k0, tk)].astype(bf16), ones_rows], axis=0)
            return carry

        lax.fori_loop(0, ka_ref.shape[0] // tk, fill, 0)

    def produce(j, s_ref):
        k0 = pl.multiple_of(j * tk, tk)
        s_ref[...] = jnp.dot(ka_ref[pl.ds(k0, tk), :], qa, preferred_element_type=f32)

    def consume(j, s_ref, width, causal):
        k0 = pl.multiple_of(j * tk, tk)
        s = s_ref[0:width, :]
        if causal:
            tail = jnp.where(k0 + (width - tq) + lax.broadcasted_iota(i32, (tq, c), 0) <= tpos, s[width - tq:], NEG)
            s = tail if width == tq else jnp.concatenate([s[:width - tq], tail], axis=0)
        m_new = jnp.maximum(m_ref[...], jnp.max(s, axis=0, keepdims=True))
        a = jnp.exp2(m_ref[...] - m_new)
        p = jnp.exp2(s - m_new).astype(bf16)
        acc_ref[...] = a * acc_ref[...] + jnp.dot(va_ref[:, pl.ds(k0, width)], p, preferred_element_type=f32)
        m_ref[...] = m_new

    n_full = s0 // tk
    produce(0, sa_ref)

    def pair(i, carry):
        produce(2 * i + 1, sb_ref)
        consume(2 * i, sa_ref, tk, False)
        produce(2 * i + 2, sa_ref)
        consume(2 * i + 1, sb_ref, tk, False)
        return carry

    lax.fori_loop(0, n_full // 2, pair, 0)
    odd = n_full % 2 == 1

    @pl.when(odd)
    def _():
        produce(n_full, sb_ref)
        consume(n_full - 1, sa_ref, tk, False)

    for wi in range(1, tk // tq + 1):
        here = s0 - n_full * tk == (wi - 1) * tq

        @pl.when(here & odd)
        def _():
            consume(n_full, sb_ref, wi * tq, True)

        @pl.when(here & jnp.logical_not(odd))
        def _():
            consume(n_full, sa_ref, wi * tq, True)

    osel = acc_ref[0:HEAD_DIM] / jnp.maximum(acc_ref[HEAD_DIM:HEAD_DIM + 1], 1e-30)

    wt = WINDOW + tq
    k0w = pl.multiple_of(jnp.maximum(s0 - WINDOW, 0), LANES)
    s = jnp.dot(kw_ref[pl.ds(k0w, wt), :], qt, preferred_element_type=f32)
    dist = tpos - (k0w + lax.broadcasted_iota(i32, (wt, c), 0))
    s = jnp.where(dist >= 0, jnp.where(dist < WINDOW, s, NEG), NEG)
    p = jnp.exp2(s - jnp.max(s, axis=0, keepdims=True)).astype(bf16)
    ow = jnp.dot(vwa_ref[:, pl.ds(k0w, wt)], p, preferred_element_type=f32)
    ow = ow[0:HEAD_DIM] / jnp.maximum(ow[HEAD_DIM:HEAD_DIM + 1], 1e-30)

    gt = _sigmoid(g_ref[0, 0])
    outs = []
    for g in range(GROUP):
        cs = slice(g * tq, (g + 1) * tq)
        og = gt[3 * g:3 * g + 1] * oc[:, cs] + gt[3 * g + 1:3 * g + 2] * osel[:, cs] + gt[3 * g + 2:3 * g + 3] * ow[:, cs]
        outs.append(og.T)
    o_ref[0] = jnp.concatenate(outs, axis=1).astype(o_ref.dtype)


def _nsa_prompt_attention(qt, gt, cblk_t, kvs_t, kvw_t, onehot_t, tq, tk):
    b, nq, seq = qt.shape
    kvh, hd, c = KV_HEADS, HEAD_DIM, GROUP * tq
    nbp = onehot_t.shape[1]
    assert tk % tq == 0
    comp = lambda cc: (lambda bh, i: (bh // kvh, cc, bh % kvh, 0, 0))
    rows = lambda n: (pl.BlockSpec((1, 1, 1, hd, n), comp(0)), pl.BlockSpec((1, 1, 1, hd, n), comp(1)))
    return pl.pallas_call(
        functools.partial(_nsa_prompt_kernel, tq=tq, tk=tk, nb=seq // CMP_BLOCK), grid=(b * kvh, seq // tq),
        in_specs=[pl.BlockSpec((1, GROUP * hd, tq), lambda bh, i: (bh // kvh, bh % kvh, i)),
                  pl.BlockSpec((1, 1, GROUP * 3, tq), lambda bh, i: (bh // kvh, bh % kvh, 0, i)),
                  *rows(nbp), *rows(seq), *rows(seq), pl.BlockSpec((seq, nbp), lambda bh, i: (0, 0))],
        out_specs=pl.BlockSpec((1, tq, GROUP * hd), lambda bh, i: (bh // kvh, i, bh % kvh)),
        out_shape=jax.ShapeDtypeStruct((b, seq, nq), bf16),
        scratch_shapes=[pltpu.VMEM((1, c), f32), pltpu.VMEM((hd + ONES_ROWS, c), f32),
                        pltpu.VMEM((tk, c), f32), pltpu.VMEM((tk, c), f32),
                        pltpu.VMEM((seq, nbp + hd), bf16), pltpu.VMEM((hd + ONES_ROWS, seq), bf16),
                        pltpu.VMEM((seq, hd), bf16), pltpu.VMEM((hd + ONES_ROWS, seq), bf16)],
        compiler_params=_cp("arbitrary", "arbitrary"), name="nsa_prompt_attn")(
            qt, gt, cblk_t, cblk_t, kvs_t, kvs_t, kvw_t, kvw_t, onehot_t)


def _nsa_prompt(x, w_t, alpha_rows, tq=256, tk=512):
    b, seq, d = x.shape
    nb = seq // CMP_BLOCK
    qt, gt, kvc_t, kvs_t, kvw_t = _nsa_in_proj(x, w_t, 512, HEAD_DIM ** -0.5 * LOG2E)
    five = lambda a: a.reshape(b, 2, KV_HEADS, HEAD_DIM, -1)
    nbp = -(-nb // LANES) * LANES
    onehot = (jnp.arange(nbp)[:, None] == jnp.arange(seq)[None, :] // CMP_BLOCK).astype(bf16)
    alpha_l = jnp.tile(alpha_rows[::HEAD_DIM], (1, nb)).reshape(2 * KV_HEADS, 1, seq)
    cblk_t = _compress_t(kvc_t.reshape(b, 2 * KV_HEADS, HEAD_DIM, seq), alpha_l, onehot)
    o = _nsa_prompt_attention(qt, gt.reshape(b, KV_HEADS, GROUP * 3, seq), five(cblk_t), five(kvs_t), five(kvw_t),
                              onehot.T, tq, tk)
    return o.reshape(b * seq, d), kvc_t, kvs_t, kvw_t


def _dot_t0(a, b):
    return lax.dot_general(a, b, (((0,), (0,)), ((), ())), preferred_element_type=f32)


def _smp_cmp_win_kernel(q_ref, kc_ref, vc_ref, cn_ref, a_ref, kw_ref, vw_ref, oc_ref, ow_ref, sel_ref, *, past, lq,
                        nb_real):
    q = q_ref[0]
    nbp = kc_ref.shape[1]
    col = lax.broadcasted_iota(i32, (1, LANES), 1)
    tpos = past + jnp.bitwise_and(col, lq - 1)

    c_new = jnp.sum(cn_ref[0] * a_ref[...], axis=0, keepdims=True).astype(bf16)
    is_new = lax.broadcasted_iota(i32, (nbp, KV_WIDTH), 0) == nb_real - 1
    kc = jnp.where(is_new, c_new[:, :KV_WIDTH], kc_ref[0])
    vc = jnp.where(is_new, c_new[:, KV_WIDTH:], vc_ref[0])
    sc = jnp.dot(kc, q, preferred_element_type=f32)
    blk = lax.broadcasted_iota(i32, (nbp, LANES), 0)
    pc = _masked_softmax_cols(sc, blk * CMP_BLOCK + (CMP_BLOCK - 1) <= tpos, jnp.exp)
    oc_ref[0] = _dot_t0(pc.astype(bf16), vc)

    r = lax.broadcasted_iota(i32, (LANES, LANES), 0)
    cc = lax.broadcasted_iota(i32, (LANES, LANES), 1)
    per_kv = GROUP * lq
    same = jnp.where(r // per_kv == cc // per_kv, jnp.where(r % lq == cc % lq, 1.0, 0.0), 0.0).astype(bf16)
    hi, mid, lo = _split3(pc)
    imp = (jnp.dot(hi, same, preferred_element_type=f32) + jnp.dot(mid, same, preferred_element_type=f32)
           + jnp.dot(lo, same, preferred_element_type=f32))
    cur = jnp.right_shift(tpos, CMP_SHIFT)
    forced = (blk == 0) | (blk == cur) | (blk == cur - 1)
    base = jnp.where(forced, FORCED_SCORE, jnp.where(blk * CMP_BLOCK <= tpos, 0.0, -1.0))
    score = jnp.where(blk < nb_real, base + imp, -3.0)
    sel_ref[0] = _select_bias(score, min(TOP_K, nb_real))

    nwt = kw_ref.shape[1]
    nw = nwt - lq
    s = jnp.dot(kw_ref[0], q, preferred_element_type=f32)
    wpos = past - nw + lax.broadcasted_iota(i32, (nwt, LANES), 0)
    dist = tpos - wpos
    ok = (dist >= 0) & (dist < WINDOW) & (wpos >= 0)
    pw = _masked_softmax_cols(s, ok, jnp.exp)
    ow_ref[0] = _dot_t0(pw.astype(bf16), vw_ref[0])


def _smp_sel_kernel(pt_ref, *refs, n_pg):
    del pt_ref
    pages = refs[:n_pg]
    q_ref, sel_ref, e_ref, kn_ref, vn_ref, bn_ref, g_ref, oc_ref, ow_ref, o_ref, m_ref, l_ref, acc_ref = refs[n_pg:]
    i = pl.program_id(1)

    @pl.when(i == 0)
    def _():
        m_ref[...] = jnp.full(m_ref.shape, NEG, f32)
        l_ref[...] = jnp.zeros(l_ref.shape, f32)
        acc_ref[...] = jnp.zeros(acc_ref.shape, f32)

    def update(s, vt):
        m_new = jnp.maximum(m_ref[...], jnp.max(s, axis=1, keepdims=True))
        a = jnp.exp(m_ref[...] - m_new)
        p = jnp.exp(s - m_new)
        l_ref[...] = a * l_ref[...] + jnp.sum(p, axis=1, keepdims=True)
        acc_ref[...] = a * acc_ref[...] + _dot_nt(p.astype(bf16), vt)
        m_ref[...] = m_new

    qa = jnp.concatenate([q_ref[0], sel_ref[0, 0].astype(bf16)], axis=1)
    kt = jnp.concatenate([r[0, 0, 0].reshape(KV_WIDTH, PAGE_SIZE) for r in pages], axis=1).astype(bf16)
    vt = jnp.concatenate([r[0, 0, 1].reshape(KV_WIDTH, PAGE_SIZE) for r in pages], axis=1).astype(bf16)
    update(jnp.dot(qa, jnp.concatenate([kt, e_ref[...]], axis=0), preferred_element_type=f32), vt)

    @pl.when(i == pl.num_programs(1) - 1)
    def _():
        update(jnp.dot(q_ref[0], kn_ref[0].astype(bf16), preferred_element_type=f32) + bn_ref[0], vn_ref[0].astype(bf16))
        gt = _sigmoid(g_ref[0])
        osel = acc_ref[...] / jnp.maximum(l_ref[...], 1e-30)
        o_ref[0] = gt[:, 0:1] * oc_ref[0] + gt[:, 1:2] * osel + gt[:, 2:3] * ow_ref[0]


def _nsa_sample(x, cache_cmp_t, cache_sel_t, cache_win, page_table, layer, w_in, alpha_rows, n_pg=16):
    b, lq, d = x.shape
    n_pages = page_table.shape[1]
    past = n_pages * PAGE_SIZE
    assert KV_HEADS * GROUP * lq == LANES and past % CMP_BLOCK == 0 and lq <= CMP_BLOCK
    assert n_pages % n_pg == 0 and n_pg * PAGE_SIZE // CMP_BLOCK <= LANES
    p = _matmul(x.reshape(b * lq, d), w_in, b * lq, "nsa_in_proj_s")
    nq = N_HEADS * HEAD_DIM
    w2 = 2 * KV_WIDTH
    q = (p[:, :nq] * HEAD_DIM ** -0.5).reshape(b, lq, KV_HEADS, GROUP, HEAD_DIM)
    eye = jnp.eye(KV_HEADS, dtype=f32)
    qbd = (q.transpose(0, 2, 3, 1, 4)[:, :, :, :, None, :] * eye[None, :, None, None, :, None])
    qbd = qbd.reshape(b, LANES, KV_WIDTH).astype(bf16)
    qbd_t = qbd.transpose(0, 2, 1)
    gt = p[:, nq + 6 * KV_WIDTH:nq + 6 * KV_WIDTH + 3 * N_HEADS].reshape(b, lq, KV_HEADS, GROUP, 3)
    gt = gt.transpose(0, 2, 3, 1, 4).reshape(b, LANES, 3)
    kv = p[:, nq:nq + 6 * KV_WIDTH].reshape(b, lq, 3, w2)
    kv_c, kv_s, kv_w = kv[:, :, 0], kv[:, :, 1], kv[:, :, 2]

    bpp = PAGE_SIZE // CMP_BLOCK
    c_past = _compress_paged(cache_cmp_t, layer, page_table, alpha_rows, n_pg)
    c_past = c_past.transpose(0, 1, 3, 2).reshape(b, past // CMP_BLOCK, w2)
    nb_real = past // CMP_BLOCK + 1
    nbp = -(-nb_real // 8) * 8
    cblk = jnp.pad(c_past, ((0, 0), (0, nbp - nb_real + 1), (0, 0)))
    wkv = jnp.concatenate([cache_win[layer].reshape(b, -1, w2), kv_w], axis=1)
    nwt = wkv.shape[1]

    seq3 = lambda i: (i, 0, 0)
    oc, ow, sel = pl.pallas_call(
        functools.partial(_smp_cmp_win_kernel, past=past, lq=lq, nb_real=nb_real), grid=(b,),
        in_specs=[pl.BlockSpec((1, KV_WIDTH, LANES), seq3), pl.BlockSpec((1, nbp, KV_WIDTH), seq3),
                  pl.BlockSpec((1, nbp, KV_WIDTH), seq3), pl.BlockSpec((1, lq, w2), seq3),
                  pl.BlockSpec((lq, w2), lambda i: (0, 0)), pl.BlockSpec((1, nwt, KV_WIDTH), seq3),
                  pl.BlockSpec((1, nwt, KV_WIDTH), seq3)],
        out_specs=[pl.BlockSpec((1, LANES, KV_WIDTH), seq3), pl.BlockSpec((1, LANES, KV_WIDTH), seq3),
                   pl.BlockSpec((1, nbp, LANES), seq3)],
        out_shape=[jax.ShapeDtypeStruct((b, LANES, KV_WIDTH), f32), jax.ShapeDtypeStruct((b, LANES, KV_WIDTH), f32),
                   jax.ShapeDtypeStruct((b, nbp, LANES), f32)],
        compiler_params=_cp("arbitrary"), name="nsa_sample_cmp_win")(
            qbd_t, cblk[:, :, :KV_WIDTH].astype(bf16), cblk[:, :, KV_WIDTH:].astype(bf16), kv_c, alpha_rows.T[:lq],
            wkv[:, :, :KV_WIDTH].astype(bf16), wkv[:, :, KV_WIDTH:].astype(bf16))

    n_steps, bps = n_pages // n_pg, n_pg * bpp
    sel_r = sel.transpose(0, 2, 1)
    sel_steps = sel_r[:, :, :past // CMP_BLOCK].reshape(b, LANES, n_steps, bps).transpose(0, 2, 1, 3)
    sel_steps = jnp.pad(sel_steps, ((0, 0), (0, 0), (0, 0), (0, LANES - bps)))
    onehot = (jnp.arange(LANES)[:, None] == jnp.arange(n_pg * PAGE_SIZE)[None, :] // CMP_BLOCK).astype(bf16)
    pad_t = lambda a: jnp.pad(a.transpose(0, 2, 1), ((0, 0), (0, 0), (0, LANES - lq)))
    qoff = jnp.arange(LANES) % lq
    readable = jnp.arange(LANES)[None, :] <= qoff[:, None]
    bias_new = jnp.where(readable[None], sel_r[:, :, past // CMP_BLOCK][:, :, None], NEG)

    def page_map(bb, i, pt, k):
        return (layer, pt[bb, i * n_pg + k], 0, 0, 0, 0)

    seq3p = lambda bb, i, pt: (bb, 0, 0)
    grid_spec = pltpu.PrefetchScalarGridSpec(
        num_scalar_prefetch=1, grid=(b, n_steps),
        in_specs=[pl.BlockSpec((1, 1, 2, KV_HEADS, HEAD_DIM, PAGE_SIZE), functools.partial(page_map, k=k))
                  for k in range(n_pg)]
        + [pl.BlockSpec((1, LANES, KV_WIDTH), seq3p), pl.BlockSpec((1, 1, LANES, LANES), lambda bb, i, pt: (bb, i, 0, 0)),
           pl.BlockSpec(onehot.shape, lambda bb, i, pt: (0, 0)),
           pl.BlockSpec((1, KV_WIDTH, LANES), seq3p), pl.BlockSpec((1, KV_WIDTH, LANES), seq3p),
           pl.BlockSpec((1, LANES, LANES), seq3p), pl.BlockSpec((1, LANES, 3), seq3p),
           pl.BlockSpec((1, LANES, KV_WIDTH), seq3p), pl.BlockSpec((1, LANES, KV_WIDTH), seq3p)],
        out_specs=pl.BlockSpec((1, LANES, KV_WIDTH), seq3p),
        scratch_shapes=[pltpu.VMEM((LANES, 1), f32), pltpu.VMEM((LANES, 1), f32), pltpu.VMEM((LANES, KV_WIDTH), f32)])
    o = pl.pallas_call(
        functools.partial(_smp_sel_kernel, n_pg=n_pg), grid_spec=grid_spec,
        out_shape=jax.ShapeDtypeStruct((b, LANES, KV_WIDTH), f32),
        compiler_params=_cp("arbitrary", "arbitrary"), name="nsa_sample_sel")(
            page_table, *([cache_sel_t] * n_pg), qbd, sel_steps, onehot, pad_t(kv_s[:, :, :KV_WIDTH]),
            pad_t(kv_s[:, :, KV_WIDTH:]), bias_new, gt, oc, ow)
    o = o.reshape(b, KV_HEADS, GROUP, lq, KV_HEADS, HEAD_DIM)
    o = jnp.stack([o[:, h, :, :, h] for h in range(KV_HEADS)], axis=1)
    o = o.transpose(0, 3, 1, 2, 4).reshape(b * lq, d)
    return o, kv_c, kv_s, wkv[:, lq:]


def _hgrn_kernel(zq_ref, zf_ref, zi_ref, zg_ref, lb_ref, ng_ref, s0_ref, o_ref, sout_ref, st_ref, *, n_valid, hb):
    t = pl.program_id(2)
    tb = zq_ref.shape[1]
    n_ch = tb // HG_CHUNK

    @pl.when(t == 0)
    def _():
        for hh in range(hb):
            st_ref[hh] = s0_ref[0, hh].T

    r = lax.broadcasted_iota(i32, (tb, tb), 0)
    cc = lax.broadcasted_iota(i32, (tb, tb), 1)
    tri = jnp.where(r // HG_CHUNK == cc // HG_CHUNK, jnp.where(cc <= r, 1.0, 0.0), 0.0).astype(bf16)
    causal = lax.broadcasted_iota(i32, (HG_CHUNK, HG_CHUNK), 1) <= lax.broadcasted_iota(i32, (HG_CHUNK, HG_CHUNK), 0)
    mid_row = HG_CHUNK // 2

    zf = zf_ref[0]
    lb = lb_ref[0]
    la = jnp.log(lb)
    l1 = jnp.log1p(-lb)
    bb = l1 + _log_sigmoid(zf)
    log_f = jnp.maximum(la, bb) + jnp.log1p(jnp.exp(-jnp.abs(la - bb)))
    kk = jnp.exp(l1 + _log_sigmoid(-zf))
    if n_valid < tb:
        live = lax.broadcasted_iota(i32, zf.shape, 0) < n_valid
        log_f = jnp.where(live, log_f, 0.0)
        kk = jnp.where(live, kk, 0.0)
    hi, mid, lo = _split3(log_f)
    cum = (jnp.dot(tri, hi, preferred_element_type=f32) + jnp.dot(tri, mid, preferred_element_type=f32)
           + jnp.dot(tri, lo, preferred_element_type=f32))
    zq, zi, zg = zq_ref[0], zi_ref[0], zg_ref[0]
    gate = ng_ref[0] * (zg * _sigmoid(zg))

    qs, o_in, upd, dec = {}, {}, {}, {}
    for hh in range(hb):
        hs = slice(hh * HG_DK, (hh + 1) * HG_DK)
        for ci in range(n_ch):
            sl = slice(ci * HG_CHUNK, (ci + 1) * HG_CHUNK)
            cm, qi, ki, vi = cum[sl, hs], zq[sl, hs], kk[sl, hs], zi[sl, hs]
            ref = cm[mid_row:mid_row + 1]
            last = cm[HG_CHUNK - 1:HG_CHUNK]
            a = _dot_nt((qi * jnp.exp(cm - ref)).astype(bf16), (ki * jnp.exp(ref - cm)).astype(bf16))
            a = jnp.where(causal, a, 0.0)
            o_in[hh, ci] = jnp.dot(a.astype(bf16), vi.astype(bf16), preferred_element_type=f32)
            qs[hh, ci] = (qi * jnp.exp(cm)).astype(bf16)
            upd[hh, ci] = _dot_t0(vi.astype(bf16), (ki * jnp.exp(last - cm)).astype(bf16))
            dec[hh, ci] = jnp.exp(last)
    for hh in range(hb):
        hs = slice(hh * HG_DK, (hh + 1) * HG_DK)
        st = st_ref[hh]
        outs = []
        for ci in range(n_ch):
            outs.append(o_in[hh, ci] + _dot_nt(qs[hh, ci], st.astype(bf16)))
            st = st * dec[hh, ci] + upd[hh, ci]
        st_ref[hh] = st
        o = jnp.concatenate(outs, axis=0) if n_ch > 1 else outs[0]
        o = o * lax.rsqrt(jnp.mean(o * o, axis=-1, keepdims=True) + RMS_EPS)
        o_ref[0, :, hs] = (o * gate[:, hs]).astype(o_ref.dtype)

        @pl.when(t == pl.num_programs(2) - 1)
        def _():
            sout_ref[0, hh] = st.T


def _hgrn(p, s0, lb, norm_g, tb, n_valid, hb):
    b, lp, _ = p.shape
    ng = HG_HEADS // hb
    w = hb * HG_DK
    col = lambda off: (lambda bb, hh, t: (bb, t, off * ng + hh))
    per_head = lambda bb, hh, t: (0, hh)
    state = lambda bb, hh, t: (bb, hh, 0, 0)
    return pl.pallas_call(
        functools.partial(_hgrn_kernel, n_valid=n_valid, hb=hb), grid=(b, ng, lp // tb),
        in_specs=[pl.BlockSpec((1, tb, w), col(0)), pl.BlockSpec((1, tb, w), col(1)),
                  pl.BlockSpec((1, tb, w), col(2)), pl.BlockSpec((1, tb, w), col(3)),
                  pl.BlockSpec((1, w), per_head), pl.BlockSpec((1, w), per_head),
                  pl.BlockSpec((1, hb, HG_DK, HG_DV), state)],
        out_specs=[pl.BlockSpec((1, tb, w), lambda bb, hh, t: (bb, t, hh)),
                   pl.BlockSpec((1, hb, HG_DK, HG_DV), state)],
        out_shape=[jax.ShapeDtypeStruct((b, lp, HG_HEADS * HG_DV), bf16),
                   jax.ShapeDtypeStruct((b, HG_HEADS, HG_DK, HG_DV), f32)],
        scratch_shapes=[pltpu.VMEM((hb, HG_DV, HG_DK), f32)],
        compiler_params=_cp("arbitrary", "arbitrary", "arbitrary"), name="hgrn_scan")(
            p, p, p, p, lb.reshape(1, -1), norm_g.reshape(1, -1), s0)


def _hgrn_mixer(x, s0, w_in, lb, norm_g, tb):
    b, seq, d = x.shape
    lp = -(-seq // HG_CHUNK) * HG_CHUNK
    tb = min(tb, lp)
    p = _matmul(x.reshape(b * seq, d), w_in, min(256, b * seq), "hgrn_in_proj").reshape(b, seq, 4 * d)
    if lp != seq:
        p = jnp.pad(p, ((0, 0), (0, lp - seq), (0, 0)))
    o, s_out = _hgrn(p, s0, lb, norm_g, tb, seq if lp != seq else lp, HG_HB)
    return o[:, :seq].reshape(b * seq, d), s_out


def _router_logits(x, whi_ref, wlo_ref, b_ref):
    xh = x.astype(bf16)
    xl = (x - xh.astype(f32)).astype(bf16)
    return (jnp.dot(xh, whi_ref[...], preferred_element_type=f32) + jnp.dot(xl, whi_ref[...], preferred_element_type=f32)
            + jnp.dot(xh, wlo_ref[...], preferred_element_type=f32)) + b_ref[...]


def _route(lg):
    ne = N_GROUPS * EXPERTS_PER_GROUP
    lane = lax.broadcasted_iota(i32, lg.shape, 1)
    lanef = lane.astype(f32)
    in_c = (lane >= ne) & (lane < ne + N_GROUPS)
    lc = jnp.where(in_c, lg, -jnp.inf)
    mc = jnp.max(lc, axis=-1, keepdims=True)
    grp = jnp.min(jnp.where(lc == mc, lanef, 1e9), axis=-1, keepdims=True) - float(ne)
    pg = 1.0 / jnp.sum(jnp.where(in_c, jnp.exp(lc - mc), 0.0), axis=-1, keepdims=True)
    e_lo = grp * float(EXPERTS_PER_GROUP)
    in_e = (lanef >= e_lo) & (lanef < e_lo + float(EXPERTS_PER_GROUP))
    le = jnp.where(in_e, lg, -jnp.inf)
    v1 = jnp.max(le, axis=-1, keepdims=True)
    i1 = jnp.min(jnp.where(le == v1, lanef, 1e9), axis=-1, keepdims=True)
    le2 = jnp.where(lanef == i1, -jnp.inf, le)
    v2 = jnp.max(le2, axis=-1, keepdims=True)
    i2 = jnp.min(jnp.where(le2 == v2, lanef, 1e9), axis=-1, keepdims=True)
    e2 = jnp.exp(v2 - v1)
    w1 = pg / (1.0 + e2)
    return jnp.where(lanef == i1, w1, 0.0) + jnp.where(lanef == i2, w1 * e2, 0.0), grp


def _router_kernel(x_ref, whi_ref, wlo_ref, b_ref, cw_ref):
    cw, _ = _route(_router_logits(x_ref[...], whi_ref, wlo_ref, b_ref))
    for g in range(N_GROUPS):
        cw_ref[g] = cw if g == 0 else pltpu.roll(cw, LANES - g * EXPERTS_PER_GROUP, axis=1)


def _router(x, whi, wlo, bias, tm):
    t, d = x.shape
    return pl.pallas_call(
        _router_kernel, grid=(t // tm,),
        in_specs=[pl.BlockSpec((tm, d), lambda i: (i, 0)), pl.BlockSpec((d, LANES), lambda i: (0, 0)),
                  pl.BlockSpec((d, LANES), lambda i: (0, 0)), pl.BlockSpec((1, LANES), lambda i: (0, 0))],
        out_specs=pl.BlockSpec((N_GROUPS, tm, LANES), lambda i: (0, i, 0)),
        out_shape=jax.ShapeDtypeStruct((N_GROUPS, t, LANES), f32), compiler_params=_cp("arbitrary"),
        name="moe_router")(x, whi, wlo, bias)


def _moe_kernel(x_ref, cw_ref, wg_ref, wu_ref, wd_ref, g_ref, b_ref, o_ref, acc_ref):
    gi = pl.program_id(1)

    @pl.when(gi == 0)
    def _():
        acc_ref[...] = jnp.zeros(acc_ref.shape, f32)

    xb = x_ref[...].astype(bf16)
    cw = cw_ref[0]
    acc = acc_ref[...]
    for e in range(EXPERTS_PER_GROUP):
        hg = jnp.dot(xb, wg_ref[0, e], preferred_element_type=f32)
        hu = jnp.dot(xb, wu_ref[0, e], preferred_element_type=f32)
        h = hg * _sigmoid(hg) * hu * cw[:, e:e + 1]
        acc = acc + jnp.dot(h.astype(bf16), wd_ref[0, e], preferred_element_type=f32)
    acc_ref[...] = acc

    @pl.when(gi == pl.num_programs(1) - 1)
    def _():
        o_ref[...] = _layer_norm(ALPHA * x_ref[...] + acc_ref[...], g_ref[...], b_ref[...])


def _moe_ln(x, cw, wg, wu, wd, g, b, tm):
    t, d = x.shape
    e, hid = wg.shape[1], wg.shape[3]
    row = lambda i, gi: (i, 0)
    fix = lambda i, gi: (0, 0)
    return pl.pallas_call(
        _moe_kernel, grid=(t // tm, N_GROUPS),
        in_specs=[pl.BlockSpec((tm, d), row), pl.BlockSpec((1, tm, LANES), lambda i, gi: (gi, i, 0)),
                  pl.BlockSpec((1, e, d, hid), lambda i, gi: (gi, 0, 0, 0)),
                  pl.BlockSpec((1, e, d, hid), lambda i, gi: (gi, 0, 0, 0)),
                  pl.BlockSpec((1, e, hid, d), lambda i, gi: (gi, 0, 0, 0)),
                  pl.BlockSpec((1, d), fix), pl.BlockSpec((1, d), fix)],
        out_specs=pl.BlockSpec((tm, d), row),
        out_shape=jax.ShapeDtypeStruct((t, d), f32),
        scratch_shapes=[pltpu.VMEM((tm, d), f32)],
        compiler_params=_cp("arbitrary", "arbitrary"), name="moe_dense")(
            x, cw, wg, wu, wd, g.reshape(1, d), b.reshape(1, d))


META_GROUP, META_RANK = EXPERTS_PER_GROUP, EXPERTS_PER_GROUP + 1


def _router_sort_kernel(x_ref, whi_ref, wlo_ref, b_ref, xc_ref, cnt_ref, carry_ref):
    @pl.when(pl.program_id(0) == 0)
    def _():
        carry_ref[...] = jnp.zeros(carry_ref.shape, f32)

    x = x_ref[...]
    tm, d = x.shape
    cw, grp = _route(_router_logits(x, whi_ref, wlo_ref, b_ref))
    cw8 = cw
    for g in range(1, N_GROUPS):
        cw8 = jnp.where(grp == float(g), pltpu.roll(cw, LANES - g * EXPERTS_PER_GROUP, axis=1), cw8)
    lanef = lax.broadcasted_iota(i32, (tm, LANES), 1).astype(f32)
    onehot = jnp.where(lanef == grp, 1.0, 0.0)
    before = lax.broadcasted_iota(i32, (tm, tm), 1) < lax.broadcasted_iota(i32, (tm, tm), 0)
    prefix = jnp.dot(before.astype(bf16), onehot.astype(bf16), preferred_element_type=f32) + carry_ref[...]
    rank = jnp.sum(onehot * prefix, axis=-1, keepdims=True)
    carry_ref[...] = carry_ref[...] + jnp.sum(onehot, axis=0, keepdims=True)
    cnt_ref[...] = carry_ref[...]
    xc_ref[:, :d] = x
    xc_ref[:, d:] = cw8 + jnp.where(lanef == float(META_GROUP), grp, 0.0) + jnp.where(lanef == float(META_RANK), rank, 0.0)


def _router_sort(x, whi, wlo, bias, tm):
    t, d = x.shape
    fix = lambda i: (0, 0)
    return pl.pallas_call(
        _router_sort_kernel, grid=(t // tm,),
        in_specs=[pl.BlockSpec((tm, d), lambda i: (i, 0)), pl.BlockSpec((d, LANES), fix), pl.BlockSpec((d, LANES), fix),
                  pl.BlockSpec((1, LANES), fix)],
        out_specs=[pl.BlockSpec((tm, d + LANES), lambda i: (i, 0)), pl.BlockSpec((1, LANES), fix)],
        out_shape=[jax.ShapeDtypeStruct((t, d + LANES), f32), jax.ShapeDtypeStruct((1, LANES), f32)],
        scratch_shapes=[pltpu.VMEM((1, LANES), f32)],
        compiler_params=_cp("arbitrary"), name="moe_router_sort")(x, whi, wlo, bias)


def _row_copy(src, i, dst, j, sem):
    return pltpu.make_async_copy(src.at[pl.ds(i, 1)], dst.at[pl.ds(j, 1)], sem)


def _dispatch_kernel(pos_ref, xc_ref, zero_ref, xs_ref, sem):
    del zero_ref
    tm = xc_ref.shape[0]
    base = pl.program_id(0) * tm

    def start(r, c):
        _row_copy(xc_ref, r, xs_ref, pos_ref[base + r], sem).start()
        return c

    def wait(r, c):
        _row_copy(xc_ref, 0, xs_ref, 0, sem).wait()
        return c

    lax.fori_loop(0, tm, start, 0, unroll=8)
    lax.fori_loop(0, tm, wait, 0, unroll=8)


def _dispatch(xc, pos, n_rows, tm):
    t, w = xc.shape
    grid_spec = pltpu.PrefetchScalarGridSpec(
        num_scalar_prefetch=1, grid=(t // tm,),
        in_specs=[pl.BlockSpec((tm, w), lambda i, pos: (i, 0)), pl.BlockSpec(memory_space=pl.ANY)],
        out_specs=pl.BlockSpec(memory_space=pl.ANY),
        scratch_shapes=[pltpu.SemaphoreType.DMA(())])
    return pl.pallas_call(
        _dispatch_kernel, grid_spec=grid_spec, out_shape=jax.ShapeDtypeStruct((n_rows, w), f32),
        input_output_aliases={2: 0}, compiler_params=_cp("arbitrary"), name="moe_dispatch")(
            pos, xc, jnp.zeros((n_rows, w), f32))


def _moe_sorted_kernel(tg_ref, xs_ref, wg_ref, wu_ref, wd_ref, ys_ref):
    del tg_ref
    d = ys_ref.shape[1]
    xb = xs_ref[:, :d].astype(bf16)
    cw = xs_ref[:, d:]
    acc = jnp.zeros(ys_ref.shape, f32)
    for e in range(EXPERTS_PER_GROUP):
        hg = jnp.dot(xb, wg_ref[0, e], preferred_element_type=f32)
        hu = jnp.dot(xb, wu_ref[0, e], preferred_element_type=f32)
        h = hg * _sigmoid(hg) * hu * cw[:, e:e + 1]
        acc = acc + jnp.dot(h.astype(bf16), wd_ref[0, e], preferred_element_type=f32)
    ys_ref[...] = acc


def _moe_sorted(xs, tile_group, wg, wu, wd, tm):
    n_rows, w = xs.shape
    e, d, hid = wg.shape[1], wg.shape[2], wg.shape[3]
    grp4 = lambda i, tg: (tg[i], 0, 0, 0)
    grid_spec = pltpu.PrefetchScalarGridSpec(
        num_scalar_prefetch=1, grid=(n_rows // tm,),
        in_specs=[pl.BlockSpec((tm, w), lambda i, tg: (i, 0)), pl.BlockSpec((1, e, d, hid), grp4),
                  pl.BlockSpec((1, e, d, hid), grp4), pl.BlockSpec((1, e, hid, d), grp4)],
        out_specs=pl.BlockSpec((tm, d), lambda i, tg: (i, 0)))
    return pl.pallas_call(
        _moe_sorted_kernel, grid_spec=grid_spec, out_shape=jax.ShapeDtypeStruct((n_rows, d), f32),
        compiler_params=_cp("arbitrary"), name="moe_sorted")(tile_group, xs, wg, wu, wd)


def _combine_kernel(pos_ref, x_ref, ys_ref, g_ref, b_ref, o_ref, ybuf, sem):
    tm = x_ref.shape[0]
    i = pl.program_id(0)
    n = pl.num_programs(0)

    def fetch(step, slot):
        def start(r, c):
            _row_copy(ys_ref, pos_ref[step * tm + r], ybuf.at[slot], r, sem.at[slot]).start()
            return c
        lax.fori_loop(0, tm, start, 0, unroll=8)

    @pl.when(i == 0)
    def _():
        fetch(0, 0)

    @pl.when(i + 1 < n)
    def _():
        fetch(i + 1, (i + 1) % 2)

    slot = i % 2

    def wait(r, c):
        _row_copy(ys_ref, 0, ybuf.at[slot], 0, sem.at[slot]).wait()
        return c

    lax.fori_loop(0, tm, wait, 0, unroll=8)
    o_ref[...] = _layer_norm(ALPHA * x_ref[...] + ybuf[slot], g_ref[...], b_ref[...])


def _combine_ln(x, ys, pos, g, b, tm):
    t, d = x.shape
    fix = lambda i, pos: (0, 0)
    grid_spec = pltpu.PrefetchScalarGridSpec(
        num_scalar_prefetch=1, grid=(t // tm,),
        in_specs=[pl.BlockSpec((tm, d), lambda i, pos: (i, 0)), pl.BlockSpec(memory_space=pl.ANY),
                  pl.BlockSpec((1, d), fix), pl.BlockSpec((1, d), fix)],
        out_specs=pl.BlockSpec((tm, d), lambda i, pos: (i, 0)),
        scratch_shapes=[pltpu.VMEM((2, tm, d), f32), pltpu.SemaphoreType.DMA((2,))])
    return pl.pallas_call(
        _combine_kernel, grid_spec=grid_spec, out_shape=jax.ShapeDtypeStruct((t, d), f32),
        compiler_params=_cp("arbitrary"), name="moe_combine_ln")(pos, x, ys, g.reshape(1, d), b.reshape(1, d))


def _moe_ln_sorted(x, whi, wlo, rbias, wg, wu, wd, g, b, tm=512):
    t, d = x.shape
    xc, cnt = _router_sort(x, whi, wlo, rbias, tm)
    grp = xc[:, d + META_GROUP].astype(i32)
    rank = xc[:, d + META_RANK].astype(i32)
    counts = cnt[0, :N_GROUPS].astype(i32)
    padded = (counts + tm - 1) // tm * tm
    ends = jnp.cumsum(padded)
    start = ends - padded
    pos = rank + sum(jnp.where(grp == gi, start[gi], 0) for gi in range(N_GROUPS))
    n_rows = t + N_GROUPS * tm
    tile_start = jnp.arange(n_rows // tm, dtype=i32) * tm
    tile_group = jnp.minimum(sum((tile_start >= ends[gi]).astype(i32) for gi in range(N_GROUPS)), N_GROUPS - 1)
    xs = _dispatch(xc, pos, n_rows, tm)
    ys = _moe_sorted(xs, tile_group, wg, wu, wd, tm)
    return _combine_ln(x, ys, pos, g, b, tm)


def _router_weights(w_rc, b_rc, w_re, b_re):
    d = w_rc.shape[0]
    ne = N_GROUPS * EXPERTS_PER_GROUP
    w = jnp.concatenate([w_re.reshape(d, ne), w_rc, jnp.zeros((d, LANES - ne - N_GROUPS), f32)], axis=1)
    bias = jnp.concatenate([b_re.reshape(ne), b_rc, jnp.zeros((LANES - ne - N_GROUPS,), f32)]).reshape(1, LANES)
    whi = w.astype(bf16)
    wlo = (w - whi.astype(f32)).astype(bf16)
    return whi, wlo, bias


def _token_major(kv_t):
    b, _, n = kv_t.shape
    return kv_t.reshape(b, 2, KV_HEADS, HEAD_DIM, n).transpose(0, 4, 1, 2, 3)


def kernel(x_prompt, x_sample, cache_cmp_kv, cache_sel_kv, cache_win_kv, state_hgrn, page_table, attn_w_in, attn_cmp_alpha, attn_w_out, rec_w_in, rec_lb_logits, rec_norm_g, rec_w_out, ln_g, ln_b, moe_w_router_c, moe_b_router_c, moe_w_router_e, moe_b_router_e, moe_w_gate, moe_w_up, moe_w_down):
    b, seq, d = x_prompt.shape
    bs, lq, _ = x_sample.shape
    tp, ts = b * seq, bs * lq
    lb_all = jnp.cumsum(jax.nn.softmax(rec_lb_logits.astype(f32), axis=0), axis=0)
    hp, hs = x_prompt.reshape(tp, d), x_sample.reshape(ts, d)
    cache_cmp_t = cache_cmp_kv.transpose(0, 1, 3, 4, 5, 2)
    cache_sel_t = cache_sel_kv.transpose(0, 1, 3, 4, 5, 2)
    kv_out = [[] for _ in range(6)]
    rec_out = [[], []]
    for layer in range(DEPTH):
        j = layer // 2
        if layer % 2 == 0:
            nq, n_in = N_HEADS * HEAD_DIM, attn_w_in.shape[2]
            w_all = attn_w_in[j]
            w_in = jnp.pad(w_all, ((0, 0), (0, -(-n_in // LANES) * LANES - n_in))).astype(bf16)
            w_t = w_all.T.astype(bf16)
            alpha_rows = jnp.repeat(attn_cmp_alpha[j], HEAD_DIM, axis=-1).reshape(CMP_BLOCK, 2 * KV_WIDTH).T
            w_out = attn_w_out[j].astype(bf16)
            op, c_p, s_p, w_p = _nsa_prompt(hp.reshape(b, seq, d), w_t, alpha_rows)
            os_, c_s, s_s, w_s = _nsa_sample(hs.reshape(bs, lq, d), cache_cmp_t, cache_sel_t, cache_win_kv,
                                             page_table, j, w_in, alpha_rows)
            kvs = (bs, lq, 2, KV_HEADS, HEAD_DIM)
            n_win = min(WINDOW, seq)
            for lst, val in zip(kv_out, (_token_major(c_p), c_s.reshape(kvs), _token_major(s_p), s_s.reshape(kvs),
                                         _token_major(w_p[:, :, seq - n_win:]),
                                         w_s.reshape(bs, -1, 2, KV_HEADS, HEAD_DIM))):
                lst.append(val)
        else:
            lb = lb_all[layer] - lb_all[0]
            w_in = rec_w_in[j].astype(bf16)
            w_out = rec_w_out[j].astype(bf16)
            op, st_p = _hgrn_mixer(hp.reshape(b, seq, d), jnp.zeros((b, HG_HEADS, HG_DK, HG_DV), f32), w_in, lb,
                                   rec_norm_g[j], 256)
            os_, st_s = _hgrn_mixer(hs.reshape(bs, lq, d), state_hgrn[j], w_in, lb, rec_norm_g[j], 256)
            rec_out[0].append(st_p)
            rec_out[1].append(st_s)
        hp = _proj_ln(op, w_out, hp, ln_g[layer, 0], ln_b[layer, 0], 512, "out_proj_ln")
        hs = _proj_ln(os_, w_out, hs, ln_g[layer, 0], ln_b[layer, 0], ts, "out_proj_ln_s")
        whi, wlo, rbias = _router_weights(moe_w_router_c[layer], moe_b_router_c[layer], moe_w_router_e[layer],
                                          moe_b_router_e[layer])
        wg, wu, wd = moe_w_gate[layer].astype(bf16), moe_w_up[layer].astype(bf16), moe_w_down[layer].astype(bf16)
        hp = _moe_ln_sorted(hp, whi, wlo, rbias, wg, wu, wd, ln_g[layer, 1], ln_b[layer, 1])
        hs = _moe_ln(hs, _router(hs, whi, wlo, rbias, ts), wg, wu, wd, ln_g[layer, 1], ln_b[layer, 1], ts)
    return (hp.reshape(b, seq, d), hs.reshape(bs, lq, d), *[jnp.stack(v) for v in kv_out],
            jnp.stack(rec_out[0]), jnp.stack(rec_out[1]))
```

```python
import functools

import jax
import jax.numpy as jnp
from jax import lax
from jax.experimental import pallas as pl
from jax.experimental.pallas import tpu as pltpu

f32, bf16, i32 = jnp.float32, jnp.bfloat16, jnp.int32

DEPTH = 2
N_HEADS, KV_HEADS, HEAD_DIM = 16, 4, 64
GROUP = N_HEADS // KV_HEADS
KV_WIDTH = KV_HEADS * HEAD_DIM
CMP_BLOCK, TOP_K, WINDOW = 64, 16, 512
FORCED_SCORE = float(GROUP + 1)
PAGE_SIZE = 128
HG_HEADS, HG_DK, HG_DV, HG_CHUNK = 8, 128, 128, 32
N_GROUPS, EXPERTS_PER_GROUP = 4, 8
ALPHA = (2.0 * DEPTH) ** 0.25
LN_EPS, RMS_EPS = 1e-5, 1e-6

NEG = -1e30
LOG2E = 1.4426950408889634
CMP_SHIFT = CMP_BLOCK.bit_length() - 1
assert 1 << CMP_SHIFT == CMP_BLOCK
LANES = 128
ONES_ROWS = 16
HG_HB = 8
VMEM_LIMIT = 56 * 1024 * 1024


def _cp(*sem):
    return pltpu.CompilerParams(dimension_semantics=sem, vmem_limit_bytes=VMEM_LIMIT)


def _sigmoid(x):
    return 1.0 / (1.0 + jnp.exp(-x))


def _log_sigmoid(x):
    return jnp.minimum(x, 0.0) - jnp.log1p(jnp.exp(-jnp.abs(x)))


def _layer_norm(z, g, b):
    mu = jnp.mean(z, axis=-1, keepdims=True)
    d = z - mu
    var = jnp.mean(d * d, axis=-1, keepdims=True)
    return d * lax.rsqrt(var + LN_EPS) * g + b


def _mm_kernel(x_ref, w_ref, o_ref):
    o_ref[...] = jnp.dot(x_ref[...].astype(bf16), w_ref[...], preferred_element_type=f32).astype(o_ref.dtype)


def _matmul(x, w, tm, name):
    m, k = x.shape
    n = w.shape[1]
    return pl.pallas_call(
        _mm_kernel, grid=(m // tm,),
        in_specs=[pl.BlockSpec((tm, k), lambda i: (i, 0)), pl.BlockSpec((k, n), lambda i: (0, 0))],
        out_specs=pl.BlockSpec((tm, n), lambda i: (i, 0)),
        out_shape=jax.ShapeDtypeStruct((m, n), f32), compiler_params=_cp("arbitrary"), name=name)(x, w)


def _proj_ln_kernel(o_ref, w_ref, x_ref, g_ref, b_ref, h_ref):
    y = jnp.dot(o_ref[...].astype(bf16), w_ref[...], preferred_element_type=f32)
    h_ref[...] = _layer_norm(ALPHA * x_ref[...] + y, g_ref[...], b_ref[...])


def _proj_ln(o, w, x, g, b, tm, name):
    m, k = o.shape
    n = w.shape[1]
    row = lambda i: (i, 0)
    fix = lambda i: (0, 0)
    return pl.pallas_call(
        _proj_ln_kernel, grid=(m // tm,),
        in_specs=[pl.BlockSpec((tm, k), row), pl.BlockSpec((k, n), fix), pl.BlockSpec((tm, n), row),
                  pl.BlockSpec((1, n), fix), pl.BlockSpec((1, n), fix)],
        out_specs=pl.BlockSpec((tm, n), row),
        out_shape=jax.ShapeDtypeStruct((m, n), f32), compiler_params=_cp("arbitrary"), name=name)(
            o, w, x, g.reshape(1, n), b.reshape(1, n))


def _split3(x):
    hi = x.astype(bf16)
    r1 = x - hi.astype(f32)
    mid = r1.astype(bf16)
    return hi, mid, (r1 - mid.astype(f32)).astype(bf16)


def _dot_nt(a, b):
    return lax.dot_general(a, b, (((1,), (1,)), ((), ())), preferred_element_type=f32)


def _block_sum_lanes(y, onehot):
    hi = y.astype(bf16)
    return _dot_nt(hi, onehot) + _dot_nt((y - hi.astype(f32)).astype(bf16), onehot)


def _compress_kernel(x_ref, a_ref, e_ref, o_ref):
    o_ref[0, 0] = _block_sum_lanes(x_ref[0, 0] * a_ref[0], e_ref[...])


def _compress_t(kvt, alpha_l, onehot):
    b, ch, hd, seq = kvt.shape
    nbp = onehot.shape[0]
    return pl.pallas_call(
        _compress_kernel, grid=(b, ch),
        in_specs=[pl.BlockSpec((1, 1, hd, seq), lambda i, j: (i, j, 0, 0)),
                  pl.BlockSpec((1, 1, seq), lambda i, j: (j, 0, 0)),
                  pl.BlockSpec((nbp, seq), lambda i, j: (0, 0))],
        out_specs=pl.BlockSpec((1, 1, hd, nbp), lambda i, j: (i, j, 0, 0)),
        out_shape=jax.ShapeDtypeStruct((b, ch, hd, nbp), f32),
        compiler_params=_cp("arbitrary", "arbitrary"), name="nsa_compress")(kvt, alpha_l, onehot)


def _compress_paged_kernel(pt_ref, *refs, n_pg):
    del pt_ref
    pages, a_ref, e_ref, o_ref = refs[:n_pg], refs[n_pg], refs[n_pg + 1], refs[n_pg + 2]
    w = 2 * KV_WIDTH
    x = jnp.concatenate([r[0, 0].reshape(w, PAGE_SIZE) for r in pages], axis=1)
    o_ref[0, 0] = _block_sum_lanes(x * a_ref[...], e_ref[...])


def _compress_paged(cache_t, layer, page_table, alpha_rows, n_pg):
    nb_, n_pages = page_table.shape
    bpp = PAGE_SIZE // CMP_BLOCK
    n_tok = n_pg * PAGE_SIZE
    a_t = jnp.tile(alpha_rows, (1, n_pg * bpp))
    onehot = (jnp.arange(n_pg * bpp)[:, None] == jnp.arange(n_tok)[None, :] // CMP_BLOCK).astype(bf16)

    def page_map(b, i, pt, k):
        return (layer, pt[b, i * n_pg + k], 0, 0, 0, 0)

    fix = lambda b, i, pt: (0, 0)
    grid_spec = pltpu.PrefetchScalarGridSpec(
        num_scalar_prefetch=1, grid=(nb_, n_pages // n_pg),
        in_specs=[pl.BlockSpec((1, 1, 2, KV_HEADS, HEAD_DIM, PAGE_SIZE), functools.partial(page_map, k=k))
                  for k in range(n_pg)]
        + [pl.BlockSpec(a_t.shape, fix), pl.BlockSpec(onehot.shape, fix)],
        out_specs=pl.BlockSpec((1, 1, 2 * KV_WIDTH, n_pg * bpp), lambda b, i, pt: (b, i, 0, 0)))
    return pl.pallas_call(
        functools.partial(_compress_paged_kernel, n_pg=n_pg), grid_spec=grid_spec,
        out_shape=jax.ShapeDtypeStruct((nb_, n_pages // n_pg, 2 * KV_WIDTH, n_pg * bpp), f32),
        compiler_params=_cp("arbitrary", "arbitrary"), name="nsa_compress_paged")(
            page_table, *([cache_t] * n_pg), a_t, onehot)


def _select_bias(score, k_sel):
    rowf = lax.broadcasted_iota(i32, score.shape, 0).astype(f32)
    picked = jnp.zeros(score.shape, f32)
    work = score
    for _ in range(k_sel):
        mx = jnp.max(work, axis=0, keepdims=True)
        first = jnp.min(jnp.where(work == mx, rowf, 1e9), axis=0, keepdims=True)
        pick = rowf == first
        picked = jnp.where(pick, 1.0, picked)
        work = jnp.where(pick, -jnp.inf, work)
    return jnp.where(picked > 0.0, 0.0, NEG)


def _masked_softmax_cols(s, mask, exp_fn):
    sm = jnp.where(mask, s, NEG)
    m = jnp.max(sm, axis=0, keepdims=True)
    e = jnp.where(mask, exp_fn(sm - m), 0.0)
    return e / jnp.maximum(jnp.sum(e, axis=0, keepdims=True), 1e-30)


def _nsa_in_proj_kernel(x_ref, w_ref, q_ref, g_ref, c_ref, s_ref, w_out_ref, *, q_scale):
    pt = _dot_nt(w_ref[...], x_ref[0].astype(bf16))
    nq, w = N_HEADS * HEAD_DIM, 2 * KV_WIDTH
    q_ref[0] = (pt[:nq] * q_scale).astype(q_ref.dtype)
    for n, ref in enumerate((c_ref, s_ref, w_out_ref)):
        ref[0] = pt[nq + n * w:nq + (n + 1) * w]
    g_ref[0] = pt[nq + 3 * w:nq + 3 * w + 3 * N_HEADS]


def _nsa_in_proj(x, w_t, tm, q_scale):
    b, seq, d = x.shape
    nq, w, ng = N_HEADS * HEAD_DIM, 2 * KV_WIDTH, 3 * N_HEADS
    rows = lambda n: pl.BlockSpec((1, n, tm), lambda i, j: (i, 0, j))
    kv_shape = jax.ShapeDtypeStruct((b, w, seq), f32)
    return pl.pallas_call(
        functools.partial(_nsa_in_proj_kernel, q_scale=q_scale), grid=(b, seq // tm),
        in_specs=[pl.BlockSpec((1, tm, d), lambda i, j: (i, j, 0)), pl.BlockSpec(w_t.shape, lambda i, j: (0, 0))],
        out_specs=[rows(nq), rows(ng), rows(w), rows(w), rows(w)],
        out_shape=[jax.ShapeDtypeStruct((b, nq, seq), bf16), jax.ShapeDtypeStruct((b, ng, seq), f32),
                   kv_shape, kv_shape, kv_shape],
        compiler_params=_cp("arbitrary", "arbitrary"), name="nsa_in_proj")(x, w_t)


def _nsa_prompt_kernel(qt_ref, g_ref, kct_ref, vct_ref, kst_ref, vst_ref, kwt_ref, vwt_ref, et_ref, o_ref,
                       m_ref, acc_ref, sa_ref, sb_ref, ka_ref, va_ref, kw_ref, vwa_ref, *, tq, tk, nb):
    s0 = pl.program_id(1) * tq
    c = GROUP * tq
    nbp = et_ref.shape[1]
    qt = jnp.concatenate([qt_ref[0, g * HEAD_DIM:(g + 1) * HEAD_DIM, :] for g in range(GROUP)], axis=1)
    tq_pos = s0 + lax.broadcasted_iota(i32, (1, tq), 1)
    tpos = jnp.concatenate([tq_pos] * GROUP, axis=1)

    kc = kct_ref[0, 0, 0].T.astype(bf16)
    sc = jnp.dot(kc, qt, preferred_element_type=f32)
    blk_c = lax.broadcasted_iota(i32, (nbp, c), 0)
    pc = _masked_softmax_cols(sc, blk_c * CMP_BLOCK + (CMP_BLOCK - 1) <= tpos, jnp.exp2)
    oc = jnp.dot(vct_ref[0, 0, 0].astype(bf16), pc.astype(bf16), preferred_element_type=f32)

    imp = pc[:, 0:tq]
    for g in range(1, GROUP):
        imp = imp + pc[:, g * tq:(g + 1) * tq]
    blk = lax.broadcasted_iota(i32, (nbp, tq), 0)
    cur = jnp.right_shift(tq_pos, CMP_SHIFT)
    forced = (blk == 0) | (blk == cur) | (blk == cur - 1)
    base = jnp.where(forced, FORCED_SCORE, jnp.where(blk * CMP_BLOCK <= tq_pos, 0.0, -1.0))
    sel = _select_bias(jnp.where(blk < nb, base + imp, -3.0), min(TOP_K, nb)).astype(bf16)
    qa = jnp.concatenate([jnp.concatenate([sel] * GROUP, axis=1), qt], axis=0)

    m_ref[...] = jnp.full(m_ref.shape, NEG, f32)
    acc_ref[...] = jnp.zeros(acc_ref.shape, f32)

    @pl.when(pl.program_id(1) == 0)
    def _():
        ones_rows = jnp.ones((ONES_ROWS, tk), bf16)

        def fill(t, carry):
            k0 = pl.multiple_of(t * tk, tk)
            ka_ref[pl.ds(k0, tk), :] = jnp.concatenate(
                [et_ref[pl.ds(k0, tk), :], kst_ref[0, 0, 0, :, pl.ds(k0, tk)].T.astype(bf16)], axis=1)
            va_ref[:, pl.ds(k0, tk)] = jnp.concatenate([vst_ref[0, 0, 0, :, pl.ds(k0, tk)].astype(bf16), ones_rows], axis=0)
            kw_ref[pl.ds(k0, tk), :] = kwt_ref[0, 0, 0, :, pl.ds(k0, tk)].T.astype(bf16)
            vwa_ref[:, pl.ds(k0, tk)] = jnp.concatenate([vwt_ref[0, 0, 0, :, pl.ds(k0, tk)].astype(bf16), ones_rows], axis=0)
            return carry

        lax.fori_loop(0, ka_ref.shape[0] // tk, fill, 0)

    def produce(j, s_ref):
        k0 = pl.multiple_of(j * tk, tk)
        s_ref[...] = jnp.dot(ka_ref[pl.ds(k0, tk), :], qa, preferred_element_type=f32)

    def consume(j, s_ref, width, causal):
        k0 = pl.multiple_of(j * tk, tk)
        s = s_ref[0:width, :]
        if causal:
            tail = jnp.where(k0 + (width - tq) + lax.broadcasted_iota(i32, (tq, c), 0) <= tpos, s[width - tq:], NEG)
            s = tail if width == tq else jnp.concatenate([s[:width - tq], tail], axis=0)
        m_new = jnp.maximum(m_ref[...], jnp.max(s, axis=0, keepdims=True))
        a = jnp.exp2(m_ref[...] - m_new)
        p = jnp.exp2(s - m_new).astype(bf16)
        acc_ref[...] = a * acc_ref[...] + jnp.dot(va_ref[:, pl.ds(k0, width)], p, preferred_element_type=f32)
        m_ref[...] = m_new

    n_full = s0 // tk
    produce(0, sa_ref)

    def pair(i, carry):
        produce(2 * i + 1, sb_ref)
        consume(2 * i, sa_ref, tk, False)
        produce(2 * i + 2, sa_ref)
        consume(2 * i + 1, sb_ref, tk, False)
        return carry

    lax.fori_loop(0, n_full // 2, pair, 0)
    odd = n_full % 2 == 1

    @pl.when(odd)
    def _():
        produce(n_full, sb_ref)
        consume(n_full - 1, sa_ref, tk, False)

    for wi in range(1, tk // tq + 1):
        here = s0 - n_full * tk == (wi - 1) * tq

        @pl.when(here & odd)
        def _():
            consume(n_full, sb_ref, wi * tq, True)

        @pl.when(here & jnp.logical_not(odd))
        def _():
            consume(n_full, sa_ref, wi * tq, True)

    osel = acc_ref[0:HEAD_DIM] / jnp.maximum(acc_ref[HEAD_DIM:HEAD_DIM + 1], 1e-30)

    wt = WINDOW + tq
    k0w = pl.multiple_of(jnp.maximum(s0 - WINDOW, 0), LANES)
    s = jnp.dot(kw_ref[pl.ds(k0w, wt), :], qt, preferred_element_type=f32)
    dist = tpos - (k0w + lax.broadcasted_iota(i32, (wt, c), 0))
    s = jnp.where(dist >= 0, jnp.where(dist < WINDOW, s, NEG), NEG)
    p = jnp.exp2(s - jnp.max(s, axis=0, keepdims=True)).astype(bf16)
    ow = jnp.dot(vwa_ref[:, pl.ds(k0w, wt)], p, preferred_element_type=f32)
    ow = ow[0:HEAD_DIM] / jnp.maximum(ow[HEAD_DIM:HEAD_DIM + 1], 1e-30)

    gt = _sigmoid(g_ref[0, 0])
    outs = []
    for g in range(GROUP):
        cs = slice(g * tq, (g + 1) * tq)
        og = gt[3 * g:3 * g + 1] * oc[:, cs] + gt[3 * g + 1:3 * g + 2] * osel[:, cs] + gt[3 * g + 2:3 * g + 3] * ow[:, cs]
        outs.append(og.T)
    o_ref[0] = jnp.concatenate(outs, axis=1).astype(o_ref.dtype)


def _nsa_prompt_attention(qt, gt, cblk_t, kvs_t, kvw_t, onehot_t, tq, tk):
    b, nq, seq = qt.shape
    kvh, hd, c = KV_HEADS, HEAD_DIM, GROUP * tq
    nbp = onehot_t.shape[1]
    assert tk % tq == 0
    comp = lambda cc: (lambda bh, i: (bh // kvh, cc, bh % kvh, 0, 0))
    rows = lambda n: (pl.BlockSpec((1, 1, 1, hd, n), comp(0)), pl.BlockSpec((1, 1, 1, hd, n), comp(1)))
    return pl.pallas_call(
        functools.partial(_nsa_prompt_kernel, tq=tq, tk=tk, nb=seq // CMP_BLOCK), grid=(b * kvh, seq // tq),
        in_specs=[pl.BlockSpec((1, GROUP * hd, tq), lambda bh, i: (bh // kvh, bh % kvh, i)),
                  pl.BlockSpec((1, 1, GROUP * 3, tq), lambda bh, i: (bh // kvh, bh % kvh, 0, i)),
                  *rows(nbp), *rows(seq), *rows(seq), pl.BlockSpec((seq, nbp), lambda bh, i: (0, 0))],
        out_specs=pl.BlockSpec((1, tq, GROUP * hd), lambda bh, i: (bh // kvh, i, bh % kvh)),
        out_shape=jax.ShapeDtypeStruct((b, seq, nq), bf16),
        scratch_shapes=[pltpu.VMEM((1, c), f32), pltpu.VMEM((hd + ONES_ROWS, c), f32),
                        pltpu.VMEM((tk, c), f32), pltpu.VMEM((tk, c), f32),
                        pltpu.VMEM((seq, nbp + hd), bf16), pltpu.VMEM((hd + ONES_ROWS, seq), bf16),
                        pltpu.VMEM((seq, hd), bf16), pltpu.VMEM((hd + ONES_ROWS, seq), bf16)],
        compiler_params=_cp("arbitrary", "arbitrary"), name="nsa_prompt_attn")(
            qt, gt, cblk_t, cblk_t, kvs_t, kvs_t, kvw_t, kvw_t, onehot_t)


def _nsa_prompt(x, w_t, alpha_rows, tq=256, tk=512):
    b, seq, d = x.shape
    nb = seq // CMP_BLOCK
    qt, gt, kvc_t, kvs_t, kvw_t = _nsa_in_proj(x, w_t, 512, HEAD_DIM ** -0.5 * LOG2E)
    five = lambda a: a.reshape(b, 2, KV_HEADS, HEAD_DIM, -1)
    nbp = -(-nb // LANES) * LANES
    onehot = (jnp.arange(nbp)[:, None] == jnp.arange(seq)[None, :] // CMP_BLOCK).astype(bf16)
    alpha_l = jnp.tile(alpha_rows[::HEAD_DIM], (1, nb)).reshape(2 * KV_HEADS, 1, seq)
    cblk_t = _compress_t(kvc_t.reshape(b, 2 * KV_HEADS, HEAD_DIM, seq), alpha_l, onehot)
    o = _nsa_prompt_attention(qt, gt.reshape(b, KV_HEADS, GROUP * 3, seq), five(cblk_t), five(kvs_t), five(kvw_t),
                              onehot.T, tq, tk)
    return o.reshape(b * seq, d), kvc_t, kvs_t, kvw_t


def _dot_t0(a, b):
    return lax.dot_general(a, b, (((0,), (0,)), ((), ())), preferred_element_type=f32)


def _smp_cmp_win_kernel(q_ref, kc_ref, vc_ref, cn_ref, a_ref, kw_ref, vw_ref, oc_ref, ow_ref, sel_ref, *, past, lq,
                        nb_real):
    q = q_ref[0]
    nbp = kc_ref.shape[1]
    col = lax.broadcasted_iota(i32, (1, LANES), 1)
    tpos = past + jnp.bitwise_and(col, lq - 1)

    c_new = jnp.sum(cn_ref[0] * a_ref[...], axis=0, keepdims=True).astype(bf16)
    is_new = lax.broadcasted_iota(i32, (nbp, KV_WIDTH), 0) == nb_real - 1
    kc = jnp.where(is_new, c_new[:, :KV_WIDTH], kc_ref[0])
    vc = jnp.where(is_new, c_new[:, KV_WIDTH:], vc_ref[0])
    sc = jnp.dot(kc, q, preferred_element_type=f32)
    blk = lax.broadcasted_iota(i32, (nbp, LANES), 0)
    pc = _masked_softmax_cols(sc, blk * CMP_BLOCK + (CMP_BLOCK - 1) <= tpos, jnp.exp)
    oc_ref[0] = _dot_t0(pc.astype(bf16), vc)

    r = lax.broadcasted_iota(i32, (LANES, LANES), 0)
    cc = lax.broadcasted_iota(i32, (LANES, LANES), 1)
    per_kv = GROUP * lq
    same = jnp.where(r // per_kv == cc // per_kv, jnp.where(r % lq == cc % lq, 1.0, 0.0), 0.0).astype(bf16)
    hi, mid, lo = _split3(pc)
    imp = (jnp.dot(hi, same, preferred_element_type=f32) + jnp.dot(mid, same, preferred_element_type=f32)
           + jnp.dot(lo, same, preferred_element_type=f32))
    cur = jnp.right_shift(tpos, CMP_SHIFT)
    forced = (blk == 0) | (blk == cur) | (blk == cur - 1)
    base = jnp.where(forced, FORCED_SCORE, jnp.where(blk * CMP_BLOCK <= tpos, 0.0, -1.0))
    score = jnp.where(blk < nb_real, base + imp, -3.0)
    sel_ref[0] = _select_bias(score, min(TOP_K, nb_real))

    nwt = kw_ref.shape[1]
    nw = nwt - lq
    s = jnp.dot(kw_ref[0], q, preferred_element_type=f32)
    wpos = past - nw + lax.broadcasted_iota(i32, (nwt, LANES), 0)
    dist = tpos - wpos
    ok = (dist >= 0) & (dist < WINDOW) & (wpos >= 0)
    pw = _masked_softmax_cols(s, ok, jnp.exp)
    ow_ref[0] = _dot_t0(pw.astype(bf16), vw_ref[0])


def _smp_sel_kernel(pt_ref, *refs, n_pg):
    del pt_ref
    pages = refs[:n_pg]
    q_ref, sel_ref, e_ref, kn_ref, vn_ref, bn_ref, g_ref, oc_ref, ow_ref, o_ref, m_ref, l_ref, acc_ref = refs[n_pg:]
    i = pl.program_id(1)

    @pl.when(i == 0)
    def _():
        m_ref[...] = jnp.full(m_ref.shape, NEG, f32)
        l_ref[...] = jnp.zeros(l_ref.shape, f32)
        acc_ref[...] = jnp.zeros(acc_ref.shape, f32)

    def update(s, vt):
        m_new = jnp.maximum(m_ref[...], jnp.max(s, axis=1, keepdims=True))
        a = jnp.exp(m_ref[...] - m_new)
        p = jnp.exp(s - m_new)
        l_ref[...] = a * l_ref[...] + jnp.sum(p, axis=1, keepdims=True)
        acc_ref[...] = a * acc_ref[...] + _dot_nt(p.astype(bf16), vt)
        m_ref[...] = m_new

    qa = jnp.concatenate([q_ref[0], sel_ref[0, 0].astype(bf16)], axis=1)
    kt = jnp.concatenate([r[0, 0, 0].reshape(KV_WIDTH, PAGE_SIZE) for r in pages], axis=1).astype(bf16)
    vt = jnp.concatenate([r[0, 0, 1].reshape(KV_WIDTH, PAGE_SIZE) for r in pages], axis=1).astype(bf16)
    update(jnp.dot(qa, jnp.concatenate([kt, e_ref[...]], axis=0), preferred_element_type=f32), vt)

    @pl.when(i == pl.num_programs(1) - 1)
    def _():
        update(jnp.dot(q_ref[0], kn_ref[0].astype(bf16), preferred_element_type=f32) + bn_ref[0], vn_ref[0].astype(bf16))
        gt = _sigmoid(g_ref[0])
        osel = acc_ref[...] / jnp.maximum(l_ref[...], 1e-30)
        o_ref[0] = gt[:, 0:1] * oc_ref[0] + gt[:, 1:2] * osel + gt[:, 2:3] * ow_ref[0]


def _nsa_sample(x, cache_cmp_t, cache_sel_t, cache_win, page_table, layer, w_in, alpha_rows, n_pg=32):
    b, lq, d = x.shape
    n_pages = page_table.shape[1]
    past = n_pages * PAGE_SIZE
    assert KV_HEADS * GROUP * lq == LANES and past % CMP_BLOCK == 0 and lq <= CMP_BLOCK
    assert n_pages % n_pg == 0 and n_pg * PAGE_SIZE // CMP_BLOCK <= LANES
    p = _matmul(x.reshape(b * lq, d), w_in, b * lq, "nsa_in_proj_s")
    nq = N_HEADS * HEAD_DIM
    w2 = 2 * KV_WIDTH
    q = (p[:, :nq] * HEAD_DIM ** -0.5).reshape(b, lq, KV_HEADS, GROUP, HEAD_DIM)
    eye = jnp.eye(KV_HEADS, dtype=f32)
    qbd = (q.transpose(0, 2, 3, 1, 4)[:, :, :, :, None, :] * eye[None, :, None, None, :, None])
    qbd = qbd.reshape(b, LANES, KV_WIDTH).astype(bf16)
    qbd_t = qbd.transpose(0, 2, 1)
    gt = p[:, nq + 6 * KV_WIDTH:nq + 6 * KV_WIDTH + 3 * N_HEADS].reshape(b, lq, KV_HEADS, GROUP, 3)
    gt = gt.transpose(0, 2, 3, 1, 4).reshape(b, LANES, 3)
    kv = p[:, nq:nq + 6 * KV_WIDTH].reshape(b, lq, 3, w2)
    kv_c, kv_s, kv_w = kv[:, :, 0], kv[:, :, 1], kv[:, :, 2]

    bpp = PAGE_SIZE // CMP_BLOCK
    c_past = _compress_paged(cache_cmp_t, layer, page_table, alpha_rows, n_pg)
    c_past = c_past.transpose(0, 1, 3, 2).reshape(b, past // CMP_BLOCK, w2)
    nb_real = past // CMP_BLOCK + 1
    nbp = -(-nb_real // 8) * 8
    cblk = jnp.pad(c_past, ((0, 0), (0, nbp - nb_real + 1), (0, 0)))
    wkv = jnp.concatenate([cache_win[layer].reshape(b, -1, w2), kv_w], axis=1)
    nwt = wkv.shape[1]

    seq3 = lambda i: (i, 0, 0)
    oc, ow, sel = pl.pallas_call(
        functools.partial(_smp_cmp_win_kernel, past=past, lq=lq, nb_real=nb_real), grid=(b,),
        in_specs=[pl.BlockSpec((1, KV_WIDTH, LANES), seq3), pl.BlockSpec((1, nbp, KV_WIDTH), seq3),
                  pl.BlockSpec((1, nbp, KV_WIDTH), seq3), pl.BlockSpec((1, lq, w2), seq3),
                  pl.BlockSpec((lq, w2), lambda i: (0, 0)), pl.BlockSpec((1, nwt, KV_WIDTH), seq3),
                  pl.BlockSpec((1, nwt, KV_WIDTH), seq3)],
        out_specs=[pl.BlockSpec((1, LANES, KV_WIDTH), seq3), pl.BlockSpec((1, LANES, KV_WIDTH), seq3),
                   pl.BlockSpec((1, nbp, LANES), seq3)],
        out_shape=[jax.ShapeDtypeStruct((b, LANES, KV_WIDTH), f32), jax.ShapeDtypeStruct((b, LANES, KV_WIDTH), f32),
                   jax.ShapeDtypeStruct((b, nbp, LANES), f32)],
        compiler_params=_cp("arbitrary"), name="nsa_sample_cmp_win")(
            qbd_t, cblk[:, :, :KV_WIDTH].astype(bf16), cblk[:, :, KV_WIDTH:].astype(bf16), kv_c, alpha_rows.T[:lq],
            wkv[:, :, :KV_WIDTH].astype(bf16), wkv[:, :, KV_WIDTH:].astype(bf16))

    n_steps, bps = n_pages // n_pg, n_pg * bpp
    sel_r = sel.transpose(0, 2, 1)
    sel_steps = sel_r[:, :, :past // CMP_BLOCK].reshape(b, LANES, n_steps, bps).transpose(0, 2, 1, 3)
    sel_steps = jnp.pad(sel_steps, ((0, 0), (0, 0), (0, 0), (0, LANES - bps)))
    onehot = (jnp.arange(LANES)[:, None] == jnp.arange(n_pg * PAGE_SIZE)[None, :] // CMP_BLOCK).astype(bf16)
    pad_t = lambda a: jnp.pad(a.transpose(0, 2, 1), ((0, 0), (0, 0), (0, LANES - lq)))
    qoff = jnp.arange(LANES) % lq
    readable = jnp.arange(LANES)[None, :] <= qoff[:, None]
    bias_new = jnp.where(readable[None], sel_r[:, :, past // CMP_BLOCK][:, :, None], NEG)

    def page_map(bb, i, pt, k):
        return (layer, pt[bb, i * n_pg + k], 0, 0, 0, 0)

    seq3p = lambda bb, i, pt: (bb, 0, 0)
    grid_spec = pltpu.PrefetchScalarGridSpec(
        num_scalar_prefetch=1, grid=(b, n_steps),
        in_specs=[pl.BlockSpec((1, 1, 2, KV_HEADS, HEAD_DIM, PAGE_SIZE), functools.partial(page_map, k=k))
                  for k in range(n_pg)]
        + [pl.BlockSpec((1, LANES, KV_WIDTH), seq3p), pl.BlockSpec((1, 1, LANES, LANES), lambda bb, i, pt: (bb, i, 0, 0)),
           pl.BlockSpec(onehot.shape, lambda bb, i, pt: (0, 0)),
           pl.BlockSpec((1, KV_WIDTH, LANES), seq3p), pl.BlockSpec((1, KV_WIDTH, LANES), seq3p),
           pl.BlockSpec((1, LANES, LANES), seq3p), pl.BlockSpec((1, LANES, 3), seq3p),
           pl.BlockSpec((1, LANES, KV_WIDTH), seq3p), pl.BlockSpec((1, LANES, KV_WIDTH), seq3p)],
        out_specs=pl.BlockSpec((1, LANES, KV_WIDTH), seq3p),
        scratch_shapes=[pltpu.VMEM((LANES, 1), f32), pltpu.VMEM((LANES, 1), f32), pltpu.VMEM((LANES, KV_WIDTH), f32)])
    o = pl.pallas_call(
        functools.partial(_smp_sel_kernel, n_pg=n_pg), grid_spec=grid_spec,
        out_shape=jax.ShapeDtypeStruct((b, LANES, KV_WIDTH), f32),
        compiler_params=_cp("arbitrary", "arbitrary"), name="nsa_sample_sel")(
            page_table, *([cache_sel_t] * n_pg), qbd, sel_steps, onehot, pad_t(kv_s[:, :, :KV_WIDTH]),
            pad_t(kv_s[:, :, KV_WIDTH:]), bias_new, gt, oc, ow)
    o = o.reshape(b, KV_HEADS, GROUP, lq, KV_HEADS, HEAD_DIM)
    o = jnp.stack([o[:, h, :, :, h] for h in range(KV_HEADS)], axis=1)
    o = o.transpose(0, 3, 1, 2, 4).reshape(b * lq, d)
    return o, kv_c, kv_s, wkv[:, lq:]


def _hgrn_kernel(zq_ref, zf_ref, zi_ref, zg_ref, lb_ref, ng_ref, s0_ref, o_ref, sout_ref, st_ref, *, n_valid, hb):
    t = pl.program_id(2)
    tb = zq_ref.shape[1]
    n_ch = tb // HG_CHUNK

    @pl.when(t == 0)
    def _():
        for hh in range(hb):
            st_ref[hh] = s0_ref[0, hh].T

    r = lax.broadcasted_iota(i32, (tb, tb), 0)
    cc = lax.broadcasted_iota(i32, (tb, tb), 1)
    tri = jnp.where(r // HG_CHUNK == cc // HG_CHUNK, jnp.where(cc <= r, 1.0, 0.0), 0.0).astype(bf16)
    causal = lax.broadcasted_iota(i32, (HG_CHUNK, HG_CHUNK), 1) <= lax.broadcasted_iota(i32, (HG_CHUNK, HG_CHUNK), 0)
    mid_row = HG_CHUNK // 2

    zf = zf_ref[0]
    lb = lb_ref[0]
    la = jnp.log(lb)
    l1 = jnp.log1p(-lb)
    bb = l1 + _log_sigmoid(zf)
    log_f = jnp.maximum(la, bb) + jnp.log1p(jnp.exp(-jnp.abs(la - bb)))
    kk = jnp.exp(l1 + _log_sigmoid(-zf))
    if n_valid < tb:
        live = lax.broadcasted_iota(i32, zf.shape, 0) < n_valid
        log_f = jnp.where(live, log_f, 0.0)
        kk = jnp.where(live, kk, 0.0)
    hi, mid, lo = _split3(log_f)
    cum = (jnp.dot(tri, hi, preferred_element_type=f32) + jnp.dot(tri, mid, preferred_element_type=f32)
           + jnp.dot(tri, lo, preferred_element_type=f32))
    zq, zi, zg = zq_ref[0], zi_ref[0], zg_ref[0]
    gate = ng_ref[0] * (zg * _sigmoid(zg))

    qs, o_in, upd, dec = {}, {}, {}, {}
    for hh in range(hb):
        hs = slice(hh * HG_DK, (hh + 1) * HG_DK)
        for ci in range(n_ch):
            sl = slice(ci * HG_CHUNK, (ci + 1) * HG_CHUNK)
            cm, qi, ki, vi = cum[sl, hs], zq[sl, hs], kk[sl, hs], zi[sl, hs]
            ref = cm[mid_row:mid_row + 1]
            last = cm[HG_CHUNK - 1:HG_CHUNK]
            a = _dot_nt((qi * jnp.exp(cm - ref)).astype(bf16), (ki * jnp.exp(ref - cm)).astype(bf16))
            a = jnp.where(causal, a, 0.0)
            o_in[hh, ci] = jnp.dot(a.astype(bf16), vi.astype(bf16), preferred_element_type=f32)
            qs[hh, ci] = (qi * jnp.exp(cm)).astype(bf16)
            upd[hh, ci] = _dot_t0(vi.astype(bf16), (ki * jnp.exp(last - cm)).astype(bf16))
            dec[hh, ci] = jnp.exp(last)
    for hh in range(hb):
        hs = slice(hh * HG_DK, (hh + 1) * HG_DK)
        st = st_ref[hh]
        outs = []
        for ci in range(n_ch):
            outs.append(o_in[hh, ci] + _dot_nt(qs[hh, ci], st.astype(bf16)))
            st = st * dec[hh, ci] + upd[hh, ci]
        st_ref[hh] = st
        o = jnp.concatenate(outs, axis=0) if n_ch > 1 else outs[0]
        o = o * lax.rsqrt(jnp.mean(o * o, axis=-1, keepdims=True) + RMS_EPS)
        o_ref[0, :, hs] = (o * gate[:, hs]).astype(o_ref.dtype)

        @pl.when(t == pl.num_programs(2) - 1)
        def _():
            sout_ref[0, hh] = st.T


def _hgrn(p, s0, lb, norm_g, tb, n_valid, hb):
    b, lp, _ = p.shape
    ng = HG_HEADS // hb
    w = hb * HG_DK
    col = lambda off: (lambda bb, hh, t: (bb, t, off * ng + hh))
    per_head = lambda bb, hh, t: (0, hh)
    state = lambda bb, hh, t: (bb, hh, 0, 0)
    return pl.pallas_call(
        functools.partial(_hgrn_kernel, n_valid=n_valid, hb=hb), grid=(b, ng, lp // tb),
        in_specs=[pl.BlockSpec((1, tb, w), col(0)), pl.BlockSpec((1, tb, w), col(1)),
                  pl.BlockSpec((1, tb, w), col(2)), pl.BlockSpec((1, tb, w), col(3)),
                  pl.BlockSpec((1, w), per_head), pl.BlockSpec((1, w), per_head),
                  pl.BlockSpec((1, hb, HG_DK, HG_DV), state)],
        out_specs=[pl.BlockSpec((1, tb, w), lambda bb, hh, t: (bb, t, hh)),
                   pl.BlockSpec((1, hb, HG_DK, HG_DV), state)],
        out_shape=[jax.ShapeDtypeStruct((b, lp, HG_HEADS * HG_DV), bf16),
                   jax.ShapeDtypeStruct((b, HG_HEADS, HG_DK, HG_DV), f32)],
        scratch_shapes=[pltpu.VMEM((hb, HG_DV, HG_DK), f32)],
        compiler_params=_cp("arbitrary", "arbitrary", "arbitrary"), name="hgrn_scan")(
            p, p, p, p, lb.reshape(1, -1), norm_g.reshape(1, -1), s0)


def _hgrn_mixer(x, s0, w_in, lb, norm_g, tb):
    b, seq, d = x.shape
    lp = -(-seq // HG_CHUNK) * HG_CHUNK
    tb = min(tb, lp)
    p = _matmul(x.reshape(b * seq, d), w_in, min(256, b * seq), "hgrn_in_proj").reshape(b, seq, 4 * d)
    if lp != seq:
        p = jnp.pad(p, ((0, 0), (0, lp - seq), (0, 0)))
    o, s_out = _hgrn(p, s0, lb, norm_g, tb, seq if lp != seq else lp, HG_HB)
    return o[:, :seq].reshape(b * seq, d), s_out


def _router_logits(x, whi_ref, wlo_ref, b_ref):
    xh = x.astype(bf16)
    xl = (x - xh.astype(f32)).astype(bf16)
    return (jnp.dot(xh, whi_ref[...], preferred_element_type=f32) + jnp.dot(xl, whi_ref[...], preferred_element_type=f32)
            + jnp.dot(xh, wlo_ref[...], preferred_element_type=f32)) + b_ref[...]


def _route(lg):
    ne = N_GROUPS * EXPERTS_PER_GROUP
    lane = lax.broadcasted_iota(i32, lg.shape, 1)
    lanef = lane.astype(f32)
    in_c = (lane >= ne) & (lane < ne + N_GROUPS)
    lc = jnp.where(in_c, lg, -jnp.inf)
    mc = jnp.max(lc, axis=-1, keepdims=True)
    grp = jnp.min(jnp.where(lc == mc, lanef, 1e9), axis=-1, keepdims=True) - float(ne)
    pg = 1.0 / jnp.sum(jnp.where(in_c, jnp.exp(lc - mc), 0.0), axis=-1, keepdims=True)
    e_lo = grp * float(EXPERTS_PER_GROUP)
    in_e = (lanef >= e_lo) & (lanef < e_lo + float(EXPERTS_PER_GROUP))
    le = jnp.where(in_e, lg, -jnp.inf)
    v1 = jnp.max(le, axis=-1, keepdims=True)
    i1 = jnp.min(jnp.where(le == v1, lanef, 1e9), axis=-1, keepdims=True)
    le2 = jnp.where(lanef == i1, -jnp.inf, le)
    v2 = jnp.max(le2, axis=-1, keepdims=True)
    i2 = jnp.min(jnp.where(le2 == v2, lanef, 1e9), axis=-1, keepdims=True)
    e2 = jnp.exp(v2 - v1)
    w1 = pg / (1.0 + e2)
    return jnp.where(lanef == i1, w1, 0.0) + jnp.where(lanef == i2, w1 * e2, 0.0), grp


def _router_kernel(x_ref, whi_ref, wlo_ref, b_ref, cw_ref):
    cw, _ = _route(_router_logits(x_ref[...], whi_ref, wlo_ref, b_ref))
    for g in range(N_GROUPS):
        cw_ref[g] = cw if g == 0 else pltpu.roll(cw, LANES - g * EXPERTS_PER_GROUP, axis=1)


def _router(x, whi, wlo, bias, tm):
    t, d = x.shape
    return pl.pallas_call(
        _router_kernel, grid=(t // tm,),
        in_specs=[pl.BlockSpec((tm, d), lambda i: (i, 0)), pl.BlockSpec((d, LANES), lambda i: (0, 0)),
                  pl.BlockSpec((d, LANES), lambda i: (0, 0)), pl.BlockSpec((1, LANES), lambda i: (0, 0))],
        out_specs=pl.BlockSpec((N_GROUPS, tm, LANES), lambda i: (0, i, 0)),
        out_shape=jax.ShapeDtypeStruct((N_GROUPS, t, LANES), f32), compiler_params=_cp("arbitrary"),
        name="moe_router")(x, whi, wlo, bias)


def _moe_kernel(x_ref, cw_ref, wg_ref, wu_ref, wd_ref, g_ref, b_ref, o_ref, acc_ref):
    gi = pl.program_id(1)

    @pl.when(gi == 0)
    def _():
        acc_ref[...] = jnp.zeros(acc_ref.shape, f32)

    xb = x_ref[...].astype(bf16)
    cw = cw_ref[0]
    acc = acc_ref[...]
    for e in range(EXPERTS_PER_GROUP):
        hg = jnp.dot(xb, wg_ref[0, e], preferred_element_type=f32)
        hu = jnp.dot(xb, wu_ref[0, e], preferred_element_type=f32)
        h = hg * _sigmoid(hg) * hu * cw[:, e:e + 1]
        acc = acc + jnp.dot(h.astype(bf16), wd_ref[0, e], preferred_element_type=f32)
    acc_ref[...] = acc

    @pl.when(gi == pl.num_programs(1) - 1)
    def _():
        o_ref[...] = _layer_norm(ALPHA * x_ref[...] + acc_ref[...], g_ref[...], b_ref[...])


def _moe_ln(x, cw, wg, wu, wd, g, b, tm):
    t, d = x.shape
    e, hid = wg.shape[1], wg.shape[3]
    row = lambda i, gi: (i, 0)
    fix = lambda i, gi: (0, 0)
    return pl.pallas_call(
        _moe_kernel, grid=(t // tm, N_GROUPS),
        in_specs=[pl.BlockSpec((tm, d), row), pl.BlockSpec((1, tm, LANES), lambda i, gi: (gi, i, 0)),
                  pl.BlockSpec((1, e, d, hid), lambda i, gi: (gi, 0, 0, 0)),
                  pl.BlockSpec((1, e, d, hid), lambda i, gi: (gi, 0, 0, 0)),
                  pl.BlockSpec((1, e, hid, d), lambda i, gi: (gi, 0, 0, 0)),
                  pl.BlockSpec((1, d), fix), pl.BlockSpec((1, d), fix)],
        out_specs=pl.BlockSpec((tm, d), row),
        out_shape=jax.ShapeDtypeStruct((t, d), f32),
        scratch_shapes=[pltpu.VMEM((tm, d), f32)],
        compiler_params=_cp("arbitrary", "arbitrary"), name="moe_dense")(
            x, cw, wg, wu, wd, g.reshape(1, d), b.reshape(1, d))


META_GROUP, META_RANK = EXPERTS_PER_GROUP, EXPERTS_PER_GROUP + 1


def _router_sort_kernel(x_ref, whi_ref, wlo_ref, b_ref, xc_ref, cnt_ref, carry_ref):
    @pl.when(pl.program_id(0) == 0)
    def _():
        carry_ref[...] = jnp.zeros(carry_ref.shape, f32)

    x = x_ref[...]
    tm, d = x.shape
    cw, grp = _route(_router_logits(x, whi_ref, wlo_ref, b_ref))
    cw8 = cw
    for g in range(1, N_GROUPS):
        cw8 = jnp.where(grp == float(g), pltpu.roll(cw, LANES - g * EXPERTS_PER_GROUP, axis=1), cw8)
    lanef = lax.broadcasted_iota(i32, (tm, LANES), 1).astype(f32)
    onehot = jnp.where(lanef == grp, 1.0, 0.0)
    before = lax.broadcasted_iota(i32, (tm, tm), 1) < lax.broadcasted_iota(i32, (tm, tm), 0)
    prefix = jnp.dot(before.astype(bf16), onehot.astype(bf16), preferred_element_type=f32) + carry_ref[...]
    rank = jnp.sum(onehot * prefix, axis=-1, keepdims=True)
    carry_ref[...] = carry_ref[...] + jnp.sum(onehot, axis=0, keepdims=True)
    cnt_ref[...] = carry_ref[...]
    xc_ref[:, :d] = x
    xc_ref[:, d:] = cw8 + jnp.where(lanef == float(META_GROUP), grp, 0.0) + jnp.where(lanef == float(META_RANK), rank, 0.0)


def _router_sort(x, whi, wlo, bias, tm):
    t, d = x.shape
    fix = lambda i: (0, 0)
    return pl.pallas_call(
        _router_sort_kernel, grid=(t // tm,),
        in_specs=[pl.BlockSpec((tm, d), lambda i: (i, 0)), pl.BlockSpec((d, LANES), fix), pl.BlockSpec((d, LANES), fix),
                  pl.BlockSpec((1, LANES), fix)],
        out_specs=[pl.BlockSpec((tm, d + LANES), lambda i: (i, 0)), pl.BlockSpec((1, LANES), fix)],
        out_shape=[jax.ShapeDtypeStruct((t, d + LANES), f32), jax.ShapeDtypeStruct((1, LANES), f32)],
        scratch_shapes=[pltpu.VMEM((1, LANES), f32)],
        compiler_params=_cp("arbitrary"), name="moe_router_sort")(x, whi, wlo, bias)


def _row_copy(src, i, dst, j, sem):
    return pltpu.make_async_copy(src.at[pl.ds(i, 1)], dst.at[pl.ds(j, 1)], sem)


def _dispatch_kernel(pos_ref, xc_ref, zero_ref, xs_ref, sem):
    del zero_ref
    tm = xc_ref.shape[0]
    base = pl.program_id(0) * tm

    def start(r, c):
        _row_copy(xc_ref, r, xs_ref, pos_ref[base + r], sem).start()
        return c

    def wait(r, c):
        _row_copy(xc_ref, 0, xs_ref, 0, sem).wait()
        return c

    lax.fori_loop(0, tm, start, 0, unroll=8)
    lax.fori_loop(0, tm, wait, 0, unroll=8)


def _dispatch(xc, pos, n_rows, tm):
    t, w = xc.shape
    grid_spec = pltpu.PrefetchScalarGridSpec(
        num_scalar_prefetch=1, grid=(t // tm,),
        in_specs=[pl.BlockSpec((tm, w), lambda i, pos: (i, 0)), pl.BlockSpec(memory_space=pl.ANY)],
        out_specs=pl.BlockSpec(memory_space=pl.ANY),
        scratch_shapes=[pltpu.SemaphoreType.DMA(())])
    return pl.pallas_call(
        _dispatch_kernel, grid_spec=grid_spec, out_shape=jax.ShapeDtypeStruct((n_rows, w), f32),
        input_output_aliases={2: 0}, compiler_params=_cp("arbitrary"), name="moe_dispatch")(
            pos, xc, jnp.zeros((n_rows, w), f32))


def _moe_sorted_kernel(tg_ref, xs_ref, wg_ref, wu_ref, wd_ref, ys_ref):
    del tg_ref
    d = ys_ref.shape[1]
    xb = xs_ref[:, :d].astype(bf16)
    cw = xs_ref[:, d:]
    acc = jnp.zeros(ys_ref.shape, f32)
    for e in range(EXPERTS_PER_GROUP):
        hg = jnp.dot(xb, wg_ref[0, e], preferred_element_type=f32)
        hu = jnp.dot(xb, wu_ref[0, e], preferred_element_type=f32)
        h = hg * _sigmoid(hg) * hu * cw[:, e:e + 1]
        acc = acc + jnp.dot(h.astype(bf16), wd_ref[0, e], preferred_element_type=f32)
    ys_ref[...] = acc


def _moe_sorted(xs, tile_group, wg, wu, wd, tm):
    n_rows, w = xs.shape
    e, d, hid = wg.shape[1], wg.shape[2], wg.shape[3]
    grp4 = lambda i, tg: (tg[i], 0, 0, 0)
    grid_spec = pltpu.PrefetchScalarGridSpec(
        num_scalar_prefetch=1, grid=(n_rows // tm,),
        in_specs=[pl.BlockSpec((tm, w), lambda i, tg: (i, 0)), pl.BlockSpec((1, e, d, hid), grp4),
                  pl.BlockSpec((1, e, d, hid), grp4), pl.BlockSpec((1, e, hid, d), grp4)],
        out_specs=pl.BlockSpec((tm, d), lambda i, tg: (i, 0)))
    return pl.pallas_call(
        _moe_sorted_kernel, grid_spec=grid_spec, out_shape=jax.ShapeDtypeStruct((n_rows, d), f32),
        compiler_params=_cp("arbitrary"), name="moe_sorted")(tile_group, xs, wg, wu, wd)


def _combine_kernel(pos_ref, x_ref, ys_ref, g_ref, b_ref, o_ref, ybuf, sem):
    tm = x_ref.shape[0]
    i = pl.program_id(0)
    n = pl.num_programs(0)

    def fetch(step, slot):
        def start(r, c):
            _row_copy(ys_ref, pos_ref[step * tm + r], ybuf.at[slot], r, sem.at[slot]).start()
            return c
        lax.fori_loop(0, tm, start, 0, unroll=8)

    @pl.when(i == 0)
    def _():
        fetch(0, 0)

    @pl.when(i + 1 < n)
    def _():
        fetch(i + 1, (i + 1) % 2)

    slot = i % 2

    def wait(r, c):
        _row_copy(ys_ref, 0, ybuf.at[slot], 0, sem.at[slot]).wait()
        return c

    lax.fori_loop(0, tm, wait, 0, unroll=8)
    o_ref[...] = _layer_norm(ALPHA * x_ref[...] + ybuf[slot], g_ref[...], b_ref[...])


def _combine_ln(x, ys, pos, g, b, tm):
    t, d = x.shape
    fix = lambda i, pos: (0, 0)
    grid_spec = pltpu.PrefetchScalarGridSpec(
        num_scalar_prefetch=1, grid=(t // tm,),
        in_specs=[pl.BlockSpec((tm, d), lambda i, pos: (i, 0)), pl.BlockSpec(memory_space=pl.ANY),
                  pl.BlockSpec((1, d), fix), pl.BlockSpec((1, d), fix)],
        out_specs=pl.BlockSpec((tm, d), lambda i, pos: (i, 0)),
        scratch_shapes=[pltpu.VMEM((2, tm, d), f32), pltpu.SemaphoreType.DMA((2,))])
    return pl.pallas_call(
        _combine_kernel, grid_spec=grid_spec, out_shape=jax.ShapeDtypeStruct((t, d), f32),
        compiler_params=_cp("arbitrary"), name="moe_combine_ln")(pos, x, ys, g.reshape(1, d), b.reshape(1, d))


def _moe_ln_sorted(x, whi, wlo, rbias, wg, wu, wd, g, b, tm=512):
    t, d = x.shape
    xc, cnt = _router_sort(x, whi, wlo, rbias, tm)
    grp = xc[:, d + META_GROUP].astype(i32)
    rank = xc[:, d + META_RANK].astype(i32)
    counts = cnt[0, :N_GROUPS].astype(i32)
    padded = (counts + tm - 1) // tm * tm
    ends = jnp.cumsum(padded)
    start = ends - padded
    pos = rank + sum(jnp.where(grp == gi, start[gi], 0) for gi in range(N_GROUPS))
    n_rows = t + N_GROUPS * tm
    tile_start = jnp.arange(n_rows // tm, dtype=i32) * tm
    tile_group = jnp.minimum(sum((tile_start >= ends[gi]).astype(i32) for gi in range(N_GROUPS)), N_GROUPS - 1)
    xs = _dispatch(xc, pos, n_rows, tm)
    ys = _moe_sorted(xs, tile_group, wg, wu, wd, tm)
    return _combine_ln(x, ys, pos, g, b, tm)


def _router_weights(w_rc, b_rc, w_re, b_re):
    d = w_rc.shape[0]
    ne = N_GROUPS * EXPERTS_PER_GROUP
    w = jnp.concatenate([w_re.reshape(d, ne), w_rc, jnp.zeros((d, LANES - ne - N_GROUPS), f32)], axis=1)
    bias = jnp.concatenate([b_re.reshape(ne), b_rc, jnp.zeros((LANES - ne - N_GROUPS,), f32)]).reshape(1, LANES)
    whi = w.astype(bf16)
    wlo = (w - whi.astype(f32)).astype(bf16)
    return whi, wlo, bias


def _token_major(kv_t):
    b, _, n = kv_t.shape
    return kv_t.reshape(b, 2, KV_HEADS, HEAD_DIM, n).transpose(0, 4, 1, 2, 3)


def kernel(x_prompt, x_sample, cache_cmp_kv, cache_sel_kv, cache_win_kv, state_hgrn, page_table, attn_w_in, attn_cmp_alpha, attn_w_out, rec_w_in, rec_lb_logits, rec_norm_g, rec_w_out, ln_g, ln_b, moe_w_router_c, moe_b_router_c, moe_w_router_e, moe_b_router_e, moe_w_gate, moe_w_up, moe_w_down):
    b, seq, d = x_prompt.shape
    bs, lq, _ = x_sample.shape
    tp, ts = b * seq, bs * lq
    lb_all = jnp.cumsum(jax.nn.softmax(rec_lb_logits.astype(f32), axis=0), axis=0)
    hp, hs = x_prompt.reshape(tp, d), x_sample.reshape(ts, d)
    cache_cmp_t = cache_cmp_kv.transpose(0, 1, 3, 4, 5, 2)
    cache_sel_t = cache_sel_kv.transpose(0, 1, 3, 4, 5, 2)
    kv_out = [[] for _ in range(6)]
    rec_out = [[], []]
    for layer in range(DEPTH):
        j = layer // 2
        if layer % 2 == 0:
            nq, n_in = N_HEADS * HEAD_DIM, attn_w_in.shape[2]
            w_all = attn_w_in[j]
            w_in = jnp.pad(w_all, ((0, 0), (0, -(-n_in // LANES) * LANES - n_in))).astype(bf16)
            w_t = w_all.T.astype(bf16)
            alpha_rows = jnp.repeat(attn_cmp_alpha[j], HEAD_DIM, axis=-1).reshape(CMP_BLOCK, 2 * KV_WIDTH).T
            w_out = attn_w_out[j].astype(bf16)
            op, c_p, s_p, w_p = _nsa_prompt(hp.reshape(b, seq, d), w_t, alpha_rows)
            os_, c_s, s_s, w_s = _nsa_sample(hs.reshape(bs, lq, d), cache_cmp_t, cache_sel_t, cache_win_kv,
                                             page_table, j, w_in, alpha_rows)
            kvs = (bs, lq, 2, KV_HEADS, HEAD_DIM)
            n_win = min(WINDOW, seq)
            for lst, val in zip(kv_out, (_token_major(c_p), c_s.reshape(kvs), _token_major(s_p), s_s.reshape(kvs),
                                         _token_major(w_p[:, :, seq - n_win:]),
                                         w_s.reshape(bs, -1, 2, KV_HEADS, HEAD_DIM))):
                lst.append(val)
        else:
            lb = lb_all[layer] - lb_all[0]
            w_in = rec_w_in[j].astype(bf16)
            w_out = rec_w_out[j].astype(bf16)
            op, st_p = _hgrn_mixer(hp.reshape(b, seq, d), jnp.zeros((b, HG_HEADS, HG_DK, HG_DV), f32), w_in, lb,
                                   rec_norm_g[j], 512)
            os_, st_s = _hgrn_mixer(hs.reshape(bs, lq, d), state_hgrn[j], w_in, lb, rec_norm_g[j], 256)
            rec_out[0].append(st_p)
            rec_out[1].append(st_s)
        hp = _proj_ln(op, w_out, hp, ln_g[layer, 0], ln_b[layer, 0], 512, "out_proj_ln")
        hs = _proj_ln(os_, w_out, hs, ln_g[layer, 0], ln_b[layer, 0], ts, "out_proj_ln_s")
        whi, wlo, rbias = _router_weights(moe_w_router_c[layer], moe_b_router_c[layer], moe_w_router_e[layer],
                                          moe_b_router_e[layer])
        wg, wu, wd = moe_w_gate[layer].astype(bf16), moe_w_up[layer].astype(bf16), moe_w_down[layer].astype(bf16)
        hp = _moe_ln_sorted(hp, whi, wlo, rbias, wg, wu, wd, ln_g[layer, 1], ln_b[layer, 1])
        hs = _moe_ln(hs, _router(hs, whi, wlo, rbias, ts), wg, wu, wd, ln_g[layer, 1], ln_b[layer, 1], ts)
    return (hp.reshape(b, seq, d), hs.reshape(bs, lq, d), *[jnp.stack(v) for v in kv_out],
            jnp.stack(rec_out[0]), jnp.stack(rec_out[1]))
```

```python
import functools

import jax
import jax.numpy as jnp
from jax import lax
from jax.experimental import pallas as pl
from jax.experimental.pallas import tpu as pltpu

f32, bf16, i32 = jnp.float32, jnp.bfloat16, jnp.int32

DEPTH = 2
N_HEADS, KV_HEADS, HEAD_DIM = 16, 4, 64
GROUP = N_HEADS // KV_HEADS
KV_WIDTH = KV_HEADS * HEAD_DIM
CMP_BLOCK, TOP_K, WINDOW = 64, 16, 512
FORCED_SCORE = float(GROUP + 1)
PAGE_SIZE = 128
HG_HEADS, HG_DK, HG_DV, HG_CHUNK = 8, 128, 128, 32
N_GROUPS, EXPERTS_PER_GROUP = 4, 8
ALPHA = (2.0 * DEPTH) ** 0.25
LN_EPS, RMS_EPS = 1e-5, 1e-6

NEG = -1e30
LOG2E = 1.4426950408889634
CMP_SHIFT = CMP_BLOCK.bit_length() - 1
assert 1 << CMP_SHIFT == CMP_BLOCK
LANES = 128
ONES_ROWS = 16
HG_HB = 8
VMEM_LIMIT = 56 * 1024 * 1024


def _cp(*sem):
    return pltpu.CompilerParams(dimension_semantics=sem, vmem_limit_bytes=VMEM_LIMIT)


def _sigmoid(x):
    return 1.0 / (1.0 + jnp.exp(-x))


def _log_sigmoid(x):
    return jnp.minimum(x, 0.0) - jnp.log1p(jnp.exp(-jnp.abs(x)))


def _layer_norm(z, g, b):
    mu = jnp.mean(z, axis=-1, keepdims=True)
    d = z - mu
    var = jnp.mean(d * d, axis=-1, keepdims=True)
    return d * lax.rsqrt(var + LN_EPS) * g + b


def _mm_kernel(x_ref, w_ref, o_ref):
    o_ref[...] = jnp.dot(x_ref[...].astype(bf16), w_ref[...], preferred_element_type=f32).astype(o_ref.dtype)


def _matmul(x, w, tm, name):
    m, k = x.shape
    n = w.shape[1]
    return pl.pallas_call(
        _mm_kernel, grid=(m // tm,),
        in_specs=[pl.BlockSpec((tm, k), lambda i: (i, 0)), pl.BlockSpec((k, n), lambda i: (0, 0))],
        out_specs=pl.BlockSpec((tm, n), lambda i: (i, 0)),
        out_shape=jax.ShapeDtypeStruct((m, n), f32), compiler_params=_cp("arbitrary"), name=name)(x, w)


def _proj_ln_kernel(o_ref, w_ref, x_ref, g_ref, b_ref, h_ref):
    y = jnp.dot(o_ref[...].astype(bf16), w_ref[...], preferred_element_type=f32)
    h_ref[...] = _layer_norm(ALPHA * x_ref[...] + y, g_ref[...], b_ref[...])


def _proj_ln(o, w, x, g, b, tm, name):
    m, k = o.shape
    n = w.shape[1]
    row = lambda i: (i, 0)
    fix = lambda i: (0, 0)
    return pl.pallas_call(
        _proj_ln_kernel, grid=(m // tm,),
        in_specs=[pl.BlockSpec((tm, k), row), pl.BlockSpec((k, n), fix), pl.BlockSpec((tm, n), row),
                  pl.BlockSpec((1, n), fix), pl.BlockSpec((1, n), fix)],
        out_specs=pl.BlockSpec((tm, n), row),
        out_shape=jax.ShapeDtypeStruct((m, n), f32), compiler_params=_cp("arbitrary"), name=name)(
            o, w, x, g.reshape(1, n), b.reshape(1, n))


def _split3(x):
    hi = x.astype(bf16)
    r1 = x - hi.astype(f32)
    mid = r1.astype(bf16)
    return hi, mid, (r1 - mid.astype(f32)).astype(bf16)


def _dot_nt(a, b):
    return lax.dot_general(a, b, (((1,), (1,)), ((), ())), preferred_element_type=f32)


def _block_sum_lanes(y, onehot):
    hi = y.astype(bf16)
    return _dot_nt(hi, onehot) + _dot_nt((y - hi.astype(f32)).astype(bf16), onehot)


def _compress_kernel(x_ref, a_ref, e_ref, o_ref):
    o_ref[0, 0] = _block_sum_lanes(x_ref[0, 0] * a_ref[0], e_ref[...])


def _compress_t(kvt, alpha_l, onehot):
    b, ch, hd, seq = kvt.shape
    nbp = onehot.shape[0]
    return pl.pallas_call(
        _compress_kernel, grid=(b, ch),
        in_specs=[pl.BlockSpec((1, 1, hd, seq), lambda i, j: (i, j, 0, 0)),
                  pl.BlockSpec((1, 1, seq), lambda i, j: (j, 0, 0)),
                  pl.BlockSpec((nbp, seq), lambda i, j: (0, 0))],
        out_specs=pl.BlockSpec((1, 1, hd, nbp), lambda i, j: (i, j, 0, 0)),
        out_shape=jax.ShapeDtypeStruct((b, ch, hd, nbp), f32),
        compiler_params=_cp("arbitrary", "arbitrary"), name="nsa_compress")(kvt, alpha_l, onehot)


def _compress_paged_kernel(pt_ref, *refs, n_pg):
    del pt_ref
    pages, a_ref, e_ref, o_ref = refs[:n_pg], refs[n_pg], refs[n_pg + 1], refs[n_pg + 2]
    w = 2 * KV_WIDTH
    x = jnp.concatenate([r[0, 0].reshape(w, PAGE_SIZE) for r in pages], axis=1)
    o_ref[0, 0] = _block_sum_lanes(x * a_ref[...], e_ref[...])


def _compress_paged(cache_t, layer, page_table, alpha_rows, n_pg):
    nb_, n_pages = page_table.shape
    bpp = PAGE_SIZE // CMP_BLOCK
    n_tok = n_pg * PAGE_SIZE
    a_t = jnp.tile(alpha_rows, (1, n_pg * bpp))
    onehot = (jnp.arange(n_pg * bpp)[:, None] == jnp.arange(n_tok)[None, :] // CMP_BLOCK).astype(bf16)

    def page_map(b, i, pt, k):
        return (layer, pt[b, i * n_pg + k], 0, 0, 0, 0)

    fix = lambda b, i, pt: (0, 0)
    grid_spec = pltpu.PrefetchScalarGridSpec(
        num_scalar_prefetch=1, grid=(nb_, n_pages // n_pg),
        in_specs=[pl.BlockSpec((1, 1, 2, KV_HEADS, HEAD_DIM, PAGE_SIZE), functools.partial(page_map, k=k))
                  for k in range(n_pg)]
        + [pl.BlockSpec(a_t.shape, fix), pl.BlockSpec(onehot.shape, fix)],
        out_specs=pl.BlockSpec((1, 1, 2 * KV_WIDTH, n_pg * bpp), lambda b, i, pt: (b, i, 0, 0)))
    return pl.pallas_call(
        functools.partial(_compress_paged_kernel, n_pg=n_pg), grid_spec=grid_spec,
        out_shape=jax.ShapeDtypeStruct((nb_, n_pages // n_pg, 2 * KV_WIDTH, n_pg * bpp), f32),
        compiler_params=_cp("arbitrary", "arbitrary"), name="nsa_compress_paged")(
            page_table, *([cache_t] * n_pg), a_t, onehot)


def _select_bias(score, k_sel):
    rowf = lax.broadcasted_iota(i32, score.shape, 0).astype(f32)
    picked = jnp.zeros(score.shape, f32)
    work = score
    for _ in range(k_sel):
        mx = jnp.max(work, axis=0, keepdims=True)
        first = jnp.min(jnp.where(work == mx, rowf, 1e9), axis=0, keepdims=True)
        pick = rowf == first
        picked = jnp.where(pick, 1.0, picked)
        work = jnp.where(pick, -jnp.inf, work)
    return jnp.where(picked > 0.0, 0.0, NEG)


def _masked_softmax_cols(s, mask, exp_fn):
    sm = jnp.where(mask, s, NEG)
    m = jnp.max(sm, axis=0, keepdims=True)
    e = jnp.where(mask, exp_fn(sm - m), 0.0)
    return e / jnp.maximum(jnp.sum(e, axis=0, keepdims=True), 1e-30)


def _nsa_in_proj_kernel(x_ref, w_ref, q_ref, g_ref, c_ref, s_ref, w_out_ref, *, q_scale):
    pt = _dot_nt(w_ref[...], x_ref[0].astype(bf16))
    nq, w = N_HEADS * HEAD_DIM, 2 * KV_WIDTH
    q_ref[0] = (pt[:nq] * q_scale).astype(q_ref.dtype)
    for n, ref in enumerate((c_ref, s_ref, w_out_ref)):
        ref[0] = pt[nq + n * w:nq + (n + 1) * w]
    g_ref[0] = pt[nq + 3 * w:nq + 3 * w + 3 * N_HEADS]


def _nsa_in_proj(x, w_t, tm, q_scale):
    b, seq, d = x.shape
    nq, w, ng = N_HEADS * HEAD_DIM, 2 * KV_WIDTH, 3 * N_HEADS
    rows = lambda n: pl.BlockSpec((1, n, tm), lambda i, j: (i, 0, j))
    kv_shape = jax.ShapeDtypeStruct((b, w, seq), f32)
    return pl.pallas_call(
        functools.partial(_nsa_in_proj_kernel, q_scale=q_scale), grid=(b, seq // tm),
        in_specs=[pl.BlockSpec((1, tm, d), lambda i, j: (i, j, 0)), pl.BlockSpec(w_t.shape, lambda i, j: (0, 0))],
        out_specs=[rows(nq), rows(ng), rows(w), rows(w), rows(w)],
        out_shape=[jax.ShapeDtypeStruct((b, nq, seq), bf16), jax.ShapeDtypeStruct((b, ng, seq), f32),
                   kv_shape, kv_shape, kv_shape],
        compiler_params=_cp("arbitrary", "arbitrary"), name="nsa_in_proj")(x, w_t)


def _nsa_prompt_kernel(qt_ref, g_ref, kct_ref, vct_ref, kst_ref, vst_ref, kwt_ref, vwt_ref, et_ref, o_ref,
                       m_ref, acc_ref, sa_ref, sb_ref, ka_ref, va_ref, kw_ref, vwa_ref, ow_ref, *, tq, tk, nb):
    s0 = pl.program_id(1) * tq
    c = GROUP * tq
    nbp = et_ref.shape[1]
    qt = jnp.concatenate([qt_ref[0, g * HEAD_DIM:(g + 1) * HEAD_DIM, :] for g in range(GROUP)], axis=1)
    tq_pos = s0 + lax.broadcasted_iota(i32, (1, tq), 1)
    tpos = jnp.concatenate([tq_pos] * GROUP, axis=1)

    kc = kct_ref[0, 0, 0].T.astype(bf16)
    sc = jnp.dot(kc, qt, preferred_element_type=f32)
    blk_c = lax.broadcasted_iota(i32, (nbp, c), 0)
    pc = _masked_softmax_cols(sc, blk_c * CMP_BLOCK + (CMP_BLOCK - 1) <= tpos, jnp.exp2)
    oc = jnp.dot(vct_ref[0, 0, 0].astype(bf16), pc.astype(bf16), preferred_element_type=f32)

    imp = pc[:, 0:tq]
    for g in range(1, GROUP):
        imp = imp + pc[:, g * tq:(g + 1) * tq]
    blk = lax.broadcasted_iota(i32, (nbp, tq), 0)
    cur = jnp.right_shift(tq_pos, CMP_SHIFT)
    forced = (blk == 0) | (blk == cur) | (blk == cur - 1)
    base = jnp.where(forced, FORCED_SCORE, jnp.where(blk * CMP_BLOCK <= tq_pos, 0.0, -1.0))
    sel = _select_bias(jnp.where(blk < nb, base + imp, -3.0), min(TOP_K, nb)).astype(bf16)
    qa = jnp.concatenate([jnp.concatenate([sel] * GROUP, axis=1), qt], axis=0)

    m_ref[...] = jnp.full(m_ref.shape, NEG, f32)
    acc_ref[...] = jnp.zeros(acc_ref.shape, f32)

    @pl.when(pl.program_id(1) == 0)
    def _():
        ones_rows = jnp.ones((ONES_ROWS, tk), bf16)

        def fill(t, carry):
            k0 = pl.multiple_of(t * tk, tk)
            ka_ref[pl.ds(k0, tk), :] = jnp.concatenate(
                [et_ref[pl.ds(k0, tk), :], kst_ref[0, 0, 0, :, pl.ds(k0, tk)].T.astype(bf16)], axis=1)
            va_ref[:, pl.ds(k0, tk)] = jnp.concatenate([vst_ref[0, 0, 0, :, pl.ds(k0, tk)].astype(bf16), ones_rows], axis=0)
            kw_ref[pl.ds(k0, tk), :] = kwt_ref[0, 0, 0, :, pl.ds(k0, tk)].T.astype(bf16)
            vwa_ref[:, pl.ds(k0, tk)] = jnp.concatenate([vwt_ref[0, 0, 0, :, pl.ds(k0, tk)].astype(bf16), ones_rows], axis=0)
            return carry

        lax.fori_loop(0, ka_ref.shape[0] // tk, fill, 0)

    def produce(j, s_ref):
        k0 = pl.multiple_of(j * tk, tk)
        s_ref[...] = jnp.dot(ka_ref[pl.ds(k0, tk), :], qa, preferred_element_type=f32)

    def consume(j, s_ref, width, causal):
        k0 = pl.multiple_of(j * tk, tk)
        s = s_ref[0:width, :]
        if causal:
            tail = jnp.where(k0 + (width - tq) + lax.broadcasted_iota(i32, (tq, c), 0) <= tpos, s[width - tq:], NEG)
            s = tail if width == tq else jnp.concatenate([s[:width - tq], tail], axis=0)
        m_new = jnp.maximum(m_ref[...], jnp.max(s, axis=0, keepdims=True))
        a = jnp.exp2(m_ref[...] - m_new)
        p = jnp.exp2(s - m_new).astype(bf16)
        acc_ref[...] = a * acc_ref[...] + jnp.dot(va_ref[:, pl.ds(k0, width)], p, preferred_element_type=f32)
        m_ref[...] = m_new

    n_full = s0 // tk
    produce(0, sa_ref)

    def pair(i, carry):
        produce(2 * i + 1, sb_ref)
        consume(2 * i, sa_ref, tk, False)
        produce(2 * i + 2, sa_ref)
        consume(2 * i + 1, sb_ref, tk, False)
        return carry

    lax.fori_loop(0, n_full // 2, pair, 0)
    odd = n_full % 2 == 1

    @pl.when(odd)
    def _():
        produce(n_full, sb_ref)
        consume(n_full - 1, sa_ref, tk, False)

    for wi in range(1, tk // tq + 1):
        here = s0 - n_full * tk == (wi - 1) * tq

        @pl.when(here & odd)
        def _():
            consume(n_full, sb_ref, wi * tq, True)

        @pl.when(here & jnp.logical_not(odd))
        def _():
            consume(n_full, sa_ref, wi * tq, True)

    osel = acc_ref[0:HEAD_DIM] / jnp.maximum(acc_ref[HEAD_DIM:HEAD_DIM + 1], 1e-30)

    wt = WINDOW + tq
    k0w = pl.multiple_of(jnp.maximum(s0 - WINDOW, 0), LANES)

    def window(interior):
        s = jnp.dot(kw_ref[pl.ds(k0w, wt), :], qt, preferred_element_type=f32)
        if interior:
            row = lax.broadcasted_iota(i32, (tq, c), 0)
            top = jnp.where(tpos - (k0w + row) < WINDOW, s[:tq], NEG)
            bot = jnp.where(tpos - (k0w + (wt - tq) + row) >= 0, s[wt - tq:], NEG)
            s = jnp.concatenate([top, s[tq:wt - tq], bot], axis=0)
        else:
            dist = tpos - (k0w + lax.broadcasted_iota(i32, (wt, c), 0))
            s = jnp.where(dist >= 0, jnp.where(dist < WINDOW, s, NEG), NEG)
        p = jnp.exp2(s - jnp.max(s, axis=0, keepdims=True)).astype(bf16)
        ow = jnp.dot(vwa_ref[:, pl.ds(k0w, wt)], p, preferred_element_type=f32)
        ow_ref[...] = ow[0:HEAD_DIM] / jnp.maximum(ow[HEAD_DIM:HEAD_DIM + 1], 1e-30)

    @pl.when(s0 >= WINDOW)
    def _():
        window(True)

    @pl.when(s0 < WINDOW)
    def _():
        window(False)

    ow = ow_ref[...]

    gt = _sigmoid(g_ref[0, 0])
    outs = []
    for g in range(GROUP):
        cs = slice(g * tq, (g + 1) * tq)
        og = gt[3 * g:3 * g + 1] * oc[:, cs] + gt[3 * g + 1:3 * g + 2] * osel[:, cs] + gt[3 * g + 2:3 * g + 3] * ow[:, cs]
        outs.append(og.T)
    o_ref[0] = jnp.concatenate(outs, axis=1).astype(o_ref.dtype)


def _nsa_prompt_attention(qt, gt, cblk_t, kvs_t, kvw_t, onehot_t, tq, tk):
    b, nq, seq = qt.shape
    kvh, hd, c = KV_HEADS, HEAD_DIM, GROUP * tq
    nbp = onehot_t.shape[1]
    assert tk % tq == 0 and tq <= WINDOW
    comp = lambda cc: (lambda bh, i: (bh // kvh, cc, bh % kvh, 0, 0))
    rows = lambda n: (pl.BlockSpec((1, 1, 1, hd, n), comp(0)), pl.BlockSpec((1, 1, 1, hd, n), comp(1)))
    return pl.pallas_call(
        functools.partial(_nsa_prompt_kernel, tq=tq, tk=tk, nb=seq // CMP_BLOCK), grid=(b * kvh, seq // tq),
        in_specs=[pl.BlockSpec((1, GROUP * hd, tq), lambda bh, i: (bh // kvh, bh % kvh, i)),
                  pl.BlockSpec((1, 1, GROUP * 3, tq), lambda bh, i: (bh // kvh, bh % kvh, 0, i)),
                  *rows(nbp), *rows(seq), *rows(seq), pl.BlockSpec((seq, nbp), lambda bh, i: (0, 0))],
        out_specs=pl.BlockSpec((1, tq, GROUP * hd), lambda bh, i: (bh // kvh, i, bh % kvh)),
        out_shape=jax.ShapeDtypeStruct((b, seq, nq), bf16),
        scratch_shapes=[pltpu.VMEM((1, c), f32), pltpu.VMEM((hd + ONES_ROWS, c), f32),
                        pltpu.VMEM((tk, c), f32), pltpu.VMEM((tk, c), f32),
                        pltpu.VMEM((seq, nbp + hd), bf16), pltpu.VMEM((hd + ONES_ROWS, seq), bf16),
                        pltpu.VMEM((seq, hd), bf16), pltpu.VMEM((hd + ONES_ROWS, seq), bf16),
                        pltpu.VMEM((hd, c), f32)],
        compiler_params=_cp("arbitrary", "arbitrary"), name="nsa_prompt_attn")(
            qt, gt, cblk_t, cblk_t, kvs_t, kvs_t, kvw_t, kvw_t, onehot_t)


def _nsa_prompt(x, w_t, alpha_rows, tq=256, tk=512):
    b, seq, d = x.shape
    nb = seq // CMP_BLOCK
    qt, gt, kvc_t, kvs_t, kvw_t = _nsa_in_proj(x, w_t, 512, HEAD_DIM ** -0.5 * LOG2E)
    five = lambda a: a.reshape(b, 2, KV_HEADS, HEAD_DIM, -1)
    nbp = -(-nb // LANES) * LANES
    onehot = (jnp.arange(nbp)[:, None] == jnp.arange(seq)[None, :] // CMP_BLOCK).astype(bf16)
    alpha_l = jnp.tile(alpha_rows[::HEAD_DIM], (1, nb)).reshape(2 * KV_HEADS, 1, seq)
    cblk_t = _compress_t(kvc_t.reshape(b, 2 * KV_HEADS, HEAD_DIM, seq), alpha_l, onehot)
    o = _nsa_prompt_attention(qt, gt.reshape(b, KV_HEADS, GROUP * 3, seq), five(cblk_t), five(kvs_t), five(kvw_t),
                              onehot.T, tq, tk)
    return o.reshape(b * seq, d), kvc_t, kvs_t, kvw_t


def _dot_t0(a, b):
    return lax.dot_general(a, b, (((0,), (0,)), ((), ())), preferred_element_type=f32)


def _smp_cmp_win_kernel(q_ref, kc_ref, vc_ref, cn_ref, a_ref, kw_ref, vw_ref, oc_ref, ow_ref, sel_ref, *, past, lq,
                        nb_real):
    q = q_ref[0]
    nbp = kc_ref.shape[1]
    col = lax.broadcasted_iota(i32, (1, LANES), 1)
    tpos = past + jnp.bitwise_and(col, lq - 1)

    c_new = jnp.sum(cn_ref[0] * a_ref[...], axis=0, keepdims=True).astype(bf16)
    is_new = lax.broadcasted_iota(i32, (nbp, KV_WIDTH), 0) == nb_real - 1
    kc = jnp.where(is_new, c_new[:, :KV_WIDTH], kc_ref[0])
    vc = jnp.where(is_new, c_new[:, KV_WIDTH:], vc_ref[0])
    sc = jnp.dot(kc, q, preferred_element_type=f32)
    blk = lax.broadcasted_iota(i32, (nbp, LANES), 0)
    pc = _masked_softmax_cols(sc, blk * CMP_BLOCK + (CMP_BLOCK - 1) <= tpos, jnp.exp)
    oc_ref[0] = _dot_t0(pc.astype(bf16), vc)

    r = lax.broadcasted_iota(i32, (LANES, LANES), 0)
    cc = lax.broadcasted_iota(i32, (LANES, LANES), 1)
    per_kv = GROUP * lq
    same = jnp.where(r // per_kv == cc // per_kv, jnp.where(r % lq == cc % lq, 1.0, 0.0), 0.0).astype(bf16)
    hi, mid, lo = _split3(pc)
    imp = (jnp.dot(hi, same, preferred_element_type=f32) + jnp.dot(mid, same, preferred_element_type=f32)
           + jnp.dot(lo, same, preferred_element_type=f32))
    cur = jnp.right_shift(tpos, CMP_SHIFT)
    forced = (blk == 0) | (blk == cur) | (blk == cur - 1)
    base = jnp.where(forced, FORCED_SCORE, jnp.where(blk * CMP_BLOCK <= tpos, 0.0, -1.0))
    score = jnp.where(blk < nb_real, base + imp, -3.0)
    sel_ref[0] = _select_bias(score, min(TOP_K, nb_real))

    nwt = kw_ref.shape[1]
    nw = nwt - lq
    s = jnp.dot(kw_ref[0], q, preferred_element_type=f32)
    wpos = past - nw + lax.broadcasted_iota(i32, (nwt, LANES), 0)
    dist = tpos - wpos
    ok = (dist >= 0) & (dist < WINDOW) & (wpos >= 0)
    pw = _masked_softmax_cols(s, ok, jnp.exp)
    ow_ref[0] = _dot_t0(pw.astype(bf16), vw_ref[0])


def _smp_sel_kernel(pt_ref, *refs, n_pg):
    del pt_ref
    pages = refs[:n_pg]
    q_ref, sel_ref, e_ref, kn_ref, vn_ref, bn_ref, g_ref, oc_ref, ow_ref, o_ref, m_ref, l_ref, acc_ref = refs[n_pg:]
    i = pl.program_id(1)

    @pl.when(i == 0)
    def _():
        m_ref[...] = jnp.full(m_ref.shape, NEG, f32)
        l_ref[...] = jnp.zeros(l_ref.shape, f32)
        acc_ref[...] = jnp.zeros(acc_ref.shape, f32)

    def update(s, vt):
        m_new = jnp.maximum(m_ref[...], jnp.max(s, axis=1, keepdims=True))
        a = jnp.exp(m_ref[...] - m_new)
        p = jnp.exp(s - m_new)
        l_ref[...] = a * l_ref[...] + jnp.sum(p, axis=1, keepdims=True)
        acc_ref[...] = a * acc_ref[...] + _dot_nt(p.astype(bf16), vt)
        m_ref[...] = m_new

    qa = jnp.concatenate([q_ref[0], sel_ref[0, 0].astype(bf16)], axis=1)
    kt = jnp.concatenate([r[0, 0, 0].reshape(KV_WIDTH, PAGE_SIZE) for r in pages], axis=1).astype(bf16)
    vt = jnp.concatenate([r[0, 0, 1].reshape(KV_WIDTH, PAGE_SIZE) for r in pages], axis=1).astype(bf16)
    update(jnp.dot(qa, jnp.concatenate([kt, e_ref[...]], axis=0), preferred_element_type=f32), vt)

    @pl.when(i == pl.num_programs(1) - 1)
    def _():
        update(jnp.dot(q_ref[0], kn_ref[0].astype(bf16), preferred_element_type=f32) + bn_ref[0], vn_ref[0].astype(bf16))
        gt = _sigmoid(g_ref[0])
        osel = acc_ref[...] / jnp.maximum(l_ref[...], 1e-30)
        o_ref[0] = gt[:, 0:1] * oc_ref[0] + gt[:, 1:2] * osel + gt[:, 2:3] * ow_ref[0]


def _nsa_sample(x, cache_cmp_t, cache_sel_t, cache_win, page_table, layer, w_in, alpha_rows, n_pg=32):
    b, lq, d = x.shape
    n_pages = page_table.shape[1]
    past = n_pages * PAGE_SIZE
    assert KV_HEADS * GROUP * lq == LANES and past % CMP_BLOCK == 0 and lq <= CMP_BLOCK
    assert n_pages % n_pg == 0 and n_pg * PAGE_SIZE // CMP_BLOCK <= LANES
    p = _matmul(x.reshape(b * lq, d), w_in, b * lq, "nsa_in_proj_s")
    nq = N_HEADS * HEAD_DIM
    w2 = 2 * KV_WIDTH
    q = (p[:, :nq] * HEAD_DIM ** -0.5).reshape(b, lq, KV_HEADS, GROUP, HEAD_DIM)
    eye = jnp.eye(KV_HEADS, dtype=f32)
    qbd = (q.transpose(0, 2, 3, 1, 4)[:, :, :, :, None, :] * eye[None, :, None, None, :, None])
    qbd = qbd.reshape(b, LANES, KV_WIDTH).astype(bf16)
    qbd_t = qbd.transpose(0, 2, 1)
    gt = p[:, nq + 6 * KV_WIDTH:nq + 6 * KV_WIDTH + 3 * N_HEADS].reshape(b, lq, KV_HEADS, GROUP, 3)
    gt = gt.transpose(0, 2, 3, 1, 4).reshape(b, LANES, 3)
    kv = p[:, nq:nq + 6 * KV_WIDTH].reshape(b, lq, 3, w2)
    kv_c, kv_s, kv_w = kv[:, :, 0], kv[:, :, 1], kv[:, :, 2]

    bpp = PAGE_SIZE // CMP_BLOCK
    c_past = _compress_paged(cache_cmp_t, layer, page_table, alpha_rows, n_pg)
    c_past = c_past.transpose(0, 1, 3, 2).reshape(b, past // CMP_BLOCK, w2)
    nb_real = past // CMP_BLOCK + 1
    nbp = -(-nb_real // 8) * 8
    cblk = jnp.pad(c_past, ((0, 0), (0, nbp - nb_real + 1), (0, 0)))
    wkv = jnp.concatenate([cache_win[layer].reshape(b, -1, w2), kv_w], axis=1)
    nwt = wkv.shape[1]

    seq3 = lambda i: (i, 0, 0)
    oc, ow, sel = pl.pallas_call(
        functools.partial(_smp_cmp_win_kernel, past=past, lq=lq, nb_real=nb_real), grid=(b,),
        in_specs=[pl.BlockSpec((1, KV_WIDTH, LANES), seq3), pl.BlockSpec((1, nbp, KV_WIDTH), seq3),
                  pl.BlockSpec((1, nbp, KV_WIDTH), seq3), pl.BlockSpec((1, lq, w2), seq3),
                  pl.BlockSpec((lq, w2), lambda i: (0, 0)), pl.BlockSpec((1, nwt, KV_WIDTH), seq3),
                  pl.BlockSpec((1, nwt, KV_WIDTH), seq3)],
        out_specs=[pl.BlockSpec((1, LANES, KV_WIDTH), seq3), pl.BlockSpec((1, LANES, KV_WIDTH), seq3),
                   pl.BlockSpec((1, nbp, LANES), seq3)],
        out_shape=[jax.ShapeDtypeStruct((b, LANES, KV_WIDTH), f32), jax.ShapeDtypeStruct((b, LANES, KV_WIDTH), f32),
                   jax.ShapeDtypeStruct((b, nbp, LANES), f32)],
        compiler_params=_cp("arbitrary"), name="nsa_sample_cmp_win")(
            qbd_t, cblk[:, :, :KV_WIDTH].astype(bf16), cblk[:, :, KV_WIDTH:].astype(bf16), kv_c, alpha_rows.T[:lq],
            wkv[:, :, :KV_WIDTH].astype(bf16), wkv[:, :, KV_WIDTH:].astype(bf16))

    n_steps, bps = n_pages // n_pg, n_pg * bpp
    sel_r = sel.transpose(0, 2, 1)
    sel_steps = sel_r[:, :, :past // CMP_BLOCK].reshape(b, LANES, n_steps, bps).transpose(0, 2, 1, 3)
    sel_steps = jnp.pad(sel_steps, ((0, 0), (0, 0), (0, 0), (0, LANES - bps)))
    onehot = (jnp.arange(LANES)[:, None] == jnp.arange(n_pg * PAGE_SIZE)[None, :] // CMP_BLOCK).astype(bf16)
    pad_t = lambda a: jnp.pad(a.transpose(0, 2, 1), ((0, 0), (0, 0), (0, LANES - lq)))
    qoff = jnp.arange(LANES) % lq
    readable = jnp.arange(LANES)[None, :] <= qoff[:, None]
    bias_new = jnp.where(readable[None], sel_r[:, :, past // CMP_BLOCK][:, :, None], NEG)

    def page_map(bb, i, pt, k):
        return (layer, pt[bb, i * n_pg + k], 0, 0, 0, 0)

    seq3p = lambda bb, i, pt: (bb, 0, 0)
    grid_spec = pltpu.PrefetchScalarGridSpec(
        num_scalar_prefetch=1, grid=(b, n_steps),
        in_specs=[pl.BlockSpec((1, 1, 2, KV_HEADS, HEAD_DIM, PAGE_SIZE), functools.partial(page_map, k=k))
                  for k in range(n_pg)]
        + [pl.BlockSpec((1, LANES, KV_WIDTH), seq3p), pl.BlockSpec((1, 1, LANES, LANES), lambda bb, i, pt: (bb, i, 0, 0)),
           pl.BlockSpec(onehot.shape, lambda bb, i, pt: (0, 0)),
           pl.BlockSpec((1, KV_WIDTH, LANES), seq3p), pl.BlockSpec((1, KV_WIDTH, LANES), seq3p),
           pl.BlockSpec((1, LANES, LANES), seq3p), pl.BlockSpec((1, LANES, 3), seq3p),
           pl.BlockSpec((1, LANES, KV_WIDTH), seq3p), pl.BlockSpec((1, LANES, KV_WIDTH), seq3p)],
        out_specs=pl.BlockSpec((1, LANES, KV_WIDTH), seq3p),
        scratch_shapes=[pltpu.VMEM((LANES, 1), f32), pltpu.VMEM((LANES, 1), f32), pltpu.VMEM((LANES, KV_WIDTH), f32)])
    o = pl.pallas_call(
        functools.partial(_smp_sel_kernel, n_pg=n_pg), grid_spec=grid_spec,
        out_shape=jax.ShapeDtypeStruct((b, LANES, KV_WIDTH), f32),
        compiler_params=_cp("arbitrary", "arbitrary"), name="nsa_sample_sel")(
            page_table, *([cache_sel_t] * n_pg), qbd, sel_steps, onehot, pad_t(kv_s[:, :, :KV_WIDTH]),
            pad_t(kv_s[:, :, KV_WIDTH:]), bias_new, gt, oc, ow)
    o = o.reshape(b, KV_HEADS, GROUP, lq, KV_HEADS, HEAD_DIM)
    o = jnp.stack([o[:, h, :, :, h] for h in range(KV_HEADS)], axis=1)
    o = o.transpose(0, 3, 1, 2, 4).reshape(b * lq, d)
    return o, kv_c, kv_s, wkv[:, lq:]


def _hgrn_kernel(zq_ref, zf_ref, zi_ref, zg_ref, lb_ref, ng_ref, s0_ref, o_ref, sout_ref, st_ref, *, n_valid, hb):
    t = pl.program_id(2)
    tb = zq_ref.shape[1]
    n_ch = tb // HG_CHUNK

    @pl.when(t == 0)
    def _():
        for hh in range(hb):
            st_ref[hh] = s0_ref[0, hh].T

    r = lax.broadcasted_iota(i32, (tb, tb), 0)
    cc = lax.broadcasted_iota(i32, (tb, tb), 1)
    tri = jnp.where(r // HG_CHUNK == cc // HG_CHUNK, jnp.where(cc <= r, 1.0, 0.0), 0.0).astype(bf16)
    causal = lax.broadcasted_iota(i32, (HG_CHUNK, HG_CHUNK), 1) <= lax.broadcasted_iota(i32, (HG_CHUNK, HG_CHUNK), 0)
    mid_row = HG_CHUNK // 2

    zf = zf_ref[0]
    lb = lb_ref[0]
    la = jnp.log(lb)
    l1 = jnp.log1p(-lb)
    bb = l1 + _log_sigmoid(zf)
    log_f = jnp.maximum(la, bb) + jnp.log1p(jnp.exp(-jnp.abs(la - bb)))
    kk = jnp.exp(l1 + _log_sigmoid(-zf))
    if n_valid < tb:
        live = lax.broadcasted_iota(i32, zf.shape, 0) < n_valid
        log_f = jnp.where(live, log_f, 0.0)
        kk = jnp.where(live, kk, 0.0)
    hi, mid, lo = _split3(log_f)
    cum = (jnp.dot(tri, hi, preferred_element_type=f32) + jnp.dot(tri, mid, preferred_element_type=f32)
           + jnp.dot(tri, lo, preferred_element_type=f32))
    zq, zi, zg = zq_ref[0], zi_ref[0], zg_ref[0]
    gate = ng_ref[0] * (zg * _sigmoid(zg))

    qs, o_in, upd, dec = {}, {}, {}, {}
    for hh in range(hb):
        hs = slice(hh * HG_DK, (hh + 1) * HG_DK)
        for ci in range(n_ch):
            sl = slice(ci * HG_CHUNK, (ci + 1) * HG_CHUNK)
            cm, qi, ki, vi = cum[sl, hs], zq[sl, hs], kk[sl, hs], zi[sl, hs]
            ref = cm[mid_row:mid_row + 1]
            last = cm[HG_CHUNK - 1:HG_CHUNK]
            a = _dot_nt((qi * jnp.exp(cm - ref)).astype(bf16), (ki * jnp.exp(ref - cm)).astype(bf16))
            a = jnp.where(causal, a, 0.0)
            o_in[hh, ci] = jnp.dot(a.astype(bf16), vi.astype(bf16), preferred_element_type=f32)
            qs[hh, ci] = (qi * jnp.exp(cm)).astype(bf16)
            upd[hh, ci] = _dot_t0(vi.astype(bf16), (ki * jnp.exp(last - cm)).astype(bf16))
            dec[hh, ci] = jnp.exp(last)
    for hh in range(hb):
        hs = slice(hh * HG_DK, (hh + 1) * HG_DK)
        st = st_ref[hh]
        outs = []
        for ci in range(n_ch):
            outs.append(o_in[hh, ci] + _dot_nt(qs[hh, ci], st.astype(bf16)))
            st = st * dec[hh, ci] + upd[hh, ci]
        st_ref[hh] = st
        o = jnp.concatenate(outs, axis=0) if n_ch > 1 else outs[0]
        o = o * lax.rsqrt(jnp.mean(o * o, axis=-1, keepdims=True) + RMS_EPS)
        o_ref[0, :, hs] = (o * gate[:, hs]).astype(o_ref.dtype)

        @pl.when(t == pl.num_programs(2) - 1)
        def _():
            sout_ref[0, hh] = st.T


def _hgrn(p, s0, lb, norm_g, tb, n_valid, hb):
    b, lp, _ = p.shape
    ng = HG_HEADS // hb
    w = hb * HG_DK
    col = lambda off: (lambda bb, hh, t: (bb, t, off * ng + hh))
    per_head = lambda bb, hh, t: (0, hh)
    state = lambda bb, hh, t: (bb, hh, 0, 0)
    return pl.pallas_call(
        functools.partial(_hgrn_kernel, n_valid=n_valid, hb=hb), grid=(b, ng, lp // tb),
        in_specs=[pl.BlockSpec((1, tb, w), col(0)), pl.BlockSpec((1, tb, w), col(1)),
                  pl.BlockSpec((1, tb, w), col(2)), pl.BlockSpec((1, tb, w), col(3)),
                  pl.BlockSpec((1, w), per_head), pl.BlockSpec((1, w), per_head),
                  pl.BlockSpec((1, hb, HG_DK, HG_DV), state)],
        out_specs=[pl.BlockSpec((1, tb, w), lambda bb, hh, t: (bb, t, hh)),
                   pl.BlockSpec((1, hb, HG_DK, HG_DV), state)],
        out_shape=[jax.ShapeDtypeStruct((b, lp, HG_HEADS * HG_DV), bf16),
                   jax.ShapeDtypeStruct((b, HG_HEADS, HG_DK, HG_DV), f32)],
        scratch_shapes=[pltpu.VMEM((hb, HG_DV, HG_DK), f32)],
        compiler_params=_cp("arbitrary", "arbitrary", "arbitrary"), name="hgrn_scan")(
            p, p, p, p, lb.reshape(1, -1), norm_g.reshape(1, -1), s0)


def _hgrn_mixer(x, s0, w_in, lb, norm_g, tb):
    b, seq, d = x.shape
    lp = -(-seq // HG_CHUNK) * HG_CHUNK
    tb = min(tb, lp)
    p = _matmul(x.reshape(b * seq, d), w_in, min(256, b * seq), "hgrn_in_proj").reshape(b, seq, 4 * d)
    if lp != seq:
        p = jnp.pad(p, ((0, 0), (0, lp - seq), (0, 0)))
    o, s_out = _hgrn(p, s0, lb, norm_g, tb, seq if lp != seq else lp, HG_HB)
    return o[:, :seq].reshape(b * seq, d), s_out


def _router_logits(x, whi_ref, wlo_ref, b_ref):
    xh = x.astype(bf16)
    xl = (x - xh.astype(f32)).astype(bf16)
    return (jnp.dot(xh, whi_ref[...], preferred_element_type=f32) + jnp.dot(xl, whi_ref[...], preferred_element_type=f32)
            + jnp.dot(xh, wlo_ref[...], preferred_element_type=f32)) + b_ref[...]


def _route(lg):
    ne = N_GROUPS * EXPERTS_PER_GROUP
    lane = lax.broadcasted_iota(i32, lg.shape, 1)
    lanef = lane.astype(f32)
    in_c = (lane >= ne) & (lane < ne + N_GROUPS)
    lc = jnp.where(in_c, lg, -jnp.inf)
    mc = jnp.max(lc, axis=-1, keepdims=True)
    grp = jnp.min(jnp.where(lc == mc, lanef, 1e9), axis=-1, keepdims=True) - float(ne)
    pg = 1.0 / jnp.sum(jnp.where(in_c, jnp.exp(lc - mc), 0.0), axis=-1, keepdims=True)
    e_lo = grp * float(EXPERTS_PER_GROUP)
    in_e = (lanef >= e_lo) & (lanef < e_lo + float(EXPERTS_PER_GROUP))
    le = jnp.where(in_e, lg, -jnp.inf)
    v1 = jnp.max(le, axis=-1, keepdims=True)
    i1 = jnp.min(jnp.where(le == v1, lanef, 1e9), axis=-1, keepdims=True)
    le2 = jnp.where(lanef == i1, -jnp.inf, le)
    v2 = jnp.max(le2, axis=-1, keepdims=True)
    i2 = jnp.min(jnp.where(le2 == v2, lanef, 1e9), axis=-1, keepdims=True)
    e2 = jnp.exp(v2 - v1)
    w1 = pg / (1.0 + e2)
    return jnp.where(lanef == i1, w1, 0.0) + jnp.where(lanef == i2, w1 * e2, 0.0), grp


def _router_kernel(x_ref, whi_ref, wlo_ref, b_ref, cw_ref):
    cw, _ = _route(_router_logits(x_ref[...], whi_ref, wlo_ref, b_ref))
    for g in range(N_GROUPS):
        cw_ref[g] = cw if g == 0 else pltpu.roll(cw, LANES - g * EXPERTS_PER_GROUP, axis=1)


def _router(x, whi, wlo, bias, tm):
    t, d = x.shape
    return pl.pallas_call(
        _router_kernel, grid=(t // tm,),
        in_specs=[pl.BlockSpec((tm, d), lambda i: (i, 0)), pl.BlockSpec((d, LANES), lambda i: (0, 0)),
                  pl.BlockSpec((d, LANES), lambda i: (0, 0)), pl.BlockSpec((1, LANES), lambda i: (0, 0))],
        out_specs=pl.BlockSpec((N_GROUPS, tm, LANES), lambda i: (0, i, 0)),
        out_shape=jax.ShapeDtypeStruct((N_GROUPS, t, LANES), f32), compiler_params=_cp("arbitrary"),
        name="moe_router")(x, whi, wlo, bias)


def _moe_kernel(x_ref, cw_ref, wg_ref, wu_ref, wd_ref, g_ref, b_ref, o_ref, acc_ref):
    gi = pl.program_id(1)

    @pl.when(gi == 0)
    def _():
        acc_ref[...] = jnp.zeros(acc_ref.shape, f32)

    xb = x_ref[...].astype(bf16)
    cw = cw_ref[0]
    acc = acc_ref[...]
    for e in range(EXPERTS_PER_GROUP):
        hg = jnp.dot(xb, wg_ref[0, e], preferred_element_type=f32)
        hu = jnp.dot(xb, wu_ref[0, e], preferred_element_type=f32)
        h = hg * _sigmoid(hg) * hu * cw[:, e:e + 1]
        acc = acc + jnp.dot(h.astype(bf16), wd_ref[0, e], preferred_element_type=f32)
    acc_ref[...] = acc

    @pl.when(gi == pl.num_programs(1) - 1)
    def _():
        o_ref[...] = _layer_norm(ALPHA * x_ref[...] + acc_ref[...], g_ref[...], b_ref[...])


def _moe_ln(x, cw, wg, wu, wd, g, b, tm):
    t, d = x.shape
    e, hid = wg.shape[1], wg.shape[3]
    row = lambda i, gi: (i, 0)
    fix = lambda i, gi: (0, 0)
    return pl.pallas_call(
        _moe_kernel, grid=(t // tm, N_GROUPS),
        in_specs=[pl.BlockSpec((tm, d), row), pl.BlockSpec((1, tm, LANES), lambda i, gi: (gi, i, 0)),
                  pl.BlockSpec((1, e, d, hid), lambda i, gi: (gi, 0, 0, 0)),
                  pl.BlockSpec((1, e, d, hid), lambda i, gi: (gi, 0, 0, 0)),
                  pl.BlockSpec((1, e, hid, d), lambda i, gi: (gi, 0, 0, 0)),
                  pl.BlockSpec((1, d), fix), pl.BlockSpec((1, d), fix)],
        out_specs=pl.BlockSpec((tm, d), row),
        out_shape=jax.ShapeDtypeStruct((t, d), f32),
        scratch_shapes=[pltpu.VMEM((tm, d), f32)],
        compiler_params=_cp("arbitrary", "arbitrary"), name="moe_dense")(
            x, cw, wg, wu, wd, g.reshape(1, d), b.reshape(1, d))


META_GROUP, META_RANK = EXPERTS_PER_GROUP, EXPERTS_PER_GROUP + 1


def _router_sort_kernel(x_ref, whi_ref, wlo_ref, b_ref, xc_ref, cnt_ref, carry_ref):
    @pl.when(pl.program_id(0) == 0)
    def _():
        carry_ref[...] = jnp.zeros(carry_ref.shape, f32)

    x = x_ref[...]
    tm, d = x.shape
    cw, grp = _route(_router_logits(x, whi_ref, wlo_ref, b_ref))
    cw8 = cw
    for g in range(1, N_GROUPS):
        cw8 = jnp.where(grp == float(g), pltpu.roll(cw, LANES - g * EXPERTS_PER_GROUP, axis=1), cw8)
    lanef = lax.broadcasted_iota(i32, (tm, LANES), 1).astype(f32)
    onehot = jnp.where(lanef == grp, 1.0, 0.0)
    before = lax.broadcasted_iota(i32, (tm, tm), 1) < lax.broadcasted_iota(i32, (tm, tm), 0)
    prefix = jnp.dot(before.astype(bf16), onehot.astype(bf16), preferred_element_type=f32) + carry_ref[...]
    rank = jnp.sum(onehot * prefix, axis=-1, keepdims=True)
    carry_ref[...] = carry_ref[...] + jnp.sum(onehot, axis=0, keepdims=True)
    cnt_ref[...] = carry_ref[...]
    xc_ref[:, :d] = x
    xc_ref[:, d:] = cw8 + jnp.where(lanef == float(META_GROUP), grp, 0.0) + jnp.where(lanef == float(META_RANK), rank, 0.0)


def _router_sort(x, whi, wlo, bias, tm):
    t, d = x.shape
    fix = lambda i: (0, 0)
    return pl.pallas_call(
        _router_sort_kernel, grid=(t // tm,),
        in_specs=[pl.BlockSpec((tm, d), lambda i: (i, 0)), pl.BlockSpec((d, LANES), fix), pl.BlockSpec((d, LANES), fix),
                  pl.BlockSpec((1, LANES), fix)],
        out_specs=[pl.BlockSpec((tm, d + LANES), lambda i: (i, 0)), pl.BlockSpec((1, LANES), fix)],
        out_shape=[jax.ShapeDtypeStruct((t, d + LANES), f32), jax.ShapeDtypeStruct((1, LANES), f32)],
        scratch_shapes=[pltpu.VMEM((1, LANES), f32)],
        compiler_params=_cp("arbitrary"), name="moe_router_sort")(x, whi, wlo, bias)


def _row_copy(src, i, dst, j, sem):
    return pltpu.make_async_copy(src.at[pl.ds(i, 1)], dst.at[pl.ds(j, 1)], sem)


def _dispatch_kernel(pos_ref, xc_ref, zero_ref, xs_ref, sem):
    del zero_ref
    tm = xc_ref.shape[0]
    base = pl.program_id(0) * tm

    def start(r, c):
        _row_copy(xc_ref, r, xs_ref, pos_ref[base + r], sem).start()
        return c

    def wait(r, c):
        _row_copy(xc_ref, 0, xs_ref, 0, sem).wait()
        return c

    lax.fori_loop(0, tm, start, 0, unroll=8)
    lax.fori_loop(0, tm, wait, 0, unroll=8)


def _dispatch(xc, pos, n_rows, tm):
    t, w = xc.shape
    grid_spec = pltpu.PrefetchScalarGridSpec(
        num_scalar_prefetch=1, grid=(t // tm,),
        in_specs=[pl.BlockSpec((tm, w), lambda i, pos: (i, 0)), pl.BlockSpec(memory_space=pl.ANY)],
        out_specs=pl.BlockSpec(memory_space=pl.ANY),
        scratch_shapes=[pltpu.SemaphoreType.DMA(())])
    return pl.pallas_call(
        _dispatch_kernel, grid_spec=grid_spec, out_shape=jax.ShapeDtypeStruct((n_rows, w), f32),
        input_output_aliases={2: 0}, compiler_params=_cp("arbitrary"), name="moe_dispatch")(
            pos, xc, jnp.zeros((n_rows, w), f32))


def _moe_sorted_kernel(tg_ref, xs_ref, wg_ref, wu_ref, wd_ref, ys_ref):
    del tg_ref
    d = ys_ref.shape[1]
    xb = xs_ref[:, :d].astype(bf16)
    cw = xs_ref[:, d:]
    acc = jnp.zeros(ys_ref.shape, f32)
    for e in range(EXPERTS_PER_GROUP):
        hg = jnp.dot(xb, wg_ref[0, e], preferred_element_type=f32)
        hu = jnp.dot(xb, wu_ref[0, e], preferred_element_type=f32)
        h = hg * _sigmoid(hg) * hu * cw[:, e:e + 1]
        acc = acc + jnp.dot(h.astype(bf16), wd_ref[0, e], preferred_element_type=f32)
    ys_ref[...] = acc


def _moe_sorted(xs, tile_group, wg, wu, wd, tm):
    n_rows, w = xs.shape
    e, d, hid = wg.shape[1], wg.shape[2], wg.shape[3]
    grp4 = lambda i, tg: (tg[i], 0, 0, 0)
    grid_spec = pltpu.PrefetchScalarGridSpec(
        num_scalar_prefetch=1, grid=(n_rows // tm,),
        in_specs=[pl.BlockSpec((tm, w), lambda i, tg: (i, 0)), pl.BlockSpec((1, e, d, hid), grp4),
                  pl.BlockSpec((1, e, d, hid), grp4), pl.BlockSpec((1, e, hid, d), grp4)],
        out_specs=pl.BlockSpec((tm, d), lambda i, tg: (i, 0)))
    return pl.pallas_call(
        _moe_sorted_kernel, grid_spec=grid_spec, out_shape=jax.ShapeDtypeStruct((n_rows, d), f32),
        compiler_params=_cp("arbitrary"), name="moe_sorted")(tile_group, xs, wg, wu, wd)


def _combine_kernel(pos_ref, x_ref, ys_ref, g_ref, b_ref, o_ref, ybuf, sem):
    tm = x_ref.shape[0]
    i = pl.program_id(0)
    n = pl.num_programs(0)

    def fetch(step, slot):
        def start(r, c):
            _row_copy(ys_ref, pos_ref[step * tm + r], ybuf.at[slot], r, sem.at[slot]).start()
            return c
        lax.fori_loop(0, tm, start, 0, unroll=8)

    @pl.when(i == 0)
    def _():
        fetch(0, 0)

    @pl.when(i + 1 < n)
    def _():
        fetch(i + 1, (i + 1) % 2)

    slot = i % 2

    def wait(r, c):
        _row_copy(ys_ref, 0, ybuf.at[slot], 0, sem.at[slot]).wait()
        return c

    lax.fori_loop(0, tm, wait, 0, unroll=8)
    o_ref[...] = _layer_norm(ALPHA * x_ref[...] + ybuf[slot], g_ref[...], b_ref[...])


def _combine_ln(x, ys, pos, g, b, tm):
    t, d = x.shape
    fix = lambda i, pos: (0, 0)
    grid_spec = pltpu.PrefetchScalarGridSpec(
        num_scalar_prefetch=1, grid=(t // tm,),
        in_specs=[pl.BlockSpec((tm, d), lambda i, pos: (i, 0)), pl.BlockSpec(memory_space=pl.ANY),
                  pl.BlockSpec((1, d), fix), pl.BlockSpec((1, d), fix)],
        out_specs=pl.BlockSpec((tm, d), lambda i, pos: (i, 0)),
        scratch_shapes=[pltpu.VMEM((2, tm, d), f32), pltpu.SemaphoreType.DMA((2,))])
    return pl.pallas_call(
        _combine_kernel, grid_spec=grid_spec, out_shape=jax.ShapeDtypeStruct((t, d), f32),
        compiler_params=_cp("arbitrary"), name="moe_combine_ln")(pos, x, ys, g.reshape(1, d), b.reshape(1, d))


def _moe_ln_sorted(x, whi, wlo, rbias, wg, wu, wd, g, b, tm=512):
    t, d = x.shape
    xc, cnt = _router_sort(x, whi, wlo, rbias, tm)
    grp = xc[:, d + META_GROUP].astype(i32)
    rank = xc[:, d + META_RANK].astype(i32)
    counts = cnt[0, :N_GROUPS].astype(i32)
    padded = (counts + tm - 1) // tm * tm
    ends = jnp.cumsum(padded)
    start = ends - padded
    pos = rank + sum(jnp.where(grp == gi, start[gi], 0) for gi in range(N_GROUPS))
    n_rows = t + N_GROUPS * tm
    tile_start = jnp.arange(n_rows // tm, dtype=i32) * tm
    tile_group = jnp.minimum(sum((tile_start >= ends[gi]).astype(i32) for gi in range(N_GROUPS)), N_GROUPS - 1)
    xs = _dispatch(xc, pos, n_rows, tm)
    ys = _moe_sorted(xs, tile_group, wg, wu, wd, tm)
    return _combine_ln(x, ys, pos, g, b, tm)


def _router_weights(w_rc, b_rc, w_re, b_re):
    d = w_rc.shape[0]
    ne = N_GROUPS * EXPERTS_PER_GROUP
    w = jnp.concatenate([w_re.reshape(d, ne), w_rc, jnp.zeros((d, LANES - ne - N_GROUPS), f32)], axis=1)
    bias = jnp.concatenate([b_re.reshape(ne), b_rc, jnp.zeros((LANES - ne - N_GROUPS,), f32)]).reshape(1, LANES)
    whi = w.astype(bf16)
    wlo = (w - whi.astype(f32)).astype(bf16)
    return whi, wlo, bias


def _token_major(kv_t):
    b, _, n = kv_t.shape
    return kv_t.reshape(b, 2, KV_HEADS, HEAD_DIM, n).transpose(0, 4, 1, 2, 3)


def kernel(x_prompt, x_sample, cache_cmp_kv, cache_sel_kv, cache_win_kv, state_hgrn, page_table, attn_w_in, attn_cmp_alpha, attn_w_out, rec_w_in, rec_lb_logits, rec_norm_g, rec_w_out, ln_g, ln_b, moe_w_router_c, moe_b_router_c, moe_w_router_e, moe_b_router_e, moe_w_gate, moe_w_up, moe_w_down):
    b, seq, d = x_prompt.shape
    bs, lq, _ = x_sample.shape
    tp, ts = b * seq, bs * lq
    lb_all = jnp.cumsum(jax.nn.softmax(rec_lb_logits.astype(f32), axis=0), axis=0)
    hp, hs = x_prompt.reshape(tp, d), x_sample.reshape(ts, d)
    cache_cmp_t = cache_cmp_kv.transpose(0, 1, 3, 4, 5, 2)
    cache_sel_t = cache_sel_kv.transpose(0, 1, 3, 4, 5, 2)
    kv_out = [[] for _ in range(6)]
    rec_out = [[], []]
    for layer in range(DEPTH):
        j = layer // 2
        if layer % 2 == 0:
            nq, n_in = N_HEADS * HEAD_DIM, attn_w_in.shape[2]
            w_all = attn_w_in[j]
            w_in = jnp.pad(w_all, ((0, 0), (0, -(-n_in // LANES) * LANES - n_in))).astype(bf16)
            w_t = w_all.T.astype(bf16)
            alpha_rows = jnp.repeat(attn_cmp_alpha[j], HEAD_DIM, axis=-1).reshape(CMP_BLOCK, 2 * KV_WIDTH).T
            w_out = attn_w_out[j].astype(bf16)
            op, c_p, s_p, w_p = _nsa_prompt(hp.reshape(b, seq, d), w_t, alpha_rows)
            os_, c_s, s_s, w_s = _nsa_sample(hs.reshape(bs, lq, d), cache_cmp_t, cache_sel_t, cache_win_kv,
                                             page_table, j, w_in, alpha_rows)
            kvs = (bs, lq, 2, KV_HEADS, HEAD_DIM)
            n_win = min(WINDOW, seq)
            for lst, val in zip(kv_out, (_token_major(c_p), c_s.reshape(kvs), _token_major(s_p), s_s.reshape(kvs),
                                         _token_major(w_p[:, :, seq - n_win:]),
                                         w_s.reshape(bs, -1, 2, KV_HEADS, HEAD_DIM))):
                lst.append(val)
        else:
            lb = lb_all[layer] - lb_all[0]
            w_in = rec_w_in[j].astype(bf16)
            w_out = rec_w_out[j].astype(bf16)
            op, st_p = _hgrn_mixer(hp.reshape(b, seq, d), jnp.zeros((b, HG_HEADS, HG_DK, HG_DV), f32), w_in, lb,
                                   rec_norm_g[j], 512)
            os_, st_s = _hgrn_mixer(hs.reshape(bs, lq, d), state_hgrn[j], w_in, lb, rec_norm_g[j], 256)
            rec_out[0].append(st_p)
            rec_out[1].append(st_s)
        hp = _proj_ln(op, w_out, hp, ln_g[layer, 0], ln_b[layer, 0], 512, "out_proj_ln")
        hs = _proj_ln(os_, w_out, hs, ln_g[layer, 0], ln_b[layer, 0], ts, "out_proj_ln_s")
        whi, wlo, rbias = _router_weights(moe_w_router_c[layer], moe_b_router_c[layer], moe_w_router_e[layer],
                                          moe_b_router_e[layer])
        wg, wu, wd = moe_w_gate[layer].astype(bf16), moe_w_up[layer].astype(bf16), moe_w_down[layer].astype(bf16)
        hp = _moe_ln_sorted(hp, whi, wlo, rbias, wg, wu, wd, ln_g[layer, 1], ln_b[layer, 1])
        hs = _moe_ln(hs, _router(hs, whi, wlo, rbias, ts), wg, wu, wd, ln_g[layer, 1], ln_b[layer, 1], ts)
    return (hp.reshape(b, seq, d), hs.reshape(bs, lq, d), *[jnp.stack(v) for v in kv_out],
            jnp.stack(rec_out[0]), jnp.stack(rec_out[1]))
```

```python
import functools

import jax
import jax.numpy as jnp
from jax import lax
from jax.experimental import pallas as pl
from jax.experimental.pallas import tpu as pltpu

f32, bf16, i32 = jnp.float32, jnp.bfloat16, jnp.int32

DEPTH = 2
N_HEADS, KV_HEADS, HEAD_DIM = 16, 4, 64
GROUP = N_HEADS // KV_HEADS
KV_WIDTH = KV_HEADS * HEAD_DIM
CMP_BLOCK, TOP_K, WINDOW = 64, 16, 512
FORCED_SCORE = float(GROUP + 1)
PAGE_SIZE = 128
HG_HEADS, HG_DK, HG_DV, HG_CHUNK = 8, 128, 128, 32
N_GROUPS, EXPERTS_PER_GROUP = 4, 8
ALPHA = (2.0 * DEPTH) ** 0.25
LN_EPS, RMS_EPS = 1e-5, 1e-6

NEG = -1e30
LOG2E = 1.4426950408889634
CMP_SHIFT = CMP_BLOCK.bit_length() - 1
assert 1 << CMP_SHIFT == CMP_BLOCK
LANES = 128
ONES_ROWS = 16
HG_HB = 8
VMEM_LIMIT = 56 * 1024 * 1024


def _cp(*sem):
    return pltpu.CompilerParams(dimension_semantics=sem, vmem_limit_bytes=VMEM_LIMIT)


def _sigmoid(x):
    return 1.0 / (1.0 + jnp.exp(-x))


def _log_sigmoid(x):
    return jnp.minimum(x, 0.0) - jnp.log1p(jnp.exp(-jnp.abs(x)))


def _layer_norm(z, g, b):
    mu = jnp.mean(z, axis=-1, keepdims=True)
    d = z - mu
    var = jnp.mean(d * d, axis=-1, keepdims=True)
    return d * lax.rsqrt(var + LN_EPS) * g + b


def _mm_kernel(x_ref, w_ref, o_ref):
    o_ref[...] = jnp.dot(x_ref[...].astype(bf16), w_ref[...], preferred_element_type=f32).astype(o_ref.dtype)


def _matmul(x, w, tm, name):
    m, k = x.shape
    n = w.shape[1]
    return pl.pallas_call(
        _mm_kernel, grid=(m // tm,),
        in_specs=[pl.BlockSpec((tm, k), lambda i: (i, 0)), pl.BlockSpec((k, n), lambda i: (0, 0))],
        out_specs=pl.BlockSpec((tm, n), lambda i: (i, 0)),
        out_shape=jax.ShapeDtypeStruct((m, n), f32), compiler_params=_cp("arbitrary"), name=name)(x, w)


def _proj_ln_kernel(o_ref, w_ref, x_ref, g_ref, b_ref, h_ref):
    y = jnp.dot(o_ref[...].astype(bf16), w_ref[...], preferred_element_type=f32)
    h_ref[...] = _layer_norm(ALPHA * x_ref[...] + y, g_ref[...], b_ref[...])


def _proj_ln(o, w, x, g, b, tm, name):
    m, k = o.shape
    n = w.shape[1]
    row = lambda i: (i, 0)
    fix = lambda i: (0, 0)
    return pl.pallas_call(
        _proj_ln_kernel, grid=(m // tm,),
        in_specs=[pl.BlockSpec((tm, k), row), pl.BlockSpec((k, n), fix), pl.BlockSpec((tm, n), row),
                  pl.BlockSpec((1, n), fix), pl.BlockSpec((1, n), fix)],
        out_specs=pl.BlockSpec((tm, n), row),
        out_shape=jax.ShapeDtypeStruct((m, n), f32), compiler_params=_cp("arbitrary"), name=name)(
            o, w, x, g.reshape(1, n), b.reshape(1, n))


def _split3(x):
    hi = x.astype(bf16)
    r1 = x - hi.astype(f32)
    mid = r1.astype(bf16)
    return hi, mid, (r1 - mid.astype(f32)).astype(bf16)


def _dot_nt(a, b):
    return lax.dot_general(a, b, (((1,), (1,)), ((), ())), preferred_element_type=f32)


def _block_sum_lanes(y, onehot):
    hi = y.astype(bf16)
    return _dot_nt(hi, onehot) + _dot_nt((y - hi.astype(f32)).astype(bf16), onehot)


def _compress_kernel(x_ref, a_ref, e_ref, o_ref):
    o_ref[0, 0] = _block_sum_lanes(x_ref[0, 0] * a_ref[0], e_ref[...])


def _compress_t(kvt, alpha_l, onehot):
    b, ch, hd, seq = kvt.shape
    nbp = onehot.shape[0]
    return pl.pallas_call(
        _compress_kernel, grid=(b, ch),
        in_specs=[pl.BlockSpec((1, 1, hd, seq), lambda i, j: (i, j, 0, 0)),
                  pl.BlockSpec((1, 1, seq), lambda i, j: (j, 0, 0)),
                  pl.BlockSpec((nbp, seq), lambda i, j: (0, 0))],
        out_specs=pl.BlockSpec((1, 1, hd, nbp), lambda i, j: (i, j, 0, 0)),
        out_shape=jax.ShapeDtypeStruct((b, ch, hd, nbp), f32),
        compiler_params=_cp("arbitrary", "arbitrary"), name="nsa_compress")(kvt, alpha_l, onehot)


def _compress_paged_kernel(pt_ref, *refs, n_pg):
    del pt_ref
    pages, a_ref, e_ref, o_ref = refs[:n_pg], refs[n_pg], refs[n_pg + 1], refs[n_pg + 2]
    w = 2 * KV_WIDTH
    x = jnp.concatenate([r[0, 0].reshape(w, PAGE_SIZE) for r in pages], axis=1)
    o_ref[0, 0] = _block_sum_lanes(x * a_ref[...], e_ref[...])


def _compress_paged(cache_t, layer, page_table, alpha_rows, n_pg):
    nb_, n_pages = page_table.shape
    bpp = PAGE_SIZE // CMP_BLOCK
    n_tok = n_pg * PAGE_SIZE
    a_t = jnp.tile(alpha_rows, (1, n_pg * bpp))
    onehot = (jnp.arange(n_pg * bpp)[:, None] == jnp.arange(n_tok)[None, :] // CMP_BLOCK).astype(bf16)

    def page_map(b, i, pt, k):
        return (layer, pt[b, i * n_pg + k], 0, 0, 0, 0)

    fix = lambda b, i, pt: (0, 0)
    grid_spec = pltpu.PrefetchScalarGridSpec(
        num_scalar_prefetch=1, grid=(nb_, n_pages // n_pg),
        in_specs=[pl.BlockSpec((1, 1, 2, KV_HEADS, HEAD_DIM, PAGE_SIZE), functools.partial(page_map, k=k))
                  for k in range(n_pg)]
        + [pl.BlockSpec(a_t.shape, fix), pl.BlockSpec(onehot.shape, fix)],
        out_specs=pl.BlockSpec((1, 1, 2 * KV_WIDTH, n_pg * bpp), lambda b, i, pt: (b, i, 0, 0)))
    return pl.pallas_call(
        functools.partial(_compress_paged_kernel, n_pg=n_pg), grid_spec=grid_spec,
        out_shape=jax.ShapeDtypeStruct((nb_, n_pages // n_pg, 2 * KV_WIDTH, n_pg * bpp), f32),
        compiler_params=_cp("arbitrary", "arbitrary"), name="nsa_compress_paged")(
            page_table, *([cache_t] * n_pg), a_t, onehot)


def _select_bias(score, k_sel):
    rowf = lax.broadcasted_iota(i32, score.shape, 0).astype(f32)
    picked = jnp.zeros(score.shape, f32)
    work = score
    for _ in range(k_sel):
        mx = jnp.max(work, axis=0, keepdims=True)
        first = jnp.min(jnp.where(work == mx, rowf, 1e9), axis=0, keepdims=True)
        pick = rowf == first
        picked = jnp.where(pick, 1.0, picked)
        work = jnp.where(pick, -jnp.inf, work)
    return jnp.where(picked > 0.0, 0.0, NEG)


def _masked_softmax_cols(s, mask, exp_fn):
    sm = jnp.where(mask, s, NEG)
    m = jnp.max(sm, axis=0, keepdims=True)
    e = jnp.where(mask, exp_fn(sm - m), 0.0)
    return e / jnp.maximum(jnp.sum(e, axis=0, keepdims=True), 1e-30)


def _nsa_in_proj_kernel(x_ref, w_ref, q_ref, g_ref, c_ref, s_ref, w_out_ref, *, q_scale):
    pt = _dot_nt(w_ref[...], x_ref[0].astype(bf16))
    nq, w = N_HEADS * HEAD_DIM, 2 * KV_WIDTH
    q_ref[0] = (pt[:nq] * q_scale).astype(q_ref.dtype)
    for n, ref in enumerate((c_ref, s_ref, w_out_ref)):
        ref[0] = pt[nq + n * w:nq + (n + 1) * w]
    g_ref[0] = pt[nq + 3 * w:nq + 3 * w + 3 * N_HEADS]


def _nsa_in_proj(x, w_t, tm, q_scale):
    b, seq, d = x.shape
    nq, w, ng = N_HEADS * HEAD_DIM, 2 * KV_WIDTH, 3 * N_HEADS
    rows = lambda n: pl.BlockSpec((1, n, tm), lambda i, j: (i, 0, j))
    kv_shape = jax.ShapeDtypeStruct((b, w, seq), f32)
    return pl.pallas_call(
        functools.partial(_nsa_in_proj_kernel, q_scale=q_scale), grid=(b, seq // tm),
        in_specs=[pl.BlockSpec((1, tm, d), lambda i, j: (i, j, 0)), pl.BlockSpec(w_t.shape, lambda i, j: (0, 0))],
        out_specs=[rows(nq), rows(ng), rows(w), rows(w), rows(w)],
        out_shape=[jax.ShapeDtypeStruct((b, nq, seq), bf16), jax.ShapeDtypeStruct((b, ng, seq), f32),
                   kv_shape, kv_shape, kv_shape],
        compiler_params=_cp("arbitrary", "arbitrary"), name="nsa_in_proj")(x, w_t)


def _nsa_prompt_kernel(qt_ref, g_ref, kct_ref, vct_ref, kst_ref, vst_ref, kwt_ref, vwt_ref, et_ref, o_ref,
                       m_ref, acc_ref, sa_ref, sb_ref, ka_ref, va_ref, kw_ref, vwa_ref, ow_ref, *, tq, tk, nb):
    s0 = pl.program_id(1) * tq
    c = GROUP * tq
    nbp = et_ref.shape[1]
    qt = jnp.concatenate([qt_ref[0, g * HEAD_DIM:(g + 1) * HEAD_DIM, :] for g in range(GROUP)], axis=1)
    tq_pos = s0 + lax.broadcasted_iota(i32, (1, tq), 1)
    tpos = jnp.concatenate([tq_pos] * GROUP, axis=1)

    kc = kct_ref[0, 0, 0].T.astype(bf16)
    sc = jnp.dot(kc, qt, preferred_element_type=f32)
    blk_c = lax.broadcasted_iota(i32, (nbp, c), 0)
    pc = _masked_softmax_cols(sc, blk_c * CMP_BLOCK + (CMP_BLOCK - 1) <= tpos, jnp.exp2)
    oc = jnp.dot(vct_ref[0, 0, 0].astype(bf16), pc.astype(bf16), preferred_element_type=f32)

    imp = pc[:, 0:tq]
    for g in range(1, GROUP):
        imp = imp + pc[:, g * tq:(g + 1) * tq]
    blk = lax.broadcasted_iota(i32, (nbp, tq), 0)
    cur = jnp.right_shift(tq_pos, CMP_SHIFT)
    forced = (blk == 0) | (blk == cur) | (blk == cur - 1)
    base = jnp.where(forced, FORCED_SCORE, jnp.where(blk * CMP_BLOCK <= tq_pos, 0.0, -1.0))
    sel = _select_bias(jnp.where(blk < nb, base + imp, -3.0), min(TOP_K, nb)).astype(bf16)
    qa = jnp.concatenate([jnp.concatenate([sel] * GROUP, axis=1), qt], axis=0)

    m_ref[...] = jnp.full(m_ref.shape, NEG, f32)
    acc_ref[...] = jnp.zeros(acc_ref.shape, f32)

    @pl.when(pl.program_id(1) == 0)
    def _():
        ones_rows = jnp.ones((ONES_ROWS, tk), bf16)

        def fill(t, carry):
            k0 = pl.multiple_of(t * tk, tk)
            ka_ref[pl.ds(k0, tk), :] = jnp.concatenate(
                [et_ref[pl.ds(k0, tk), :], kst_ref[0, 0, 0, :, pl.ds(k0, tk)].T.astype(bf16)], axis=1)
            va_ref[:, pl.ds(k0, tk)] = jnp.concatenate([vst_ref[0, 0, 0, :, pl.ds(k0, tk)].astype(bf16), ones_rows], axis=0)
            kw_ref[pl.ds(k0, tk), :] = kwt_ref[0, 0, 0, :, pl.ds(k0, tk)].T.astype(bf16)
            vwa_ref[:, pl.ds(k0, tk)] = jnp.concatenate([vwt_ref[0, 0, 0, :, pl.ds(k0, tk)].astype(bf16), ones_rows], axis=0)
            return carry

        lax.fori_loop(0, ka_ref.shape[0] // tk, fill, 0)

    def produce(j, s_ref):
        k0 = pl.multiple_of(j * tk, tk)
        s_ref[...] = jnp.dot(ka_ref[pl.ds(k0, tk), :], qa, preferred_element_type=f32)

    def consume(j, s_ref, width, causal):
        k0 = pl.multiple_of(j * tk, tk)
        s = s_ref[0:width, :]
        if causal:
            tail = jnp.where(k0 + (width - tq) + lax.broadcasted_iota(i32, (tq, c), 0) <= tpos, s[width - tq:], NEG)
            s = tail if width == tq else jnp.concatenate([s[:width - tq], tail], axis=0)
        m_new = jnp.maximum(m_ref[...], jnp.max(s, axis=0, keepdims=True))
        a = jnp.exp2(m_ref[...] - m_new)
        p = jnp.exp2(s - m_new).astype(bf16)
        acc_ref[...] = a * acc_ref[...] + jnp.dot(va_ref[:, pl.ds(k0, width)], p, preferred_element_type=f32)
        m_ref[...] = m_new

    n_full = s0 // tk
    produce(0, sa_ref)

    def pair(i, carry):
        produce(2 * i + 1, sb_ref)
        consume(2 * i, sa_ref, tk, False)
        produce(2 * i + 2, sa_ref)
        consume(2 * i + 1, sb_ref, tk, False)
        return carry

    lax.fori_loop(0, n_full // 2, pair, 0)
    odd = n_full % 2 == 1

    @pl.when(odd)
    def _():
        produce(n_full, sb_ref)
        consume(n_full - 1, sa_ref, tk, False)

    for wi in range(1, tk // tq + 1):
        here = s0 - n_full * tk == (wi - 1) * tq

        @pl.when(here & odd)
        def _():
            consume(n_full, sb_ref, wi * tq, True)

        @pl.when(here & jnp.logical_not(odd))
        def _():
            consume(n_full, sa_ref, wi * tq, True)

    osel = acc_ref[0:HEAD_DIM] / jnp.maximum(acc_ref[HEAD_DIM:HEAD_DIM + 1], 1e-30)

    wt = WINDOW + tq
    k0w = pl.multiple_of(jnp.maximum(s0 - WINDOW, 0), LANES)

    def window(interior):
        s = jnp.dot(kw_ref[pl.ds(k0w, wt), :], qt, preferred_element_type=f32)
        if interior:
            row = lax.broadcasted_iota(i32, (tq, c), 0)
            top = jnp.where(tpos - (k0w + row) < WINDOW, s[:tq], NEG)
            bot = jnp.where(tpos - (k0w + (wt - tq) + row) >= 0, s[wt - tq:], NEG)
            s = jnp.concatenate([top, s[tq:wt - tq], bot], axis=0)
        else:
            dist = tpos - (k0w + lax.broadcasted_iota(i32, (wt, c), 0))
            s = jnp.where(dist >= 0, jnp.where(dist < WINDOW, s, NEG), NEG)
        p = jnp.exp2(s - jnp.max(s, axis=0, keepdims=True)).astype(bf16)
        ow = jnp.dot(vwa_ref[:, pl.ds(k0w, wt)], p, preferred_element_type=f32)
        ow_ref[...] = ow[0:HEAD_DIM] / jnp.maximum(ow[HEAD_DIM:HEAD_DIM + 1], 1e-30)

    @pl.when(s0 >= WINDOW)
    def _():
        window(True)

    @pl.when(s0 < WINDOW)
    def _():
        window(False)

    ow = ow_ref[...]

    gt = _sigmoid(g_ref[0, 0])
    outs = []
    for g in range(GROUP):
        cs = slice(g * tq, (g + 1) * tq)
        og = gt[3 * g:3 * g + 1] * oc[:, cs] + gt[3 * g + 1:3 * g + 2] * osel[:, cs] + gt[3 * g + 2:3 * g + 3] * ow[:, cs]
        outs.append(og.T)
    o_ref[0] = jnp.concatenate(outs, axis=1).astype(o_ref.dtype)


def _nsa_prompt_attention(qt, gt, cblk_t, kvs_t, kvw_t, onehot_t, tq, tk):
    b, nq, seq = qt.shape
    kvh, hd, c = KV_HEADS, HEAD_DIM, GROUP * tq
    nbp = onehot_t.shape[1]
    assert tk % tq == 0 and tq <= WINDOW
    comp = lambda cc: (lambda bh, i: (bh // kvh, cc, bh % kvh, 0, 0))
    rows = lambda n: (pl.BlockSpec((1, 1, 1, hd, n), comp(0)), pl.BlockSpec((1, 1, 1, hd, n), comp(1)))
    return pl.pallas_call(
        functools.partial(_nsa_prompt_kernel, tq=tq, tk=tk, nb=seq // CMP_BLOCK), grid=(b * kvh, seq // tq),
        in_specs=[pl.BlockSpec((1, GROUP * hd, tq), lambda bh, i: (bh // kvh, bh % kvh, i)),
                  pl.BlockSpec((1, 1, GROUP * 3, tq), lambda bh, i: (bh // kvh, bh % kvh, 0, i)),
                  *rows(nbp), *rows(seq), *rows(seq), pl.BlockSpec((seq, nbp), lambda bh, i: (0, 0))],
        out_specs=pl.BlockSpec((1, tq, GROUP * hd), lambda bh, i: (bh // kvh, i, bh % kvh)),
        out_shape=jax.ShapeDtypeStruct((b, seq, nq), bf16),
        scratch_shapes=[pltpu.VMEM((1, c), f32), pltpu.VMEM((hd + ONES_ROWS, c), f32),
                        pltpu.VMEM((tk, c), f32), pltpu.VMEM((tk, c), f32),
                        pltpu.VMEM((seq, nbp + hd), bf16), pltpu.VMEM((hd + ONES_ROWS, seq), bf16),
                        pltpu.VMEM((seq, hd), bf16), pltpu.VMEM((hd + ONES_ROWS, seq), bf16),
                        pltpu.VMEM((hd, c), f32)],
        compiler_params=_cp("arbitrary", "arbitrary"), name="nsa_prompt_attn")(
            qt, gt, cblk_t, cblk_t, kvs_t, kvs_t, kvw_t, kvw_t, onehot_t)


def _nsa_prompt(x, w_t, alpha_rows, tq=256, tk=512):
    b, seq, d = x.shape
    nb = seq // CMP_BLOCK
    qt, gt, kvc_t, kvs_t, kvw_t = _nsa_in_proj(x, w_t, 512, HEAD_DIM ** -0.5 * LOG2E)
    five = lambda a: a.reshape(b, 2, KV_HEADS, HEAD_DIM, -1)
    nbp = -(-nb // LANES) * LANES
    onehot = (jnp.arange(nbp)[:, None] == jnp.arange(seq)[None, :] // CMP_BLOCK).astype(bf16)
    alpha_l = jnp.tile(alpha_rows[::HEAD_DIM], (1, nb)).reshape(2 * KV_HEADS, 1, seq)
    cblk_t = _compress_t(kvc_t.reshape(b, 2 * KV_HEADS, HEAD_DIM, seq), alpha_l, onehot)
    o = _nsa_prompt_attention(qt, gt.reshape(b, KV_HEADS, GROUP * 3, seq), five(cblk_t), five(kvs_t), five(kvw_t),
                              onehot.T, tq, tk)
    return o.reshape(b * seq, d), kvc_t, kvs_t, kvw_t


def _dot_t0(a, b):
    return lax.dot_general(a, b, (((0,), (0,)), ((), ())), preferred_element_type=f32)


def _smp_cmp_win_kernel(q_ref, kc_ref, vc_ref, cn_ref, a_ref, kw_ref, vw_ref, oc_ref, ow_ref, sel_ref, *, past, lq,
                        nb_real):
    q = q_ref[0]
    nbp = kc_ref.shape[1]
    col = lax.broadcasted_iota(i32, (1, LANES), 1)
    tpos = past + jnp.bitwise_and(col, lq - 1)

    c_new = jnp.sum(cn_ref[0] * a_ref[...], axis=0, keepdims=True).astype(bf16)
    is_new = lax.broadcasted_iota(i32, (nbp, KV_WIDTH), 0) == nb_real - 1
    kc = jnp.where(is_new, c_new[:, :KV_WIDTH], kc_ref[0])
    vc = jnp.where(is_new, c_new[:, KV_WIDTH:], vc_ref[0])
    sc = jnp.dot(kc, q, preferred_element_type=f32)
    blk = lax.broadcasted_iota(i32, (nbp, LANES), 0)
    pc = _masked_softmax_cols(sc, blk * CMP_BLOCK + (CMP_BLOCK - 1) <= tpos, jnp.exp)
    oc_ref[0] = _dot_t0(pc.astype(bf16), vc)

    r = lax.broadcasted_iota(i32, (LANES, LANES), 0)
    cc = lax.broadcasted_iota(i32, (LANES, LANES), 1)
    per_kv = GROUP * lq
    same = jnp.where(r // per_kv == cc // per_kv, jnp.where(r % lq == cc % lq, 1.0, 0.0), 0.0).astype(bf16)
    hi, mid, lo = _split3(pc)
    imp = (jnp.dot(hi, same, preferred_element_type=f32) + jnp.dot(mid, same, preferred_element_type=f32)
           + jnp.dot(lo, same, preferred_element_type=f32))
    cur = jnp.right_shift(tpos, CMP_SHIFT)
    forced = (blk == 0) | (blk == cur) | (blk == cur - 1)
    base = jnp.where(forced, FORCED_SCORE, jnp.where(blk * CMP_BLOCK <= tpos, 0.0, -1.0))
    score = jnp.where(blk < nb_real, base + imp, -3.0)
    sel_ref[0] = _select_bias(score, min(TOP_K, nb_real))

    nwt = kw_ref.shape[1]
    nw = nwt - lq
    s = jnp.dot(kw_ref[0], q, preferred_element_type=f32)
    wpos = past - nw + lax.broadcasted_iota(i32, (nwt, LANES), 0)
    dist = tpos - wpos
    ok = (dist >= 0) & (dist < WINDOW) & (wpos >= 0)
    pw = _masked_softmax_cols(s, ok, jnp.exp)
    ow_ref[0] = _dot_t0(pw.astype(bf16), vw_ref[0])


def _smp_sel_kernel(pt_ref, *refs, n_pg):
    del pt_ref
    pages = refs[:n_pg]
    q_ref, sel_ref, e_ref, kn_ref, vn_ref, bn_ref, g_ref, oc_ref, ow_ref, o_ref, m_ref, l_ref, acc_ref = refs[n_pg:]
    i = pl.program_id(1)

    @pl.when(i == 0)
    def _():
        m_ref[...] = jnp.full(m_ref.shape, NEG, f32)
        l_ref[...] = jnp.zeros(l_ref.shape, f32)
        acc_ref[...] = jnp.zeros(acc_ref.shape, f32)

    def update(s, vt):
        m_new = jnp.maximum(m_ref[...], jnp.max(s, axis=1, keepdims=True))
        a = jnp.exp(m_ref[...] - m_new)
        p = jnp.exp(s - m_new)
        l_ref[...] = a * l_ref[...] + jnp.sum(p, axis=1, keepdims=True)
        acc_ref[...] = a * acc_ref[...] + _dot_nt(p.astype(bf16), vt)
        m_ref[...] = m_new

    qa = jnp.concatenate([q_ref[0], sel_ref[0, 0].astype(bf16)], axis=1)
    kt = jnp.concatenate([r[0, 0, 0].reshape(KV_WIDTH, PAGE_SIZE) for r in pages], axis=1).astype(bf16)
    vt = jnp.concatenate([r[0, 0, 1].reshape(KV_WIDTH, PAGE_SIZE) for r in pages], axis=1).astype(bf16)
    update(jnp.dot(qa, jnp.concatenate([kt, e_ref[...]], axis=0), preferred_element_type=f32), vt)

    @pl.when(i == pl.num_programs(1) - 1)
    def _():
        update(jnp.dot(q_ref[0], kn_ref[0].astype(bf16), preferred_element_type=f32) + bn_ref[0], vn_ref[0].astype(bf16))
        gt = _sigmoid(g_ref[0])
        osel = acc_ref[...] / jnp.maximum(l_ref[...], 1e-30)
        o_ref[0] = gt[:, 0:1] * oc_ref[0] + gt[:, 1:2] * osel + gt[:, 2:3] * ow_ref[0]


def _nsa_sample(x, cache_cmp_t, cache_sel_t, cache_win, page_table, layer, w_in, alpha_rows, n_pg=32):
    b, lq, d = x.shape
    n_pages = page_table.shape[1]
    past = n_pages * PAGE_SIZE
    assert KV_HEADS * GROUP * lq == LANES and past % CMP_BLOCK == 0 and lq <= CMP_BLOCK
    assert n_pages % n_pg == 0 and n_pg * PAGE_SIZE // CMP_BLOCK <= LANES
    p = _matmul(x.reshape(b * lq, d), w_in, b * lq, "nsa_in_proj_s")
    nq = N_HEADS * HEAD_DIM
    w2 = 2 * KV_WIDTH
    q = (p[:, :nq] * HEAD_DIM ** -0.5).reshape(b, lq, KV_HEADS, GROUP, HEAD_DIM)
    eye = jnp.eye(KV_HEADS, dtype=f32)
    qbd = (q.transpose(0, 2, 3, 1, 4)[:, :, :, :, None, :] * eye[None, :, None, None, :, None])
    qbd = qbd.reshape(b, LANES, KV_WIDTH).astype(bf16)
    qbd_t = qbd.transpose(0, 2, 1)
    gt = p[:, nq + 6 * KV_WIDTH:nq + 6 * KV_WIDTH + 3 * N_HEADS].reshape(b, lq, KV_HEADS, GROUP, 3)
    gt = gt.transpose(0, 2, 3, 1, 4).reshape(b, LANES, 3)
    kv = p[:, nq:nq + 6 * KV_WIDTH].reshape(b, lq, 3, w2)
    kv_c, kv_s, kv_w = kv[:, :, 0], kv[:, :, 1], kv[:, :, 2]

    bpp = PAGE_SIZE // CMP_BLOCK
    c_past = _compress_paged(cache_cmp_t, layer, page_table, alpha_rows, n_pg)
    c_past = c_past.transpose(0, 1, 3, 2).reshape(b, past // CMP_BLOCK, w2)
    nb_real = past // CMP_BLOCK + 1
    nbp = -(-nb_real // 8) * 8
    cblk = jnp.pad(c_past, ((0, 0), (0, nbp - nb_real + 1), (0, 0)))
    wkv = jnp.concatenate([cache_win[layer].reshape(b, -1, w2), kv_w], axis=1)
    nwt = wkv.shape[1]

    seq3 = lambda i: (i, 0, 0)
    oc, ow, sel = pl.pallas_call(
        functools.partial(_smp_cmp_win_kernel, past=past, lq=lq, nb_real=nb_real), grid=(b,),
        in_specs=[pl.BlockSpec((1, KV_WIDTH, LANES), seq3), pl.BlockSpec((1, nbp, KV_WIDTH), seq3),
                  pl.BlockSpec((1, nbp, KV_WIDTH), seq3), pl.BlockSpec((1, lq, w2), seq3),
                  pl.BlockSpec((lq, w2), lambda i: (0, 0)), pl.BlockSpec((1, nwt, KV_WIDTH), seq3),
                  pl.BlockSpec((1, nwt, KV_WIDTH), seq3)],
        out_specs=[pl.BlockSpec((1, LANES, KV_WIDTH), seq3), pl.BlockSpec((1, LANES, KV_WIDTH), seq3),
                   pl.BlockSpec((1, nbp, LANES), seq3)],
        out_shape=[jax.ShapeDtypeStruct((b, LANES, KV_WIDTH), f32), jax.ShapeDtypeStruct((b, LANES, KV_WIDTH), f32),
                   jax.ShapeDtypeStruct((b, nbp, LANES), f32)],
        compiler_params=_cp("arbitrary"), name="nsa_sample_cmp_win")(
            qbd_t, cblk[:, :, :KV_WIDTH].astype(bf16), cblk[:, :, KV_WIDTH:].astype(bf16), kv_c, alpha_rows.T[:lq],
            wkv[:, :, :KV_WIDTH].astype(bf16), wkv[:, :, KV_WIDTH:].astype(bf16))

    n_steps, bps = n_pages // n_pg, n_pg * bpp
    sel_r = sel.transpose(0, 2, 1)
    sel_steps = sel_r[:, :, :past // CMP_BLOCK].reshape(b, LANES, n_steps, bps).transpose(0, 2, 1, 3)
    sel_steps = jnp.pad(sel_steps, ((0, 0), (0, 0), (0, 0), (0, LANES - bps)))
    onehot = (jnp.arange(LANES)[:, None] == jnp.arange(n_pg * PAGE_SIZE)[None, :] // CMP_BLOCK).astype(bf16)
    pad_t = lambda a: jnp.pad(a.transpose(0, 2, 1), ((0, 0), (0, 0), (0, LANES - lq)))
    qoff = jnp.arange(LANES) % lq
    readable = jnp.arange(LANES)[None, :] <= qoff[:, None]
    bias_new = jnp.where(readable[None], sel_r[:, :, past // CMP_BLOCK][:, :, None], NEG)

    def page_map(bb, i, pt, k):
        return (layer, pt[bb, i * n_pg + k], 0, 0, 0, 0)

    seq3p = lambda bb, i, pt: (bb, 0, 0)
    grid_spec = pltpu.PrefetchScalarGridSpec(
        num_scalar_prefetch=1, grid=(b, n_steps),
        in_specs=[pl.BlockSpec((1, 1, 2, KV_HEADS, HEAD_DIM, PAGE_SIZE), functools.partial(page_map, k=k))
                  for k in range(n_pg)]
        + [pl.BlockSpec((1, LANES, KV_WIDTH), seq3p), pl.BlockSpec((1, 1, LANES, LANES), lambda bb, i, pt: (bb, i, 0, 0)),
           pl.BlockSpec(onehot.shape, lambda bb, i, pt: (0, 0)),
           pl.BlockSpec((1, KV_WIDTH, LANES), seq3p), pl.BlockSpec((1, KV_WIDTH, LANES), seq3p),
           pl.BlockSpec((1, LANES, LANES), seq3p), pl.BlockSpec((1, LANES, 3), seq3p),
           pl.BlockSpec((1, LANES, KV_WIDTH), seq3p), pl.BlockSpec((1, LANES, KV_WIDTH), seq3p)],
        out_specs=pl.BlockSpec((1, LANES, KV_WIDTH), seq3p),
        scratch_shapes=[pltpu.VMEM((LANES, 1), f32), pltpu.VMEM((LANES, 1), f32), pltpu.VMEM((LANES, KV_WIDTH), f32)])
    o = pl.pallas_call(
        functools.partial(_smp_sel_kernel, n_pg=n_pg), grid_spec=grid_spec,
        out_shape=jax.ShapeDtypeStruct((b, LANES, KV_WIDTH), f32),
        compiler_params=_cp("arbitrary", "arbitrary"), name="nsa_sample_sel")(
            page_table, *([cache_sel_t] * n_pg), qbd, sel_steps, onehot, pad_t(kv_s[:, :, :KV_WIDTH]),
            pad_t(kv_s[:, :, KV_WIDTH:]), bias_new, gt, oc, ow)
    o = o.reshape(b, KV_HEADS, GROUP, lq, KV_HEADS, HEAD_DIM)
    o = jnp.stack([o[:, h, :, :, h] for h in range(KV_HEADS)], axis=1)
    o = o.transpose(0, 3, 1, 2, 4).reshape(b * lq, d)
    return o, kv_c, kv_s, wkv[:, lq:]


def _hgrn_kernel(zq_ref, zf_ref, zi_ref, zg_ref, lb_ref, ng_ref, s0_ref, o_ref, sout_ref, st_ref, *, n_valid, hb):
    t = pl.program_id(2)
    tb = zq_ref.shape[1]
    n_ch = tb // HG_CHUNK

    @pl.when(t == 0)
    def _():
        for hh in range(hb):
            st_ref[hh] = s0_ref[0, hh].T

    r = lax.broadcasted_iota(i32, (tb, tb), 0)
    cc = lax.broadcasted_iota(i32, (tb, tb), 1)
    tri = jnp.where(r // HG_CHUNK == cc // HG_CHUNK, jnp.where(cc <= r, 1.0, 0.0), 0.0).astype(bf16)
    causal = lax.broadcasted_iota(i32, (HG_CHUNK, HG_CHUNK), 1) <= lax.broadcasted_iota(i32, (HG_CHUNK, HG_CHUNK), 0)
    mid_row = HG_CHUNK // 2

    zf = zf_ref[0]
    lb = lb_ref[0]
    la = jnp.log(lb)
    l1 = jnp.log1p(-lb)
    bb = l1 + _log_sigmoid(zf)
    log_f = jnp.maximum(la, bb) + jnp.log1p(jnp.exp(-jnp.abs(la - bb)))
    kk = jnp.exp(l1 + _log_sigmoid(-zf))
    if n_valid < tb:
        live = lax.broadcasted_iota(i32, zf.shape, 0) < n_valid
        log_f = jnp.where(live, log_f, 0.0)
        kk = jnp.where(live, kk, 0.0)
    hi, mid, lo = _split3(log_f)
    cum = (jnp.dot(tri, hi, preferred_element_type=f32) + jnp.dot(tri, mid, preferred_element_type=f32)
           + jnp.dot(tri, lo, preferred_element_type=f32))
    zq, zi, zg = zq_ref[0], zi_ref[0], zg_ref[0]
    gate = ng_ref[0] * (zg * _sigmoid(zg))

    qs, o_in, upd, dec = {}, {}, {}, {}
    for hh in range(hb):
        hs = slice(hh * HG_DK, (hh + 1) * HG_DK)
        for ci in range(n_ch):
            sl = slice(ci * HG_CHUNK, (ci + 1) * HG_CHUNK)
            cm, qi, ki, vi = cum[sl, hs], zq[sl, hs], kk[sl, hs], zi[sl, hs]
            ref = cm[mid_row:mid_row + 1]
            last = cm[HG_CHUNK - 1:HG_CHUNK]
            a = _dot_nt((qi * jnp.exp(cm - ref)).astype(bf16), (ki * jnp.exp(ref - cm)).astype(bf16))
            a = jnp.where(causal, a, 0.0)
            o_in[hh, ci] = jnp.dot(a.astype(bf16), vi.astype(bf16), preferred_element_type=f32)
            qs[hh, ci] = (qi * jnp.exp(cm)).astype(bf16)
            upd[hh, ci] = _dot_t0(vi.astype(bf16), (ki * jnp.exp(last - cm)).astype(bf16))
            dec[hh, ci] = jnp.exp(last)
    for hh in range(hb):
        hs = slice(hh * HG_DK, (hh + 1) * HG_DK)
        st = st_ref[hh]
        outs = []
        for ci in range(n_ch):
            outs.append(o_in[hh, ci] + _dot_nt(qs[hh, ci], st.astype(bf16)))
            st = st * dec[hh, ci] + upd[hh, ci]
        st_ref[hh] = st
        o = jnp.concatenate(outs, axis=0) if n_ch > 1 else outs[0]
        o = o * lax.rsqrt(jnp.mean(o * o, axis=-1, keepdims=True) + RMS_EPS)
        o_ref[0, :, hs] = (o * gate[:, hs]).astype(o_ref.dtype)

        @pl.when(t == pl.num_programs(2) - 1)
        def _():
            sout_ref[0, hh] = st.T


def _hgrn(p, s0, lb, norm_g, tb, n_valid, hb):
    b, lp, _ = p.shape
    ng = HG_HEADS // hb
    w = hb * HG_DK
    col = lambda off: (lambda bb, hh, t: (bb, t, off * ng + hh))
    per_head = lambda bb, hh, t: (0, hh)
    state = lambda bb, hh, t: (bb, hh, 0, 0)
    return pl.pallas_call(
        functools.partial(_hgrn_kernel, n_valid=n_valid, hb=hb), grid=(b, ng, lp // tb),
        in_specs=[pl.BlockSpec((1, tb, w), col(0)), pl.BlockSpec((1, tb, w), col(1)),
                  pl.BlockSpec((1, tb, w), col(2)), pl.BlockSpec((1, tb, w), col(3)),
                  pl.BlockSpec((1, w), per_head), pl.BlockSpec((1, w), per_head),
                  pl.BlockSpec((1, hb, HG_DK, HG_DV), state)],
        out_specs=[pl.BlockSpec((1, tb, w), lambda bb, hh, t: (bb, t, hh)),
                   pl.BlockSpec((1, hb, HG_DK, HG_DV), state)],
        out_shape=[jax.ShapeDtypeStruct((b, lp, HG_HEADS * HG_DV), bf16),
                   jax.ShapeDtypeStruct((b, HG_HEADS, HG_DK, HG_DV), f32)],
        scratch_shapes=[pltpu.VMEM((hb, HG_DV, HG_DK), f32)],
        compiler_params=_cp("arbitrary", "arbitrary", "arbitrary"), name="hgrn_scan")(
            p, p, p, p, lb.reshape(1, -1), norm_g.reshape(1, -1), s0)


def _hgrn_mixer(x, s0, w_in, lb, norm_g, tb):
    b, seq, d = x.shape
    lp = -(-seq // HG_CHUNK) * HG_CHUNK
    tb = min(tb, lp)
    p = _matmul(x.reshape(b * seq, d), w_in, min(256, b * seq), "hgrn_in_proj").reshape(b, seq, 4 * d)
    if lp != seq:
        p = jnp.pad(p, ((0, 0), (0, lp - seq), (0, 0)))
    o, s_out = _hgrn(p, s0, lb, norm_g, tb, seq if lp != seq else lp, HG_HB)
    return o[:, :seq].reshape(b * seq, d), s_out


def _router_logits(x, whi_ref, wlo_ref, b_ref):
    xh = x.astype(bf16)
    xl = (x - xh.astype(f32)).astype(bf16)
    return (jnp.dot(xh, whi_ref[...], preferred_element_type=f32) + jnp.dot(xl, whi_ref[...], preferred_element_type=f32)
            + jnp.dot(xh, wlo_ref[...], preferred_element_type=f32)) + b_ref[...]


def _route(lg):
    ne = N_GROUPS * EXPERTS_PER_GROUP
    lane = lax.broadcasted_iota(i32, lg.shape, 1)
    lanef = lane.astype(f32)
    in_c = (lane >= ne) & (lane < ne + N_GROUPS)
    lc = jnp.where(in_c, lg, -jnp.inf)
    mc = jnp.max(lc, axis=-1, keepdims=True)
    grp = jnp.min(jnp.where(lc == mc, lanef, 1e9), axis=-1, keepdims=True) - float(ne)
    pg = 1.0 / jnp.sum(jnp.where(in_c, jnp.exp(lc - mc), 0.0), axis=-1, keepdims=True)
    e_lo = grp * float(EXPERTS_PER_GROUP)
    in_e = (lanef >= e_lo) & (lanef < e_lo + float(EXPERTS_PER_GROUP))
    le = jnp.where(in_e, lg, -jnp.inf)
    v1 = jnp.max(le, axis=-1, keepdims=True)
    i1 = jnp.min(jnp.where(le == v1, lanef, 1e9), axis=-1, keepdims=True)
    le2 = jnp.where(lanef == i1, -jnp.inf, le)
    v2 = jnp.max(le2, axis=-1, keepdims=True)
    i2 = jnp.min(jnp.where(le2 == v2, lanef, 1e9), axis=-1, keepdims=True)
    e2 = jnp.exp(v2 - v1)
    w1 = pg / (1.0 + e2)
    return jnp.where(lanef == i1, w1, 0.0) + jnp.where(lanef == i2, w1 * e2, 0.0), grp


def _router_kernel(x_ref, whi_ref, wlo_ref, b_ref, cw_ref):
    cw, _ = _route(_router_logits(x_ref[...], whi_ref, wlo_ref, b_ref))
    for g in range(N_GROUPS):
        cw_ref[g] = cw if g == 0 else pltpu.roll(cw, LANES - g * EXPERTS_PER_GROUP, axis=1)


def _router(x, whi, wlo, bias, tm):
    t, d = x.shape
    return pl.pallas_call(
        _router_kernel, grid=(t // tm,),
        in_specs=[pl.BlockSpec((tm, d), lambda i: (i, 0)), pl.BlockSpec((d, LANES), lambda i: (0, 0)),
                  pl.BlockSpec((d, LANES), lambda i: (0, 0)), pl.BlockSpec((1, LANES), lambda i: (0, 0))],
        out_specs=pl.BlockSpec((N_GROUPS, tm, LANES), lambda i: (0, i, 0)),
        out_shape=jax.ShapeDtypeStruct((N_GROUPS, t, LANES), f32), compiler_params=_cp("arbitrary"),
        name="moe_router")(x, whi, wlo, bias)


def _moe_kernel(x_ref, cw_ref, wg_ref, wu_ref, wd_ref, g_ref, b_ref, o_ref, acc_ref):
    gi = pl.program_id(1)

    @pl.when(gi == 0)
    def _():
        acc_ref[...] = jnp.zeros(acc_ref.shape, f32)

    xb = x_ref[...].astype(bf16)
    cw = cw_ref[0]
    acc = acc_ref[...]
    for e in range(EXPERTS_PER_GROUP):
        hg = jnp.dot(xb, wg_ref[0, e], preferred_element_type=f32)
        hu = jnp.dot(xb, wu_ref[0, e], preferred_element_type=f32)
        h = hg * _sigmoid(hg) * hu * cw[:, e:e + 1]
        acc = acc + jnp.dot(h.astype(bf16), wd_ref[0, e], preferred_element_type=f32)
    acc_ref[...] = acc

    @pl.when(gi == pl.num_programs(1) - 1)
    def _():
        o_ref[...] = _layer_norm(ALPHA * x_ref[...] + acc_ref[...], g_ref[...], b_ref[...])


def _moe_ln(x, cw, wg, wu, wd, g, b, tm):
    t, d = x.shape
    e, hid = wg.shape[1], wg.shape[3]
    row = lambda i, gi: (i, 0)
    fix = lambda i, gi: (0, 0)
    return pl.pallas_call(
        _moe_kernel, grid=(t // tm, N_GROUPS),
        in_specs=[pl.BlockSpec((tm, d), row), pl.BlockSpec((1, tm, LANES), lambda i, gi: (gi, i, 0)),
                  pl.BlockSpec((1, e, d, hid), lambda i, gi: (gi, 0, 0, 0)),
                  pl.BlockSpec((1, e, d, hid), lambda i, gi: (gi, 0, 0, 0)),
                  pl.BlockSpec((1, e, hid, d), lambda i, gi: (gi, 0, 0, 0)),
                  pl.BlockSpec((1, d), fix), pl.BlockSpec((1, d), fix)],
        out_specs=pl.BlockSpec((tm, d), row),
        out_shape=jax.ShapeDtypeStruct((t, d), f32),
        scratch_shapes=[pltpu.VMEM((tm, d), f32)],
        compiler_params=_cp("arbitrary", "arbitrary"), name="moe_dense")(
            x, cw, wg, wu, wd, g.reshape(1, d), b.reshape(1, d))


ROWS_PER_TRIP = 8
META_GROUP, META_RANK = EXPERTS_PER_GROUP, EXPERTS_PER_GROUP + 1


def _router_sort_kernel(x_ref, whi_ref, wlo_ref, b_ref, xc_ref, cnt_ref, carry_ref):
    @pl.when(pl.program_id(0) == 0)
    def _():
        carry_ref[...] = jnp.zeros(carry_ref.shape, f32)

    x = x_ref[...]
    tm, d = x.shape
    cw, grp = _route(_router_logits(x, whi_ref, wlo_ref, b_ref))
    cw8 = cw
    for g in range(1, N_GROUPS):
        cw8 = jnp.where(grp == float(g), pltpu.roll(cw, LANES - g * EXPERTS_PER_GROUP, axis=1), cw8)
    lanef = lax.broadcasted_iota(i32, (tm, LANES), 1).astype(f32)
    onehot = jnp.where(lanef == grp, 1.0, 0.0)
    before = lax.broadcasted_iota(i32, (tm, tm), 1) < lax.broadcasted_iota(i32, (tm, tm), 0)
    prefix = jnp.dot(before.astype(bf16), onehot.astype(bf16), preferred_element_type=f32) + carry_ref[...]
    rank = jnp.sum(onehot * prefix, axis=-1, keepdims=True)
    carry_ref[...] = carry_ref[...] + jnp.sum(onehot, axis=0, keepdims=True)
    cnt_ref[...] = carry_ref[...]
    xc_ref[:, :d] = x
    xc_ref[:, d:] = cw8 + jnp.where(lanef == float(META_GROUP), grp, 0.0) + jnp.where(lanef == float(META_RANK), rank, 0.0)


def _router_sort(x, whi, wlo, bias, tm):
    t, d = x.shape
    fix = lambda i: (0, 0)
    return pl.pallas_call(
        _router_sort_kernel, grid=(t // tm,),
        in_specs=[pl.BlockSpec((tm, d), lambda i: (i, 0)), pl.BlockSpec((d, LANES), fix), pl.BlockSpec((d, LANES), fix),
                  pl.BlockSpec((1, LANES), fix)],
        out_specs=[pl.BlockSpec((tm, d + LANES), lambda i: (i, 0)), pl.BlockSpec((1, LANES), fix)],
        out_shape=[jax.ShapeDtypeStruct((t, d + LANES), f32), jax.ShapeDtypeStruct((1, LANES), f32)],
        scratch_shapes=[pltpu.VMEM((1, LANES), f32)],
        compiler_params=_cp("arbitrary"), name="moe_router_sort")(x, whi, wlo, bias)


def _row_copy(src, i, dst, j, sem):
    return pltpu.make_async_copy(src.at[pl.ds(i, 1)], dst.at[pl.ds(j, 1)], sem)


def _dispatch_kernel(pos_ref, xc_ref, zero_ref, xs_ref, sem):
    del zero_ref
    tm = xc_ref.shape[0]
    base = pl.program_id(0) * tm

    def start(r8, c):
        for j in range(ROWS_PER_TRIP):
            r = r8 * ROWS_PER_TRIP + j
            _row_copy(xc_ref, r, xs_ref, pos_ref[base + r], sem).start(priority=j % 2)
        return c

    def wait(r, c):
        _row_copy(xc_ref, 0, xs_ref, 0, sem).wait()
        return c

    lax.fori_loop(0, tm // ROWS_PER_TRIP, start, 0)
    lax.fori_loop(0, tm, wait, 0, unroll=8)


def _dispatch(xc, pos, n_rows, tm):
    t, w = xc.shape
    grid_spec = pltpu.PrefetchScalarGridSpec(
        num_scalar_prefetch=1, grid=(t // tm,),
        in_specs=[pl.BlockSpec((tm, w), lambda i, pos: (i, 0)), pl.BlockSpec(memory_space=pl.ANY)],
        out_specs=pl.BlockSpec(memory_space=pl.ANY),
        scratch_shapes=[pltpu.SemaphoreType.DMA(())])
    return pl.pallas_call(
        _dispatch_kernel, grid_spec=grid_spec, out_shape=jax.ShapeDtypeStruct((n_rows, w), f32),
        input_output_aliases={2: 0}, compiler_params=_cp("arbitrary"), name="moe_dispatch")(
            pos, xc, jnp.zeros((n_rows, w), f32))


def _moe_sorted_kernel(tg_ref, xs_ref, wg_ref, wu_ref, wd_ref, ys_ref):
    del tg_ref
    d = ys_ref.shape[1]
    xb = xs_ref[:, :d].astype(bf16)
    cw = xs_ref[:, d:]
    acc = jnp.zeros(ys_ref.shape, f32)
    for e in range(EXPERTS_PER_GROUP):
        hg = jnp.dot(xb, wg_ref[0, e], preferred_element_type=f32)
        hu = jnp.dot(xb, wu_ref[0, e], preferred_element_type=f32)
        h = hg * _sigmoid(hg) * hu * cw[:, e:e + 1]
        acc = acc + jnp.dot(h.astype(bf16), wd_ref[0, e], preferred_element_type=f32)
    ys_ref[...] = acc


def _moe_sorted(xs, tile_group, wg, wu, wd, tm):
    n_rows, w = xs.shape
    e, d, hid = wg.shape[1], wg.shape[2], wg.shape[3]
    grp4 = lambda i, tg: (tg[i], 0, 0, 0)
    grid_spec = pltpu.PrefetchScalarGridSpec(
        num_scalar_prefetch=1, grid=(n_rows // tm,),
        in_specs=[pl.BlockSpec((tm, w), lambda i, tg: (i, 0)), pl.BlockSpec((1, e, d, hid), grp4),
                  pl.BlockSpec((1, e, d, hid), grp4), pl.BlockSpec((1, e, hid, d), grp4)],
        out_specs=pl.BlockSpec((tm, d), lambda i, tg: (i, 0)))
    return pl.pallas_call(
        _moe_sorted_kernel, grid_spec=grid_spec, out_shape=jax.ShapeDtypeStruct((n_rows, d), f32),
        compiler_params=_cp("arbitrary"), name="moe_sorted")(tile_group, xs, wg, wu, wd)


def _combine_kernel(pos_ref, x_ref, ys_ref, g_ref, b_ref, o_ref, ybuf, sem):
    tm = x_ref.shape[0]
    i = pl.program_id(0)
    n = pl.num_programs(0)

    def fetch(step, slot):
        def start(r8, c):
            for j in range(ROWS_PER_TRIP):
                r = r8 * ROWS_PER_TRIP + j
                _row_copy(ys_ref, pos_ref[step * tm + r], ybuf.at[slot], r, sem.at[slot]).start(priority=j % 2)
            return c
        lax.fori_loop(0, tm // ROWS_PER_TRIP, start, 0)

    @pl.when(i == 0)
    def _():
        fetch(0, 0)

    @pl.when(i + 1 < n)
    def _():
        fetch(i + 1, (i + 1) % 2)

    slot = i % 2

    def wait(r, c):
        _row_copy(ys_ref, 0, ybuf.at[slot], 0, sem.at[slot]).wait()
        return c

    lax.fori_loop(0, tm, wait, 0, unroll=8)
    o_ref[...] = _layer_norm(ALPHA * x_ref[...] + ybuf[slot], g_ref[...], b_ref[...])


def _combine_ln(x, ys, pos, g, b, tm):
    t, d = x.shape
    fix = lambda i, pos: (0, 0)
    grid_spec = pltpu.PrefetchScalarGridSpec(
        num_scalar_prefetch=1, grid=(t // tm,),
        in_specs=[pl.BlockSpec((tm, d), lambda i, pos: (i, 0)), pl.BlockSpec(memory_space=pl.ANY),
                  pl.BlockSpec((1, d), fix), pl.BlockSpec((1, d), fix)],
        out_specs=pl.BlockSpec((tm, d), lambda i, pos: (i, 0)),
        scratch_shapes=[pltpu.VMEM((2, tm, d), f32), pltpu.SemaphoreType.DMA((2,))])
    return pl.pallas_call(
        _combine_kernel, grid_spec=grid_spec, out_shape=jax.ShapeDtypeStruct((t, d), f32),
        compiler_params=_cp("arbitrary"), name="moe_combine_ln")(pos, x, ys, g.reshape(1, d), b.reshape(1, d))


def _moe_ln_sorted(x, whi, wlo, rbias, wg, wu, wd, g, b, tm=512):
    t, d = x.shape
    xc, cnt = _router_sort(x, whi, wlo, rbias, tm)
    grp = xc[:, d + META_GROUP].astype(i32)
    rank = xc[:, d + META_RANK].astype(i32)
    counts = cnt[0, :N_GROUPS].astype(i32)
    padded = (counts + tm - 1) // tm * tm
    ends = jnp.cumsum(padded)
    start = ends - padded
    pos = rank + sum(jnp.where(grp == gi, start[gi], 0) for gi in range(N_GROUPS))
    n_rows = t + N_GROUPS * tm
    tile_start = jnp.arange(n_rows // tm, dtype=i32) * tm
    tile_group = jnp.minimum(sum((tile_start >= ends[gi]).astype(i32) for gi in range(N_GROUPS)), N_GROUPS - 1)
    xs = _dispatch(xc, pos, n_rows, tm)
    ys = _moe_sorted(xs, tile_group, wg, wu, wd, tm)
    return _combine_ln(x, ys, pos, g, b, tm)


def _router_weights(w_rc, b_rc, w_re, b_re):
    d = w_rc.shape[0]
    ne = N_GROUPS * EXPERTS_PER_GROUP
    w = jnp.concatenate([w_re.reshape(d, ne), w_rc, jnp.zeros((d, LANES - ne - N_GROUPS), f32)], axis=1)
    bias = jnp.concatenate([b_re.reshape(ne), b_rc, jnp.zeros((LANES - ne - N_GROUPS,), f32)]).reshape(1, LANES)
    whi = w.astype(bf16)
    wlo = (w - whi.astype(f32)).astype(bf16)
    return whi, wlo, bias


def _token_major(kv_t):
    b, _, n = kv_t.shape
    return kv_t.reshape(b, 2, KV_HEADS, HEAD_DIM, n).transpose(0, 4, 1, 2, 3)


def kernel(x_prompt, x_sample, cache_cmp_kv, cache_sel_kv, cache_win_kv, state_hgrn, page_table, attn_w_in, attn_cmp_alpha, attn_w_out, rec_w_in, rec_lb_logits, rec_norm_g, rec_w_out, ln_g, ln_b, moe_w_router_c, moe_b_router_c, moe_w_router_e, moe_b_router_e, moe_w_gate, moe_w_up, moe_w_down):
    b, seq, d = x_prompt.shape
    bs, lq, _ = x_sample.shape
    tp, ts = b * seq, bs * lq
    lb_all = jnp.cumsum(jax.nn.softmax(rec_lb_logits.astype(f32), axis=0), axis=0)
    hp, hs = x_prompt.reshape(tp, d), x_sample.reshape(ts, d)
    cache_cmp_t = cache_cmp_kv.transpose(0, 1, 3, 4, 5, 2)
    cache_sel_t = cache_sel_kv.transpose(0, 1, 3, 4, 5, 2)
    kv_out = [[] for _ in range(6)]
    rec_out = [[], []]
    for layer in range(DEPTH):
        j = layer // 2
        if layer % 2 == 0:
            nq, n_in = N_HEADS * HEAD_DIM, attn_w_in.shape[2]
            w_all = attn_w_in[j]
            w_in = jnp.pad(w_all, ((0, 0), (0, -(-n_in // LANES) * LANES - n_in))).astype(bf16)
            w_t = w_all.T.astype(bf16)
            alpha_rows = jnp.repeat(attn_cmp_alpha[j], HEAD_DIM, axis=-1).reshape(CMP_BLOCK, 2 * KV_WIDTH).T
            w_out = attn_w_out[j].astype(bf16)
            op, c_p, s_p, w_p = _nsa_prompt(hp.reshape(b, seq, d), w_t, alpha_rows)
            os_, c_s, s_s, w_s = _nsa_sample(hs.reshape(bs, lq, d), cache_cmp_t, cache_sel_t, cache_win_kv,
                                             page_table, j, w_in, alpha_rows)
            kvs = (bs, lq, 2, KV_HEADS, HEAD_DIM)
            n_win = min(WINDOW, seq)
            for lst, val in zip(kv_out, (_token_major(c_p), c_s.reshape(kvs), _token_major(s_p), s_s.reshape(kvs),
                                         _token_major(w_p[:, :, seq - n_win:]),
                                         w_s.reshape(bs, -1, 2, KV_HEADS, HEAD_DIM))):
                lst.append(val)
        else:
            lb = lb_all[layer] - lb_all[0]
            w_in = rec_w_in[j].astype(bf16)
            w_out = rec_w_out[j].astype(bf16)
            op, st_p = _hgrn_mixer(hp.reshape(b, seq, d), jnp.zeros((b, HG_HEADS, HG_DK, HG_DV), f32), w_in, lb,
                                   rec_norm_g[j], 512)
            os_, st_s = _hgrn_mixer(hs.reshape(bs, lq, d), state_hgrn[j], w_in, lb, rec_norm_g[j], 256)
            rec_out[0].append(st_p)
            rec_out[1].append(st_s)
        hp = _proj_ln(op, w_out, hp, ln_g[layer, 0], ln_b[layer, 0], 512, "out_proj_ln")
        hs = _proj_ln(os_, w_out, hs, ln_g[layer, 0], ln_b[layer, 0], ts, "out_proj_ln_s")
        whi, wlo, rbias = _router_weights(moe_w_router_c[layer], moe_b_router_c[layer], moe_w_router_e[layer],
                                          moe_b_router_e[layer])
        wg, wu, wd = moe_w_gate[layer].astype(bf16), moe_w_up[layer].astype(bf16), moe_w_down[layer].astype(bf16)
        hp = _moe_ln_sorted(hp, whi, wlo, rbias, wg, wu, wd, ln_g[layer, 1], ln_b[layer, 1])
        hs = _moe_ln(hs, _router(hs, whi, wlo, rbias, ts), wg, wu, wd, ln_g[layer, 1], ln_b[layer, 1], ts)
    return (hp.reshape(b, seq, d), hs.reshape(bs, lq, d), *[jnp.stack(v) for v in kv_out],
            jnp.stack(rec_out[0]), jnp.stack(rec_out[1]))
```
